```python
import jax, jax.numpy as jnp
from jax import lax
import numpy as np

D_MODEL = 1024
BATCH = 8
SEQ = 2048
DEPTH = 2

D_CONV = D_MODEL
CONV_WIDTH = 3
SGU_HEADS = 8
SGU_HEAD_DIM = 128
SGU_CHUNK = 128
D_SGU = SGU_HEADS * SGU_HEAD_DIM
ATTN_HEADS = 8
ATTN_HEAD_DIM = 128
D_ATTN = ATTN_HEADS * ATTN_HEAD_DIM
Q_BLOCK = 128
N_BRANCH = 3
D_FF = 2816
RMS_EPS = 1e-6
LN_EPS = 1e-5
FORGET_W_SCALE = 0.1
FORGET_BIAS = 3.0
SPLIT_SIZES = (D_CONV, D_CONV, D_CONV, D_SGU, D_SGU, D_ATTN, D_ATTN, D_ATTN, ATTN_HEADS, N_BRANCH * D_MODEL)
N_IN = 3 * D_CONV + 2 * D_SGU + 3 * D_ATTN + ATTN_HEADS + N_BRANCH * D_MODEL
FORGET_OFFSET = 3 * D_CONV + 2 * D_SGU + 3 * D_ATTN

kernel_name = "hybrid_conv_sgu_fox_macaron"


def rms_norm(x, g):
    xf = x.astype(jnp.float32)
    y = xf * lax.rsqrt(jnp.mean(xf * xf, axis=-1, keepdims=True) + RMS_EPS)
    return (y * g.astype(jnp.float32)).astype(x.dtype)


def layer_norm(x, g, b):
    xf = x.astype(jnp.float32)
    mu = jnp.mean(xf, axis=-1, keepdims=True)
    var = jnp.mean(jnp.square(xf - mu), axis=-1, keepdims=True)
    y = (xf - mu) * lax.rsqrt(var + LN_EPS)
    return (y * g.astype(jnp.float32) + b.astype(jnp.float32)).astype(x.dtype)


def swiglu(x, w_gu, w_down):
    g, u = jnp.split(x @ w_gu, 2, axis=-1)
    return (jax.nn.silu(g) * u) @ w_down


def causal_depthwise_conv(z, w):
    K = w.shape[0]
    zp = jnp.pad(z, ((0, 0), (K - 1, 0), (0, 0)))
    return lax.conv_general_dilated(
        zp, w[:, None, :], window_strides=(1,), padding='VALID',
        dimension_numbers=('NWC', 'WIO', 'NWC'), feature_group_count=z.shape[-1])


def chunked_spatial_gating(u, v, ln_g, ln_b, w_s, b_s):
    bsz, s_len, _ = v.shape
    n_chunks = s_len // SGU_CHUNK
    vn = layer_norm(v, ln_g, ln_b).reshape(bsz, n_chunks, SGU_CHUNK, SGU_HEADS, SGU_HEAD_DIM)
    causal = jnp.tril(jnp.ones((SGU_CHUNK, SGU_CHUNK), dtype=bool))
    w = jnp.where(causal[None], w_s, jnp.zeros((), w_s.dtype))
    s = jnp.einsum('gts,bnsgc->bntgc', w, vn) + b_s.T[None, None, :, :, None]
    return u * s.reshape(bsz, s_len, D_SGU)


def forgetting_attention(q, k, v, log_f):
    c = jnp.cumsum(log_f, axis=1).transpose(0, 2, 1)
    scale = ATTN_HEAD_DIM ** -0.5
    s_len = q.shape[1]
    outs = []
    for i in range(s_len // Q_BLOCK):
        q0 = i * Q_BLOCK
        q1 = q0 + Q_BLOCK
        logits = jnp.einsum('bqhd,bkhd->bhqk', q[:, q0:q1], k[:, :q1]).astype(jnp.float32) * scale
        decay = c[:, :, q0:q1, None] - c[:, :, None, :q1]
        causal = jnp.arange(q1)[None, :] <= jnp.arange(q0, q1)[:, None]
        logits = jnp.where(causal, logits + decay, -jnp.inf)
        p = jax.nn.softmax(logits, axis=-1)
        outs.append(jnp.einsum('bhqk,bkhd->bqhd', p.astype(v.dtype), v[:, :q1]))
    return jnp.concatenate(outs, axis=1)


def setup_inputs(seed: int = 0) -> dict:
    key = jax.random.key(seed)
    ks = jax.random.split(key, 24)
    L, D = DEPTH, D_MODEL

    def dense(k, shape, fan_in):
        return jax.random.normal(k, shape, jnp.float32) * (fan_in ** -0.5)

    def gain(k, shape):
        return 1.0 + 0.02 * jax.random.normal(k, shape, jnp.float32)

    w_in = dense(ks[5], (L, D, N_IN), D)
    w_in = w_in.at[:, :, FORGET_OFFSET:FORGET_OFFSET + ATTN_HEADS].multiply(FORGET_W_SCALE)
    return {
        "x": jax.random.normal(ks[0], (BATCH, SEQ, D), jnp.float32),
        "ffn1_norm": gain(ks[1], (L, D)),
        "ffn1_w_gu": dense(ks[2], (L, D, 2 * D_FF), D),
        "ffn1_w_down": dense(ks[3], (L, D_FF, D), D_FF),
        "mix_norm": gain(ks[4], (L, D)),
        "w_in": w_in,
        "b_forget": FORGET_BIAS + 0.5 * jax.random.normal(ks[6], (L, ATTN_HEADS), jnp.float32),
        "b_gate": 0.02 * jax.random.normal(ks[7], (L, N_BRANCH, D), jnp.float32),
        "conv_w": dense(ks[8], (L, CONV_WIDTH, D_CONV), CONV_WIDTH),
        "sgu_ln_g": gain(ks[9], (L, D_SGU)),
        "sgu_ln_b": 0.02 * jax.random.normal(ks[10], (L, D_SGU), jnp.float32),
        "sgu_w": dense(ks[11], (L, SGU_HEADS, SGU_CHUNK, SGU_CHUNK), SGU_CHUNK),
        "sgu_b": gain(ks[12], (L, SGU_HEADS, SGU_CHUNK)),
        "q_norm_g": gain(ks[13], (L, ATTN_HEADS, ATTN_HEAD_DIM)),
        "k_norm_g": gain(ks[14], (L, ATTN_HEADS, ATTN_HEAD_DIM)),
        "w_out_conv": dense(ks[15], (L, D_CONV, D), D_CONV),
        "w_out_sgu": dense(ks[16], (L, D_SGU, D), D_SGU),
        "w_out_attn": dense(ks[17], (L, D_ATTN, D), D_ATTN),
        "w_o": dense(ks[18], (L, D, D), D),
        "ffn2_norm": gain(ks[19], (L, D)),
        "ffn2_w_gu": dense(ks[20], (L, D, 2 * D_FF), D),
        "ffn2_w_down": dense(ks[21], (L, D_FF, D), D_FF),
    }


def reference(x, ffn1_norm, ffn1_w_gu, ffn1_w_down, mix_norm, w_in, b_forget, b_gate, conv_w,
              sgu_ln_g, sgu_ln_b, sgu_w, sgu_b, q_norm_g, k_norm_g, w_out_conv, w_out_sgu,
              w_out_attn, w_o, ffn2_norm, ffn2_w_gu, ffn2_w_down):
    bsz, s_len, _ = x.shape
    split_idx = []
    acc = 0
    for sz in SPLIT_SIZES[:-1]:
        acc += sz
        split_idx.append(acc)

    for l in range(DEPTH):
        x = x + 0.5 * swiglu(rms_norm(x, ffn1_norm[l]), ffn1_w_gu[l], ffn1_w_down[l])

        h = rms_norm(x, mix_norm[l])
        a_b, a_c, a_x, s_u, s_v, q, k, v, f_logit, g_logit = jnp.split(h @ w_in[l], split_idx, axis=-1)

        y_a = (a_b * causal_depthwise_conv(a_c * a_x, conv_w[l])) @ w_out_conv[l]

        u = jax.nn.gelu(s_u, approximate=False)
        vv = jax.nn.gelu(s_v, approximate=False)
        y_b = chunked_spatial_gating(u, vv, sgu_ln_g[l], sgu_ln_b[l], sgu_w[l], sgu_b[l]) @ w_out_sgu[l]

        q = rms_norm(q.reshape(bsz, s_len, ATTN_HEADS, ATTN_HEAD_DIM), q_norm_g[l])
        k = rms_norm(k.reshape(bsz, s_len, ATTN_HEADS, ATTN_HEAD_DIM), k_norm_g[l])
        v = v.reshape(bsz, s_len, ATTN_HEADS, ATTN_HEAD_DIM)
        log_f = jax.nn.log_sigmoid(f_logit.astype(jnp.float32) + b_forget[l].astype(jnp.float32))
        y_c = forgetting_attention(q, k, v, log_f).reshape(bsz, s_len, D_ATTN) @ w_out_attn[l]

        gates = jax.nn.sigmoid(g_logit.reshape(bsz, s_len, N_BRANCH, D_MODEL) + b_gate[l])
        merged = gates[:, :, 0] * y_a + gates[:, :, 1] * y_b + gates[:, :, 2] * y_c
        x = x + merged @ w_o[l]

        x = x + 0.5 * swiglu(rms_norm(x, ffn2_norm[l]), ffn2_w_gu[l], ffn2_w_down[l])
    return x
```

```python
import functools
import math

import jax
import jax.numpy as jnp
from jax import lax
from jax.experimental import pallas as pl
from jax.experimental.pallas import tpu as pltpu

F32 = jnp.float32
BF16 = jnp.bfloat16

RMS_EPS = 1e-6
LN_EPS = 1e-5
CONV_WIDTH = 3
SGU_CHUNK = 128
HEAD_DIM = 128
N_BRANCH = 3
MASKED = -1e30

V7X_LANES = 128
V7X_SUBLANES = 8
V7X_VMEM_BYTES = 64 * 1024 * 1024
FF_CHUNK = 1024

TOKEN_TILE_FFN = 512
TOKEN_TILE_MIX = 256
ATTN_BLOCK = 256


def _resident(shape):
    zeros = (0,) * len(shape)
    return pl.BlockSpec(shape, lambda *_: zeros, pipeline_mode=pl.Buffered(1))


def _nbytes(shape, dtype):
    return math.prod(shape) * jnp.dtype(dtype).itemsize


def _params(semantics, vmem_bytes):
    assert vmem_bytes <= V7X_VMEM_BYTES, vmem_bytes
    return pltpu.CompilerParams(dimension_semantics=semantics, vmem_limit_bytes=int(vmem_bytes))


def _rms_norm(x, g):
    return x * lax.rsqrt(jnp.mean(x * x, axis=-1, keepdims=True) + RMS_EPS) * g


def _gelu(x):
    return 0.5 * x * (1.0 + lax.erf(x * (2.0 ** -0.5)))


def _ff_chunks(d_ff):
    return [(c, min(c + FF_CHUNK, d_ff)) for c in range(0, d_ff, FF_CHUNK)]


def _swiglu_half_step(x, g_ref, wg_ref, wu_ref, wd_ref, act_ref):
    h = _rms_norm(x, g_ref[...]).astype(BF16)
    for c0, c1 in _ff_chunks(wg_ref.shape[1]):
        g = jnp.dot(h, wg_ref[:, c0:c1], preferred_element_type=F32)
        u = jnp.dot(h, wu_ref[:, c0:c1], preferred_element_type=F32)
        act_ref[:, c0:c1] = (g * jax.nn.sigmoid(g) * u).astype(BF16)
    return x + 0.5 * jnp.dot(act_ref[...], wd_ref[...], preferred_element_type=F32)


def _ffn_kernel(x_ref, g_ref, wg_ref, wu_ref, wd_ref, o_ref, act_ref):
    o_ref[...] = _swiglu_half_step(x_ref[...], g_ref, wg_ref, wu_ref, wd_ref, act_ref)


def _ffn(x, norm_g, wg, wu, wd):
    t, d = x.shape
    d_ff = wg.shape[1]
    tm = TOKEN_TILE_FFN
    assert t % tm == 0
    tile = pl.BlockSpec((tm, d), lambda i: (i, 0))
    vmem = (_nbytes((d, 2 * d_ff), BF16) + _nbytes((d_ff, d), BF16)
            + 4 * _nbytes((tm, d), F32)
            + _nbytes((tm, d_ff), BF16)
            + 6 * _nbytes((tm, FF_CHUNK), F32))
    return pl.pallas_call(
        _ffn_kernel,
        grid=(t // tm,),
        in_specs=[tile, _resident((1, d)), _resident(wg.shape), _resident(wu.shape), _resident(wd.shape)],
        out_specs=tile,
        out_shape=jax.ShapeDtypeStruct((t, d), F32),
        scratch_shapes=[pltpu.VMEM((tm, d_ff), BF16)],
        compiler_params=_params(("arbitrary",), vmem),
        name="ffn",
    )(x, norm_g, wg, wu, wd)


def _mixer_in_kernel(x_ref, ng_ref, wm_ref, wf_ref, wgt_ref, bf_ref, bg_ref, cw_ref, lng_ref, lnb_ref,
                     sw_ref, sb_ref, qg_ref, kg_ref, woc_ref, wos_ref,
                     part_ref, g2_ref, q_ref, k_ref, v_ref, c_ref,
                     zs_ref, ccarry_ref, vn_ref, yb_ref, *, tiles_per_seq, q_scale, c_scale):
    tm, d = x_ref.shape
    heads = d // HEAD_DIM
    pad = V7X_SUBLANES

    @pl.when(pl.program_id(0) % tiles_per_seq == 0)
    def _():
        zs_ref[0:pad, :] = jnp.zeros((pad, d), F32)
        ccarry_ref[...] = jnp.zeros_like(ccarry_ref)

    h = _rms_norm(x_ref[...], ng_ref[...]).astype(BF16)

    def proj(w_ref, c0, c1):
        return jnp.dot(h, w_ref[:, c0:c1], preferred_element_type=F32)

    def gate(b):
        return jax.nn.sigmoid(proj(wgt_ref, b * d, (b + 1) * d) + bg_ref[:, b * d:(b + 1) * d])

    pa = proj(wm_ref, 0, 3 * d)
    zs_ref[pad:pad + tm, :] = pa[:, d:2 * d] * pa[:, 2 * d:3 * d]
    conv = (cw_ref[0:1, :] * zs_ref[pad - 2:pad - 2 + tm, :]
            + cw_ref[1:2, :] * zs_ref[pad - 1:pad - 1 + tm, :]
            + cw_ref[2:3, :] * zs_ref[pad:pad + tm, :])
    ya = jnp.dot((pa[:, 0:d] * conv).astype(BF16), woc_ref[...], preferred_element_type=F32)
    zs_ref[0:pad, :] = zs_ref[tm:tm + pad, :]
    part = gate(0) * ya

    ps = proj(wm_ref, 3 * d, 5 * d)
    u = _gelu(ps[:, 0:d])
    vv = _gelu(ps[:, d:2 * d])
    mu = jnp.mean(vv, axis=-1, keepdims=True)
    vc = vv - mu
    var = jnp.mean(vc * vc, axis=-1, keepdims=True)
    vn_ref[...] = (vc * lax.rsqrt(var + LN_EPS) * lng_ref[...] + lnb_ref[...]).astype(BF16)
    n_chunks = tm // SGU_CHUNK
    pos_t = lax.broadcasted_iota(jnp.int32, (SGU_CHUNK, SGU_CHUNK), 0)
    pos_s = lax.broadcasted_iota(jnp.int32, (SGU_CHUNK, SGU_CHUNK), 1)
    for g in range(heads):
        hs = slice(g * HEAD_DIM, (g + 1) * HEAD_DIM)
        w = jnp.where(pos_s <= pos_t, sw_ref[g], 0.0).astype(BF16)
        rhs = jnp.concatenate([vn_ref[c * SGU_CHUNK:(c + 1) * SGU_CHUNK, hs] for c in range(n_chunks)], axis=1)
        s = jnp.dot(w, rhs, preferred_element_type=F32) + sb_ref[:, g:g + 1]
        for c in range(n_chunks):
            rows = slice(c * SGU_CHUNK, (c + 1) * SGU_CHUNK)
            yb_ref[rows, hs] = (u[rows, hs] * s[:, c * SGU_CHUNK:(c + 1) * SGU_CHUNK]).astype(BF16)
    yb = jnp.dot(yb_ref[...], wos_ref[...], preferred_element_type=F32)
    part_ref[...] = part + gate(1) * yb
    g2_ref[...] = gate(2)

    pq = proj(wm_ref, 5 * d, 8 * d)
    for g in range(heads):
        hs = slice(g * HEAD_DIM, (g + 1) * HEAD_DIM)
        q_ref[:, hs] = (_rms_norm(pq[:, hs], qg_ref[:, hs]) * q_scale).astype(BF16)
        ks = slice(d + g * HEAD_DIM, d + (g + 1) * HEAD_DIM)
        k_ref[:, hs] = _rms_norm(pq[:, ks], kg_ref[:, hs]).astype(BF16)
    v_ref[...] = pq[:, 2 * d:3 * d].astype(BF16)

    f = jnp.dot(h, wf_ref[...], preferred_element_type=F32) + bf_ref[...]
    c = jnp.minimum(f, 0.0) - jnp.log1p(jnp.exp(-jnp.abs(f)))
    t_idx = lax.broadcasted_iota(jnp.int32, c.shape, 0)
    shift = 1
    while shift < tm:
        c = c + jnp.where(t_idx >= shift, pltpu.roll(c, shift, axis=0), 0.0)
        shift *= 2
    c = c + ccarry_ref[0:1, :]
    ccarry_ref[0:1, :] = c[tm - 1:tm, :]
    c_ref[...] = c * c_scale


def _mixer_in(x, p, *, seq_len):
    t, d = x.shape
    tm = TOKEN_TILE_MIX
    assert seq_len % tm == 0 and tm % SGU_CHUNK == 0 and d % HEAD_DIM == 0
    tile = lambda dtype_cols: pl.BlockSpec((tm, dtype_cols), lambda i: (i, 0))
    kernel = functools.partial(
        _mixer_in_kernel, tiles_per_seq=seq_len // tm,
        q_scale=HEAD_DIM ** -0.5 * math.log2(math.e), c_scale=math.log2(math.e))
    weights = [p["w_main"], p["w_forget"], p["w_gate"], p["b_forget"], p["b_gate"], p["conv_w"],
               p["sgu_ln_g"], p["sgu_ln_b"], p["sgu_w"], p["sgu_b_t"], p["q_norm_g"], p["k_norm_g"],
               p["w_out_conv"], p["w_out_sgu"]]
    vmem = (sum(_nbytes(w.shape, w.dtype) for w in weights)
            + 2 * (3 * _nbytes((tm, d), F32) + 3 * _nbytes((tm, d), BF16) + _nbytes((tm, V7X_LANES), F32))
            + _nbytes((tm + V7X_SUBLANES, d), F32) + 2 * _nbytes((tm, d), BF16)
            + 12 * _nbytes((tm, d), F32))
    return pl.pallas_call(
        kernel,
        grid=(t // tm,),
        in_specs=[tile(d), _resident((1, d))] + [_resident(w.shape) for w in weights],
        out_specs=[tile(d), tile(d), tile(d), tile(d), tile(d), tile(V7X_LANES)],
        out_shape=[jax.ShapeDtypeStruct((t, d), F32), jax.ShapeDtypeStruct((t, d), F32),
                   jax.ShapeDtypeStruct((t, d), BF16), jax.ShapeDtypeStruct((t, d), BF16),
                   jax.ShapeDtypeStruct((t, d), BF16), jax.ShapeDtypeStruct((t, V7X_LANES), F32)],
        scratch_shapes=[pltpu.VMEM((tm + V7X_SUBLANES, d), F32), pltpu.VMEM((V7X_SUBLANES, V7X_LANES), F32),
                        pltpu.VMEM((tm, d), BF16), pltpu.VMEM((tm, d), BF16)],
        compiler_params=_params(("arbitrary",), vmem),
        name="mixer_in",
    )(x, p["mix_norm"], *weights)


def _attn_kernel(q_ref, k_ref, v_ref, ccol_ref, crow_ref, o_ref, *, blocks_per_seq):
    tq, d = q_ref.shape
    i = pl.program_id(1)
    q_pos = lax.broadcasted_iota(jnp.int32, (tq, tq), 0)
    k_pos = lax.broadcasted_iota(jnp.int32, (tq, tq), 1)
    causal = k_pos <= q_pos

    for h in range(d // HEAD_DIM):
        hs = slice(h * HEAD_DIM, (h + 1) * HEAD_DIM)
        q = q_ref[:, hs]
        cq = ccol_ref[:, h:h + 1]

        def block(j, carry, diagonal):
            m, l, acc = carry
            rows = pl.ds(pl.multiple_of(j * tq, tq), tq)
            ck = crow_ref[pl.ds(h * blocks_per_seq + j, 1), :]
            s = lax.dot_general(q, k_ref[rows, hs], (((1,), (1,)), ((), ())), preferred_element_type=F32) - ck
            if diagonal:
                s = jnp.where(causal, s, MASKED)
            m_new = jnp.maximum(m, jnp.max(s, axis=-1, keepdims=True) + cq)
            p = jnp.exp2(s + (cq - m_new))
            alpha = jnp.exp2(m - m_new)
            l = alpha * l + jnp.sum(p, axis=-1, keepdims=True)
            acc = alpha * acc + jnp.dot(p.astype(BF16), v_ref[rows, hs], preferred_element_type=F32)
            return m_new, l, acc

        init = (jnp.full((tq, 1), MASKED, F32), jnp.zeros((tq, 1), F32), jnp.zeros((tq, HEAD_DIM), F32))
        carry = lax.fori_loop(0, i, lambda j, c: block(j, c, False), init)
        _, l, acc = block(i, carry, True)
        o_ref[:, hs] = (acc * (1.0 / l)).astype(BF16)


def _attention(q, k, v, c_cols, c_rows, *, batch, seq_len):
    t, d = q.shape
    tq = ATTN_BLOCK
    assert seq_len % tq == 0
    nq = seq_len // tq
    heads = d // HEAD_DIM
    q_tile = pl.BlockSpec((tq, d), lambda b, i: (b * nq + i, 0))
    kv_seq = pl.BlockSpec((seq_len, d), lambda b, i: (b, 0))
    vmem = (4 * _nbytes((seq_len, d), BF16) + 4 * _nbytes((tq, d), BF16)
            + 2 * _nbytes((tq, V7X_LANES), F32) + 2 * _nbytes((heads * nq, tq), F32)
            + 8 * _nbytes((tq, tq), F32))
    return pl.pallas_call(
        functools.partial(_attn_kernel, blocks_per_seq=nq),
        grid=(batch, nq),
        in_specs=[q_tile, kv_seq, kv_seq,
                  pl.BlockSpec((tq, V7X_LANES), lambda b, i: (b * nq + i, 0)),
                  pl.BlockSpec((None, heads * nq, tq), lambda b, i: (b, 0, 0))],
        out_specs=q_tile,
        out_shape=jax.ShapeDtypeStruct((t, d), BF16),
        compiler_params=_params(("arbitrary", "arbitrary"), vmem),
        name="attention",
    )(q, k, v, c_cols, c_rows)


def _mixer_out_kernel(x_ref, a_ref, part_ref, g2_ref, woa_ref, wo_ref, g_ref, wg_ref, wu_ref, wd_ref,
                      o_ref, act_ref):
    yc = jnp.dot(a_ref[...], woa_ref[...], preferred_element_type=F32)
    merged = part_ref[...] + g2_ref[...] * yc
    x = x_ref[...] + jnp.dot(merged.astype(BF16), wo_ref[...], preferred_element_type=F32)
    o_ref[...] = _swiglu_half_step(x, g_ref, wg_ref, wu_ref, wd_ref, act_ref)


def _mixer_out(x, attn, part, g2, w_out_attn, w_o, norm_g, wg, wu, wd):
    t, d = x.shape
    d_ff = wg.shape[1]
    tm = TOKEN_TILE_FFN
    assert t % tm == 0
    tile = pl.BlockSpec((tm, d), lambda i: (i, 0))
    vmem = (_nbytes((d, 2 * d_ff), BF16) + _nbytes((d_ff, d), BF16) + 2 * _nbytes((d, d), BF16)
            + 2 * (4 * _nbytes((tm, d), F32) + _nbytes((tm, d), BF16))
            + _nbytes((tm, d_ff), BF16)
            + 6 * _nbytes((tm, FF_CHUNK), F32) + 4 * _nbytes((tm, d), F32))
    return pl.pallas_call(
        _mixer_out_kernel,
        grid=(t // tm,),
        in_specs=[tile, tile, tile, tile, _resident(w_out_attn.shape), _resident(w_o.shape),
                  _resident((1, d)), _resident(wg.shape), _resident(wu.shape), _resident(wd.shape)],
        out_specs=tile,
        out_shape=jax.ShapeDtypeStruct((t, d), F32),
        scratch_shapes=[pltpu.VMEM((tm, d_ff), BF16)],
        compiler_params=_params(("arbitrary",), vmem),
        name="mixer_out",
    )(x, attn, part, g2, w_out_attn, w_o, norm_g, wg, wu, wd)


def _row(v):
    return v.reshape(1, -1)


def kernel(x, ffn1_norm, ffn1_w_gu, ffn1_w_down, mix_norm, w_in, b_forget, b_gate, conv_w, sgu_ln_g, sgu_ln_b,
           sgu_w, sgu_b, q_norm_g, k_norm_g, w_out_conv, w_out_sgu, w_out_attn, w_o, ffn2_norm, ffn2_w_gu,
           ffn2_w_down):
    batch, seq_len, d = x.shape
    depth = w_in.shape[0]
    d_ff = ffn1_w_down.shape[1]
    heads = d // HEAD_DIM
    n_main = 8 * d
    bf = lambda a: a.astype(BF16)

    xt = x.reshape(batch * seq_len, d)
    for l in range(depth):
        xt = _ffn(xt, _row(ffn1_norm[l]), bf(ffn1_w_gu[l][:, :d_ff]), bf(ffn1_w_gu[l][:, d_ff:]),
                  bf(ffn1_w_down[l]))
        p = {
            "mix_norm": _row(mix_norm[l]),
            "w_main": bf(w_in[l][:, :n_main]),
            "w_forget": bf(jnp.pad(w_in[l][:, n_main:n_main + heads], ((0, 0), (0, V7X_LANES - heads)))),
            "w_gate": bf(w_in[l][:, n_main + heads:]),
            "b_forget": jnp.pad(_row(b_forget[l]), ((0, 0), (0, V7X_LANES - heads))),
            "b_gate": _row(b_gate[l]),
            "conv_w": conv_w[l],
            "sgu_ln_g": _row(sgu_ln_g[l]),
            "sgu_ln_b": _row(sgu_ln_b[l]),
            "sgu_w": sgu_w[l],
            "sgu_b_t": sgu_b[l].T,
            "q_norm_g": _row(q_norm_g[l]),
            "k_norm_g": _row(k_norm_g[l]),
            "w_out_conv": bf(w_out_conv[l]),
            "w_out_sgu": bf(w_out_sgu[l]),
        }
        part, g2, q, k, v, c = _mixer_in(xt, p, seq_len=seq_len)
        c_rows = c[:, :heads].reshape(batch, seq_len, heads).transpose(0, 2, 1)
        c_rows = c_rows.reshape(batch, heads * (seq_len // ATTN_BLOCK), ATTN_BLOCK)
        attn = _attention(q, k, v, c, c_rows, batch=batch, seq_len=seq_len)
        xt = _mixer_out(xt, attn, part, g2, bf(w_out_attn[l]), bf(w_o[l]), _row(ffn2_norm[l]),
                        bf(ffn2_w_gu[l][:, :d_ff]), bf(ffn2_w_gu[l][:, d_ff:]), bf(ffn2_w_down[l]))
    return xt.reshape(batch, seq_len, d)
```

```python
import functools
import math

import jax
import jax.numpy as jnp
from jax import lax
from jax.experimental import pallas as pl
from jax.experimental.pallas import tpu as pltpu

F32 = jnp.float32
BF16 = jnp.bfloat16

RMS_EPS = 1e-6
LN_EPS = 1e-5
CONV_WIDTH = 3
SGU_CHUNK = 128
HEAD_DIM = 128
N_BRANCH = 3
MASKED = -1e30

V7X_LANES = 128
V7X_SUBLANES = 8
V7X_VMEM_BYTES = 64 * 1024 * 1024
FF_CHUNK = 1024

TOKEN_TILE_FFN = 512
TOKEN_TILE_MIX = 256
ATTN_BLOCK = 256


def _resident(shape):
    zeros = (0,) * len(shape)
    return pl.BlockSpec(shape, lambda *_: zeros, pipeline_mode=pl.Buffered(1))


def _nbytes(shape, dtype):
    return math.prod(shape) * jnp.dtype(dtype).itemsize


def _params(semantics, vmem_bytes):
    assert vmem_bytes <= V7X_VMEM_BYTES, vmem_bytes
    return pltpu.CompilerParams(dimension_semantics=semantics, vmem_limit_bytes=int(vmem_bytes))


def _rms_norm(x, g):
    return x * lax.rsqrt(jnp.mean(x * x, axis=-1, keepdims=True) + RMS_EPS) * g


def _gelu(x):
    return 0.5 * x * (1.0 + lax.erf(x * (2.0 ** -0.5)))


def _ff_chunks(d_ff):
    return [(c, min(c + FF_CHUNK, d_ff)) for c in range(0, d_ff, FF_CHUNK)]


def _swiglu_half_step(x, g_ref, wg_ref, wu_ref, wd_ref, act_ref):
    h = _rms_norm(x, g_ref[...]).astype(BF16)
    for c0, c1 in _ff_chunks(wg_ref.shape[1]):
        g = jnp.dot(h, wg_ref[:, c0:c1], preferred_element_type=F32)
        u = jnp.dot(h, wu_ref[:, c0:c1], preferred_element_type=F32)
        act_ref[:, c0:c1] = (g * jax.nn.sigmoid(g) * u).astype(BF16)
    return x + 0.5 * jnp.dot(act_ref[...], wd_ref[...], preferred_element_type=F32)


def _ffn_kernel(x_ref, g_ref, wg_ref, wu_ref, wd_ref, o_ref, act_ref):
    o_ref[...] = _swiglu_half_step(x_ref[...], g_ref, wg_ref, wu_ref, wd_ref, act_ref)


def _ffn(x, norm_g, wg, wu, wd):
    t, d = x.shape
    d_ff = wg.shape[1]
    tm = TOKEN_TILE_FFN
    assert t % tm == 0
    tile = pl.BlockSpec((tm, d), lambda i: (i, 0))
    vmem = (_nbytes((d, 2 * d_ff), BF16) + _nbytes((d_ff, d), BF16)
            + 4 * _nbytes((tm, d), F32)
            + _nbytes((tm, d_ff), BF16)
            + 6 * _nbytes((tm, FF_CHUNK), F32))
    return pl.pallas_call(
        _ffn_kernel,
        grid=(t // tm,),
        in_specs=[tile, _resident((1, d)), _resident(wg.shape), _resident(wu.shape), _resident(wd.shape)],
        out_specs=tile,
        out_shape=jax.ShapeDtypeStruct((t, d), F32),
        scratch_shapes=[pltpu.VMEM((tm, d_ff), BF16)],
        compiler_params=_params(("arbitrary",), vmem),
        name="ffn",
    )(x, norm_g, wg, wu, wd)


def _mixer_in_kernel(x_ref, ng_ref, wm_ref, wf_ref, wgt_ref, bf_ref, bg_ref, cw_ref, lng_ref, lnb_ref,
                     sw_ref, sb_ref, qg_ref, kg_ref, woc_ref, wos_ref,
                     part_ref, g2_ref, q_ref, k_ref, v_ref, c_ref,
                     zs_ref, ccarry_ref, vn_ref, yb_ref, *, tiles_per_seq, q_scale, c_scale):
    tm, d = x_ref.shape
    heads = d // HEAD_DIM
    pad = V7X_SUBLANES

    @pl.when(pl.program_id(0) % tiles_per_seq == 0)
    def _():
        zs_ref[0:pad, :] = jnp.zeros((pad, d), F32)
        ccarry_ref[...] = jnp.zeros_like(ccarry_ref)

    h = _rms_norm(x_ref[...], ng_ref[...]).astype(BF16)

    def proj(w_ref, c0, c1):
        return jnp.dot(h, w_ref[:, c0:c1], preferred_element_type=F32)

    def gate(b):
        return jax.nn.sigmoid(proj(wgt_ref, b * d, (b + 1) * d) + bg_ref[:, b * d:(b + 1) * d])

    pa = proj(wm_ref, 0, 3 * d)
    zs_ref[pad:pad + tm, :] = pa[:, d:2 * d] * pa[:, 2 * d:3 * d]
    conv = (cw_ref[0:1, :] * zs_ref[pad - 2:pad - 2 + tm, :]
            + cw_ref[1:2, :] * zs_ref[pad - 1:pad - 1 + tm, :]
            + cw_ref[2:3, :] * zs_ref[pad:pad + tm, :])
    ya = jnp.dot((pa[:, 0:d] * conv).astype(BF16), woc_ref[...], preferred_element_type=F32)
    zs_ref[0:pad, :] = zs_ref[tm:tm + pad, :]
    part = gate(0) * ya

    ps = proj(wm_ref, 3 * d, 5 * d)
    u = _gelu(ps[:, 0:d])
    vv = _gelu(ps[:, d:2 * d])
    mu = jnp.mean(vv, axis=-1, keepdims=True)
    vc = vv - mu
    var = jnp.mean(vc * vc, axis=-1, keepdims=True)
    vn_ref[...] = (vc * lax.rsqrt(var + LN_EPS) * lng_ref[...] + lnb_ref[...]).astype(BF16)
    n_chunks = tm // SGU_CHUNK
    pos_t = lax.broadcasted_iota(jnp.int32, (SGU_CHUNK, SGU_CHUNK), 0)
    pos_s = lax.broadcasted_iota(jnp.int32, (SGU_CHUNK, SGU_CHUNK), 1)
    for g in range(heads):
        hs = slice(g * HEAD_DIM, (g + 1) * HEAD_DIM)
        w = jnp.where(pos_s <= pos_t, sw_ref[g], 0.0).astype(BF16)
        rhs = jnp.concatenate([vn_ref[c * SGU_CHUNK:(c + 1) * SGU_CHUNK, hs] for c in range(n_chunks)], axis=1)
        s = jnp.dot(w, rhs, preferred_element_type=F32) + sb_ref[:, g:g + 1]
        for c in range(n_chunks):
            rows = slice(c * SGU_CHUNK, (c + 1) * SGU_CHUNK)
            yb_ref[rows, hs] = (u[rows, hs] * s[:, c * SGU_CHUNK:(c + 1) * SGU_CHUNK]).astype(BF16)
    yb = jnp.dot(yb_ref[...], wos_ref[...], preferred_element_type=F32)
    part_ref[...] = part + gate(1) * yb
    g2_ref[...] = gate(2)

    pq = proj(wm_ref, 5 * d, 8 * d)
    for g in range(heads):
        hs = slice(g * HEAD_DIM, (g + 1) * HEAD_DIM)
        q_ref[:, hs] = (_rms_norm(pq[:, hs], qg_ref[:, hs]) * q_scale).astype(BF16)
        ks = slice(d + g * HEAD_DIM, d + (g + 1) * HEAD_DIM)
        k_ref[:, hs] = _rms_norm(pq[:, ks], kg_ref[:, hs]).astype(BF16)
    v_ref[...] = pq[:, 2 * d:3 * d].astype(BF16)

    f = jnp.dot(h, wf_ref[...], preferred_element_type=F32) + bf_ref[...]
    c = jnp.minimum(f, 0.0) - jnp.log1p(jnp.exp(-jnp.abs(f)))
    t_idx = lax.broadcasted_iota(jnp.int32, c.shape, 0)
    shift = 1
    while shift < tm:
        c = c + jnp.where(t_idx >= shift, pltpu.roll(c, shift, axis=0), 0.0)
        shift *= 2
    c = c + ccarry_ref[0:1, :]
    ccarry_ref[0:1, :] = c[tm - 1:tm, :]
    c_ref[...] = c * c_scale


def _mixer_in(x, p, *, seq_len):
    t, d = x.shape
    tm = TOKEN_TILE_MIX
    assert seq_len % tm == 0 and tm % SGU_CHUNK == 0 and d % HEAD_DIM == 0
    tile = lambda dtype_cols: pl.BlockSpec((tm, dtype_cols), lambda i: (i, 0))
    kernel = functools.partial(
        _mixer_in_kernel, tiles_per_seq=seq_len // tm,
        q_scale=HEAD_DIM ** -0.5 * math.log2(math.e), c_scale=math.log2(math.e))
    weights = [p["w_main"], p["w_forget"], p["w_gate"], p["b_forget"], p["b_gate"], p["conv_w"],
               p["sgu_ln_g"], p["sgu_ln_b"], p["sgu_w"], p["sgu_b_t"], p["q_norm_g"], p["k_norm_g"],
               p["w_out_conv"], p["w_out_sgu"]]
    vmem = (sum(_nbytes(w.shape, w.dtype) for w in weights)
            + 2 * (3 * _nbytes((tm, d), F32) + 3 * _nbytes((tm, d), BF16) + _nbytes((tm, V7X_LANES), F32))
            + _nbytes((tm + V7X_SUBLANES, d), F32) + 2 * _nbytes((tm, d), BF16)
            + 12 * _nbytes((tm, d), F32))
    return pl.pallas_call(
        kernel,
        grid=(t // tm,),
        in_specs=[tile(d), _resident((1, d))] + [_resident(w.shape) for w in weights],
        out_specs=[tile(d), tile(d), tile(d), tile(d), tile(d), tile(V7X_LANES)],
        out_shape=[jax.ShapeDtypeStruct((t, d), F32), jax.ShapeDtypeStruct((t, d), F32),
                   jax.ShapeDtypeStruct((t, d), BF16), jax.ShapeDtypeStruct((t, d), BF16),
                   jax.ShapeDtypeStruct((t, d), BF16), jax.ShapeDtypeStruct((t, V7X_LANES), F32)],
        scratch_shapes=[pltpu.VMEM((tm + V7X_SUBLANES, d), F32), pltpu.VMEM((V7X_SUBLANES, V7X_LANES), F32),
                        pltpu.VMEM((tm, d), BF16), pltpu.VMEM((tm, d), BF16)],
        compiler_params=_params(("arbitrary",), vmem),
        name="mixer_in",
    )(x, p["mix_norm"], *weights)


def _lane_groups(x):
    return [x[:, c:c + V7X_LANES] for c in range(0, x.shape[1], V7X_LANES)]


def _attn_kernel(q_ref, k_ref, v_ref, ccol_ref, crow_ref, o_ref, s_ref, stat_ref, l_ref, acc_ref, *,
                 blocks_per_seq):
    tq, d = q_ref.shape
    heads = d // HEAD_DIM
    i = pl.program_id(1)
    q_pos = lax.broadcasted_iota(jnp.int32, (tq, tq), 0)
    k_pos = lax.broadcasted_iota(jnp.int32, (tq, tq), 1)
    causal = k_pos <= q_pos

    def key_rows(j):
        return pl.ds(pl.multiple_of(j * tq, tq), tq)

    def logits(j, diagonal):
        for h in range(heads):
            hs = slice(h * HEAD_DIM, (h + 1) * HEAD_DIM)
            ck = crow_ref[pl.ds(h * blocks_per_seq + j, 1), :]
            s = lax.dot_general(q_ref[:, hs], k_ref[key_rows(j), hs], (((1,), (1,)), ((), ())),
                                preferred_element_type=F32) - ck
            if diagonal:
                s = jnp.where(causal, s, MASKED)
            s_ref[j, h] = s
            stat_ref[h] = functools.reduce(jnp.maximum, _lane_groups(s), stat_ref[h])

    def values(j):
        for h in range(heads):
            hs = slice(h * HEAD_DIM, (h + 1) * HEAD_DIM)
            shift = stat_ref[h]
            p = jnp.exp2(s_ref[j, h] + jnp.concatenate([shift] * (tq // V7X_LANES), axis=1))
            l_ref[h] = functools.reduce(jnp.add, _lane_groups(p), l_ref[h])
            acc_ref[h] += jnp.dot(p.astype(BF16), v_ref[key_rows(j), hs], preferred_element_type=F32)

    stat_ref[...] = jnp.full(stat_ref.shape, MASKED, F32)
    l_ref[...] = jnp.zeros(l_ref.shape, F32)
    acc_ref[...] = jnp.zeros(acc_ref.shape, F32)

    lax.fori_loop(0, i, lambda j, c: (logits(j, False), c)[1], 0)
    logits(i, True)
    for h in range(heads):
        cq = ccol_ref[:, h:h + 1]
        m = jnp.max(stat_ref[h], axis=-1, keepdims=True) + cq
        stat_ref[h] = jnp.broadcast_to(cq - m, (tq, V7X_LANES))
    lax.fori_loop(0, i + 1, lambda j, c: (values(j), c)[1], 0)
    for h in range(heads):
        hs = slice(h * HEAD_DIM, (h + 1) * HEAD_DIM)
        l = jnp.sum(l_ref[h], axis=-1, keepdims=True)
        o_ref[:, hs] = (acc_ref[h] * (1.0 / l)).astype(BF16)


def _attention(q, k, v, c_cols, c_rows, *, batch, seq_len):
    t, d = q.shape
    tq = ATTN_BLOCK
    assert seq_len % tq == 0 and tq % V7X_LANES == 0
    nq = seq_len // tq
    heads = d // HEAD_DIM
    q_tile = pl.BlockSpec((tq, d), lambda b, i: (b * nq + i, 0))
    kv_seq = pl.BlockSpec((seq_len, d), lambda b, i: (b, 0))
    scratch = [pltpu.VMEM((nq, heads, tq, tq), F32), pltpu.VMEM((heads, tq, V7X_LANES), F32),
               pltpu.VMEM((heads, tq, V7X_LANES), F32), pltpu.VMEM((heads, tq, HEAD_DIM), F32)]
    vmem = (4 * _nbytes((seq_len, d), BF16) + 4 * _nbytes((tq, d), BF16)
            + 2 * _nbytes((tq, V7X_LANES), F32) + 2 * _nbytes((heads * nq, tq), F32)
            + sum(_nbytes(s.shape, s.dtype) for s in scratch)
            + 8 * _nbytes((tq, tq), F32))
    return pl.pallas_call(
        functools.partial(_attn_kernel, blocks_per_seq=nq),
        grid=(batch, nq),
        in_specs=[q_tile, kv_seq, kv_seq,
                  pl.BlockSpec((tq, V7X_LANES), lambda b, i: (b * nq + i, 0)),
                  pl.BlockSpec((None, heads * nq, tq), lambda b, i: (b, 0, 0))],
        out_specs=q_tile,
        out_shape=jax.ShapeDtypeStruct((t, d), BF16),
        scratch_shapes=scratch,
        compiler_params=_params(("arbitrary", "arbitrary"), vmem),
        name="attention",
    )(q, k, v, c_cols, c_rows)


def _mixer_out_kernel(x_ref, a_ref, part_ref, g2_ref, woa_ref, wo_ref, g_ref, wg_ref, wu_ref, wd_ref,
                      o_ref, act_ref):
    yc = jnp.dot(a_ref[...], woa_ref[...], preferred_element_type=F32)
    merged = part_ref[...] + g2_ref[...] * yc
    x = x_ref[...] + jnp.dot(merged.astype(BF16), wo_ref[...], preferred_element_type=F32)
    o_ref[...] = _swiglu_half_step(x, g_ref, wg_ref, wu_ref, wd_ref, act_ref)


def _mixer_out(x, attn, part, g2, w_out_attn, w_o, norm_g, wg, wu, wd):
    t, d = x.shape
    d_ff = wg.shape[1]
    tm = TOKEN_TILE_FFN
    assert t % tm == 0
    tile = pl.BlockSpec((tm, d), lambda i: (i, 0))
    vmem = (_nbytes((d, 2 * d_ff), BF16) + _nbytes((d_ff, d), BF16) + 2 * _nbytes((d, d), BF16)
            + 2 * (4 * _nbytes((tm, d), F32) + _nbytes((tm, d), BF16))
            + _nbytes((tm, d_ff), BF16)
            + 6 * _nbytes((tm, FF_CHUNK), F32) + 4 * _nbytes((tm, d), F32))
    return pl.pallas_call(
        _mixer_out_kernel,
        grid=(t // tm,),
        in_specs=[tile, tile, tile, tile, _resident(w_out_attn.shape), _resident(w_o.shape),
                  _resident((1, d)), _resident(wg.shape), _resident(wu.shape), _resident(wd.shape)],
        out_specs=tile,
        out_shape=jax.ShapeDtypeStruct((t, d), F32),
        scratch_shapes=[pltpu.VMEM((tm, d_ff), BF16)],
        compiler_params=_params(("arbitrary",), vmem),
        name="mixer_out",
    )(x, attn, part, g2, w_out_attn, w_o, norm_g, wg, wu, wd)


def _row(v):
    return v.reshape(1, -1)


def kernel(x, ffn1_norm, ffn1_w_gu, ffn1_w_down, mix_norm, w_in, b_forget, b_gate, conv_w, sgu_ln_g, sgu_ln_b,
           sgu_w, sgu_b, q_norm_g, k_norm_g, w_out_conv, w_out_sgu, w_out_attn, w_o, ffn2_norm, ffn2_w_gu,
           ffn2_w_down):
    batch, seq_len, d = x.shape
    depth = w_in.shape[0]
    d_ff = ffn1_w_down.shape[1]
    heads = d // HEAD_DIM
    n_main = 8 * d
    bf = lambda a: a.astype(BF16)

    xt = x.reshape(batch * seq_len, d)
    for l in range(depth):
        xt = _ffn(xt, _row(ffn1_norm[l]), bf(ffn1_w_gu[l][:, :d_ff]), bf(ffn1_w_gu[l][:, d_ff:]),
                  bf(ffn1_w_down[l]))
        p = {
            "mix_norm": _row(mix_norm[l]),
            "w_main": bf(w_in[l][:, :n_main]),
            "w_forget": bf(jnp.pad(w_in[l][:, n_main:n_main + heads], ((0, 0), (0, V7X_LANES - heads)))),
            "w_gate": bf(w_in[l][:, n_main + heads:]),
            "b_forget": jnp.pad(_row(b_forget[l]), ((0, 0), (0, V7X_LANES - heads))),
            "b_gate": _row(b_gate[l]),
            "conv_w": conv_w[l],
            "sgu_ln_g": _row(sgu_ln_g[l]),
            "sgu_ln_b": _row(sgu_ln_b[l]),
            "sgu_w": sgu_w[l],
            "sgu_b_t": sgu_b[l].T,
            "q_norm_g": _row(q_norm_g[l]),
            "k_norm_g": _row(k_norm_g[l]),
            "w_out_conv": bf(w_out_conv[l]),
            "w_out_sgu": bf(w_out_sgu[l]),
        }
        part, g2, q, k, v, c = _mixer_in(xt, p, seq_len=seq_len)
        c_rows = c[:, :heads].reshape(batch, seq_len, heads).transpose(0, 2, 1)
        c_rows = c_rows.reshape(batch, heads * (seq_len // ATTN_BLOCK), ATTN_BLOCK)
        attn = _attention(q, k, v, c, c_rows, batch=batch, seq_len=seq_len)
        xt = _mixer_out(xt, attn, part, g2, bf(w_out_attn[l]), bf(w_o[l]), _row(ffn2_norm[l]),
                        bf(ffn2_w_gu[l][:, :d_ff]), bf(ffn2_w_gu[l][:, d_ff:]), bf(ffn2_w_down[l]))
    return xt.reshape(batch, seq_len, d)
```

```python
import functools
import math

import jax
import jax.numpy as jnp
from jax import lax
from jax.experimental import pallas as pl
from jax.experimental.pallas import tpu as pltpu

F32 = jnp.float32
BF16 = jnp.bfloat16

RMS_EPS = 1e-6
LN_EPS = 1e-5
SGU_CHUNK = 128
HEAD_DIM = 128
N_BRANCH = 3
MASKED = -1e30
FORGET_SPLIT = 3
ONES_ROWS = 16

V7X_LANES = 128
V7X_SUBLANES = 8
V7X_VMEM_BYTES = 64 * 1024 * 1024
FF_CHUNK = 1024

TOKEN_TILE_FFN = 512
SEQ_BLOCK = 256


def _layer(arr, layer, cols=None, col_block=0):
    block = (None,) + tuple(arr.shape[1:-1]) + (arr.shape[-1] if cols is None else cols,)
    index = (layer,) + (0,) * (arr.ndim - 2) + (col_block,)
    return pl.BlockSpec(block, lambda *_: index, pipeline_mode=pl.Buffered(1))


def _nbytes(shape, dtype):
    return math.prod(shape) * jnp.dtype(dtype).itemsize


def _params(semantics, vmem_bytes):
    assert vmem_bytes <= V7X_VMEM_BYTES, vmem_bytes
    return pltpu.CompilerParams(dimension_semantics=semantics, vmem_limit_bytes=int(vmem_bytes))


def _rms_norm(x, g):
    return x * lax.rsqrt(jnp.mean(x * x, axis=-1, keepdims=True) + RMS_EPS) * g


def _gelu(x):
    return 0.5 * x * (1.0 + lax.erf(x * (2.0 ** -0.5)))


def _ff_chunks(d_ff):
    return [(c, min(c + FF_CHUNK, d_ff)) for c in range(0, d_ff, FF_CHUNK)]


def _row_groups(x):
    return [x[r:r + V7X_SUBLANES] for r in range(0, x.shape[0], V7X_SUBLANES)]


def _swiglu_half_step(x, g_ref, wg_ref, wu_ref, wd_ref, act_ref):
    h = _rms_norm(x, g_ref[...]).astype(BF16)
    for c0, c1 in _ff_chunks(wg_ref.shape[1]):
        g = jnp.dot(h, wg_ref[:, c0:c1], preferred_element_type=F32)
        u = jnp.dot(h, wu_ref[:, c0:c1], preferred_element_type=F32)
        act_ref[:, c0:c1] = (g * jax.nn.sigmoid(g) * u).astype(BF16)
    return x + 0.5 * jnp.dot(act_ref[...], wd_ref[...], preferred_element_type=F32)


def _ffn_weight_specs(w, layer):
    d_ff = w["w_down"].shape[1]
    return [_layer(w["norm"], layer), _layer(w["w_gu"], layer, d_ff, 0), _layer(w["w_gu"], layer, d_ff, 1),
            _layer(w["w_down"], layer)]


def _ffn_weight_args(w):
    return [w["norm"], w["w_gu"], w["w_gu"], w["w_down"]]


def _ffn_vmem(tm, d, d_ff):
    return (_nbytes((d, 2 * d_ff), BF16) + _nbytes((d_ff, d), BF16)
            + _nbytes((tm, d_ff), BF16)
            + 6 * _nbytes((tm, FF_CHUNK), F32))


def _ffn_kernel(x_ref, g_ref, wg_ref, wu_ref, wd_ref, o_ref, act_ref):
    o_ref[...] = _swiglu_half_step(x_ref[...], g_ref, wg_ref, wu_ref, wd_ref, act_ref)


def _ffn(x, w, layer):
    t, d = x.shape
    d_ff = w["w_down"].shape[1]
    tm = TOKEN_TILE_FFN
    assert t % tm == 0
    tile = pl.BlockSpec((tm, d), lambda i: (i, 0))
    vmem = _ffn_vmem(tm, d, d_ff) + 4 * _nbytes((tm, d), F32)
    return pl.pallas_call(
        _ffn_kernel,
        grid=(t // tm,),
        in_specs=[tile] + _ffn_weight_specs(w, layer),
        out_specs=tile,
        out_shape=jax.ShapeDtypeStruct((t, d), F32),
        scratch_shapes=[pltpu.VMEM((tm, d_ff), BF16)],
        compiler_params=_params(("arbitrary",), vmem),
        name="ffn",
    )(x, *_ffn_weight_args(w))


def _mixer_in_kernel(x_ref, ng_ref, wm_ref, wf_ref, wgt_ref, bf_ref, bg_ref, cw_ref, lng_ref, lnb_ref,
                     sw_ref, sb_ref, qg_ref, kg_ref, woc_ref, wos_ref,
                     part_ref, g2_ref, qt_ref, k_ref, vt_ref, ccol_ref, crow_ref,
                     zs_ref, ccarry_ref, vn_ref, yb_ref, *, tiles_per_seq, q_scale, c_scale):
    tm, d = x_ref.shape
    heads = d // HEAD_DIM
    pad = V7X_SUBLANES

    @pl.when(pl.program_id(0) % tiles_per_seq == 0)
    def _():
        zs_ref[0:pad, :] = jnp.zeros((pad, d), F32)
        ccarry_ref[...] = jnp.zeros_like(ccarry_ref)

    h = _rms_norm(x_ref[...], ng_ref[...]).astype(BF16)

    def proj(w_ref, c0, c1):
        return jnp.dot(h, w_ref[:, c0:c1], preferred_element_type=F32)

    def gate(b):
        return jax.nn.sigmoid(proj(wgt_ref, b * d, (b + 1) * d) + bg_ref[:, b * d:(b + 1) * d])

    pa = proj(wm_ref, 0, 3 * d)
    zs_ref[pad:pad + tm, :] = pa[:, d:2 * d] * pa[:, 2 * d:3 * d]
    conv = (cw_ref[0:1, :] * zs_ref[pad - 2:pad - 2 + tm, :]
            + cw_ref[1:2, :] * zs_ref[pad - 1:pad - 1 + tm, :]
            + cw_ref[2:3, :] * zs_ref[pad:pad + tm, :])
    ya = jnp.dot((pa[:, 0:d] * conv).astype(BF16), woc_ref[...], preferred_element_type=F32)
    zs_ref[0:pad, :] = zs_ref[tm:tm + pad, :]
    part = gate(0) * ya

    ps = proj(wm_ref, 3 * d, 5 * d)
    u = _gelu(ps[:, 0:d])
    vv = _gelu(ps[:, d:2 * d])
    mu = jnp.mean(vv, axis=-1, keepdims=True)
    vc = vv - mu
    var = jnp.mean(vc * vc, axis=-1, keepdims=True)
    vn_ref[...] = (vc * lax.rsqrt(var + LN_EPS) * lng_ref[...] + lnb_ref[...]).astype(BF16)
    n_chunks = tm // SGU_CHUNK
    pos_t = lax.broadcasted_iota(jnp.int32, (SGU_CHUNK, SGU_CHUNK), 0)
    pos_s = lax.broadcasted_iota(jnp.int32, (SGU_CHUNK, SGU_CHUNK), 1)
    for g in range(heads):
        hs = slice(g * HEAD_DIM, (g + 1) * HEAD_DIM)
        w = jnp.where(pos_s <= pos_t, sw_ref[g], 0.0).astype(BF16)
        rhs = jnp.concatenate([vn_ref[c * SGU_CHUNK:(c + 1) * SGU_CHUNK, hs] for c in range(n_chunks)], axis=1)
        s = jnp.dot(w, rhs, preferred_element_type=F32) + sb_ref[:, g:g + 1]
        for c in range(n_chunks):
            rows = slice(c * SGU_CHUNK, (c + 1) * SGU_CHUNK)
            yb_ref[rows, hs] = (u[rows, hs] * s[:, c * SGU_CHUNK:(c + 1) * SGU_CHUNK]).astype(BF16)
    yb = jnp.dot(yb_ref[...], wos_ref[...], preferred_element_type=F32)
    part_ref[...] = part + gate(1) * yb
    g2_ref[...] = gate(2)

    pq = proj(wm_ref, 5 * d, 8 * d)
    qn = []
    for g in range(heads):
        hs = slice(g * HEAD_DIM, (g + 1) * HEAD_DIM)
        qn.append(_rms_norm(pq[:, hs], qg_ref[:, hs]) * q_scale)
        ks = slice(d + g * HEAD_DIM, d + (g + 1) * HEAD_DIM)
        k_ref[:, hs] = _rms_norm(pq[:, ks], kg_ref[:, hs]).astype(BF16)
    qt_ref[...] = jnp.concatenate(qn, axis=1).T.astype(BF16)
    vt_ref[...] = pq[:, 2 * d:3 * d].T.astype(BF16)

    f = jnp.dot(h, wf_ref[...], preferred_element_type=F32) + bf_ref[...]
    c = jnp.minimum(f, 0.0) - jnp.log1p(jnp.exp(-jnp.abs(f)))
    t_idx = lax.broadcasted_iota(jnp.int32, c.shape, 0)
    shift = 1
    while shift < tm:
        c = c + jnp.where(t_idx >= shift, pltpu.roll(c, shift, axis=0), 0.0)
        shift *= 2
    c = c + ccarry_ref[0:1, :]
    ccarry_ref[0:1, :] = c[tm - 1:tm, :]
    c = c * c_scale
    ccol_ref[...] = c
    crow_ref[...] = c.T[0:heads, :]


def _mixer_in(x, w, layer, *, seq_len):
    t, d = x.shape
    tm = SEQ_BLOCK
    heads = d // HEAD_DIM
    assert seq_len % tm == 0 and tm % SGU_CHUNK == 0 and d % HEAD_DIM == 0 and heads <= V7X_SUBLANES
    n_main = 8 * d
    kernel = functools.partial(
        _mixer_in_kernel, tiles_per_seq=seq_len // tm,
        q_scale=HEAD_DIM ** -0.5 * math.log2(math.e), c_scale=math.log2(math.e))
    weight_specs = [
        _layer(w["mix_norm"], layer), _layer(w["w_in"], layer, n_main, 0),
        _layer(w["w_in"], layer, V7X_LANES, n_main // V7X_LANES), _layer(w["w_gate"], layer),
        _layer(w["b_forget"], layer), _layer(w["b_gate"], layer), _layer(w["conv_w"], layer),
        _layer(w["sgu_ln_g"], layer), _layer(w["sgu_ln_b"], layer), _layer(w["sgu_w"], layer),
        _layer(w["sgu_b_t"], layer), _layer(w["q_norm_g"], layer), _layer(w["k_norm_g"], layer),
        _layer(w["w_out_conv"], layer), _layer(w["w_out_sgu"], layer)]
    weight_args = [w["mix_norm"], w["w_in"], w["w_in"], w["w_gate"], w["b_forget"], w["b_gate"], w["conv_w"],
                   w["sgu_ln_g"], w["sgu_ln_b"], w["sgu_w"], w["sgu_b_t"], w["q_norm_g"], w["k_norm_g"],
                   w["w_out_conv"], w["w_out_sgu"]]
    tile = pl.BlockSpec((tm, d), lambda i: (i, 0))
    tile_t = pl.BlockSpec((None, d, tm), lambda i: (i, 0, 0))
    vmem = (_nbytes((d, n_main + V7X_LANES + N_BRANCH * d + 2 * d), BF16)
            + 2 * (3 * _nbytes((tm, d), F32) + 3 * _nbytes((tm, d), BF16) + 2 * _nbytes((tm, V7X_LANES), F32))
            + _nbytes((tm + V7X_SUBLANES, d), F32) + 2 * _nbytes((tm, d), BF16)
            + 12 * _nbytes((tm, d), F32))
    return pl.pallas_call(
        kernel,
        grid=(t // tm,),
        in_specs=[tile] + weight_specs,
        out_specs=[tile, tile, tile_t, tile, tile_t,
                   pl.BlockSpec((tm, V7X_LANES), lambda i: (i, 0)),
                   pl.BlockSpec((None, heads, tm), lambda i: (i, 0, 0))],
        out_shape=[jax.ShapeDtypeStruct((t, d), F32), jax.ShapeDtypeStruct((t, d), F32),
                   jax.ShapeDtypeStruct((t // tm, d, tm), BF16), jax.ShapeDtypeStruct((t, d), BF16),
                   jax.ShapeDtypeStruct((t // tm, d, tm), BF16), jax.ShapeDtypeStruct((t, V7X_LANES), F32),
                   jax.ShapeDtypeStruct((t // tm, heads, tm), F32)],
        scratch_shapes=[pltpu.VMEM((tm + V7X_SUBLANES, d), F32), pltpu.VMEM((V7X_SUBLANES, V7X_LANES), F32),
                        pltpu.VMEM((tm, d), BF16), pltpu.VMEM((tm, d), BF16)],
        compiler_params=_params(("arbitrary",), vmem),
        name="mixer_in",
    )(x, *weight_args)


def _attn_kernel(qt_ref, k_ref, vt_ref, ccol_ref, crow_ref, o_ref,
                 kaug_ref, vaug_ref, qaug_ref, s0_ref, s1_ref, m_ref, acc_ref):
    d, tq = qt_ref.shape
    heads = d // HEAD_DIM
    n_blocks = vt_ref.shape[0]
    i = pl.program_id(1)
    head_cols = [slice(h * HEAD_DIM, (h + 1) * HEAD_DIM) for h in range(heads)]
    key_pos = lax.broadcasted_iota(jnp.int32, (tq, tq), 0)
    query_pos = lax.broadcasted_iota(jnp.int32, (tq, tq), 1)
    causal = key_pos <= query_pos

    @pl.when(i == 0)
    def _():
        lane = lax.broadcasted_iota(jnp.int32, (tq, HEAD_DIM), 1)
        for h, hs in enumerate(head_cols):
            for jb in range(n_blocks):
                rows = slice(jb * tq, (jb + 1) * tq)
                kaug_ref[h, rows, 0:HEAD_DIM] = k_ref[rows, hs]
                rest = -jnp.broadcast_to(ccol_ref[rows, h:h + 1], (tq, HEAD_DIM))
                slab = jnp.zeros((tq, HEAD_DIM), F32)
                for term in range(FORGET_SPLIT):
                    piece = rest.astype(BF16).astype(F32)
                    slab = jnp.where(lane == term, piece, slab)
                    rest = rest - piece
                kaug_ref[h, rows, HEAD_DIM:2 * HEAD_DIM] = slab.astype(BF16)
                vaug_ref[jb, h, 0:HEAD_DIM, :] = vt_ref[jb, hs, :]
                vaug_ref[jb, h, HEAD_DIM:HEAD_DIM + ONES_ROWS, :] = jnp.ones((ONES_ROWS, tq), BF16)

    feature = lax.broadcasted_iota(jnp.int32, (HEAD_DIM, tq), 0)
    ones_rows = jnp.where(feature < FORGET_SPLIT, 1.0, 0.0).astype(BF16)
    for h, hs in enumerate(head_cols):
        qaug_ref[h, 0:HEAD_DIM, :] = qt_ref[hs, :]
        qaug_ref[h, HEAD_DIM:2 * HEAD_DIM, :] = ones_rows
    m_ref[...] = jnp.full(m_ref.shape, MASKED, F32)
    acc_ref[...] = jnp.zeros(acc_ref.shape, F32)

    s_refs = (s0_ref, s1_ref)

    def logits(j, slot, diagonal):
        rows = pl.ds(pl.multiple_of(j * tq, tq), tq)
        for h in range(heads):
            s = jnp.dot(kaug_ref[h, rows, :], qaug_ref[h], preferred_element_type=F32)
            s_refs[slot][h] = jnp.where(causal, s, MASKED) if diagonal else s

    def softmax_pv(j, slot):
        probs, rescale = [], []
        for h in range(heads):
            s = s_refs[slot][h]
            cq = crow_ref[i, h:h + 1, :]
            m_old = m_ref[h]
            block_max = jnp.max(functools.reduce(jnp.maximum, _row_groups(s)), axis=0, keepdims=True)
            m_new = jnp.maximum(m_old, block_max + cq)
            m_ref[h] = m_new
            probs.append(jnp.exp2(s + (cq - m_new)).astype(BF16))
            rescale.append(jnp.exp2(m_old - m_new))
        for h in range(heads):
            acc_ref[h] = rescale[h] * acc_ref[h] + jnp.dot(vaug_ref[j, h], probs[h], preferred_element_type=F32)

    @pl.when(i == 0)
    def _():
        logits(0, 0, True)
        softmax_pv(0, 0)

    def stage(j_next, slot_next, diagonal, j, slot):
        logits(j_next, slot_next, diagonal)
        softmax_pv(j, slot)

    n_pairs = lax.shift_right_logical(i - 1, 1)
    j0 = 2 * n_pairs

    @pl.when(i > 0)
    def _():
        logits(0, 0, False)

        def pair(t, carry):
            j = 2 * t
            stage(j + 1, 1, False, j, 0)
            stage(j + 2, 0, False, j + 1, 1)
            return carry

        lax.fori_loop(0, n_pairs, pair, 0)

    @pl.when(jnp.logical_and(i > 0, i - j0 == 1))
    def _():
        stage(i, 1, True, j0, 0)
        softmax_pv(i, 1)

    @pl.when(jnp.logical_and(i > 0, i - j0 == 2))
    def _():
        stage(j0 + 1, 1, False, j0, 0)
        stage(i, 0, True, j0 + 1, 1)
        softmax_pv(i, 0)

    for h, hs in enumerate(head_cols):
        row_sum = acc_ref[h, HEAD_DIM:HEAD_DIM + 1, :]
        o_ref[:, hs] = (acc_ref[h, 0:HEAD_DIM, :] * (1.0 / row_sum)).T.astype(BF16)


def _attention(qt, k, vt, c_cols, c_rows, *, batch, seq_len):
    nt, d, tq = qt.shape
    nq = seq_len // tq
    heads = d // HEAD_DIM
    assert nt == batch * nq and tq % V7X_LANES == 0
    scratch = [pltpu.VMEM((heads, seq_len, 2 * HEAD_DIM), BF16),
               pltpu.VMEM((nq, heads, HEAD_DIM + ONES_ROWS, tq), BF16),
               pltpu.VMEM((heads, 2 * HEAD_DIM, tq), BF16),
               pltpu.VMEM((heads, tq, tq), F32), pltpu.VMEM((heads, tq, tq), F32),
               pltpu.VMEM((heads, 1, tq), F32),
               pltpu.VMEM((heads, HEAD_DIM + ONES_ROWS, tq), F32)]
    vmem = (6 * _nbytes((seq_len, d), BF16)
            + 2 * _nbytes((seq_len, V7X_LANES), F32) + 2 * _nbytes((nq, V7X_SUBLANES, tq), F32)
            + sum(_nbytes(s.shape, s.dtype) for s in scratch)
            + _nbytes((V7X_SUBLANES * heads, tq), F32)
            + 16 * _nbytes((tq, tq), F32))
    return pl.pallas_call(
        _attn_kernel,
        grid=(batch, nq),
        in_specs=[pl.BlockSpec((None, d, tq), lambda b, i: (b * nq + i, 0, 0)),
                  pl.BlockSpec((seq_len, d), lambda b, i: (b, 0)),
                  pl.BlockSpec((nq, d, tq), lambda b, i: (b, 0, 0)),
                  pl.BlockSpec((seq_len, V7X_LANES), lambda b, i: (b, 0)),
                  pl.BlockSpec((nq, heads, tq), lambda b, i: (b, 0, 0))],
        out_specs=pl.BlockSpec((tq, d), lambda b, i: (b * nq + i, 0)),
        out_shape=jax.ShapeDtypeStruct((nt * tq, d), BF16),
        scratch_shapes=scratch,
        compiler_params=_params(("arbitrary", "arbitrary"), vmem),
        name="attention",
    )(qt, k, vt, c_cols, c_rows)


def _mixer_out_kernel(x_ref, a_ref, part_ref, g2_ref, woa_ref, wo_ref, g_ref, wg_ref, wu_ref, wd_ref,
                      o_ref, act_ref):
    yc = jnp.dot(a_ref[...], woa_ref[...], preferred_element_type=F32)
    merged = part_ref[...] + g2_ref[...] * yc
    x = x_ref[...] + jnp.dot(merged.astype(BF16), wo_ref[...], preferred_element_type=F32)
    o_ref[...] = _swiglu_half_step(x, g_ref, wg_ref, wu_ref, wd_ref, act_ref)


def _mixer_out(x, attn, part, g2, w_mix, w_ffn, layer):
    t, d = x.shape
    d_ff = w_ffn["w_down"].shape[1]
    tm = TOKEN_TILE_FFN
    assert t % tm == 0
    tile = pl.BlockSpec((tm, d), lambda i: (i, 0))
    vmem = (_ffn_vmem(tm, d, d_ff) + 2 * _nbytes((d, d), BF16)
            + 2 * (4 * _nbytes((tm, d), F32) + _nbytes((tm, d), BF16)) + 4 * _nbytes((tm, d), F32))
    return pl.pallas_call(
        _mixer_out_kernel,
        grid=(t // tm,),
        in_specs=[tile, tile, tile, tile, _layer(w_mix["w_out_attn"], layer), _layer(w_mix["w_o"], layer)]
        + _ffn_weight_specs(w_ffn, layer),
        out_specs=tile,
        out_shape=jax.ShapeDtypeStruct((t, d), F32),
        scratch_shapes=[pltpu.VMEM((tm, d_ff), BF16)],
        compiler_params=_params(("arbitrary",), vmem),
        name="mixer_out",
    )(x, attn, part, g2, w_mix["w_out_attn"], w_mix["w_o"], *_ffn_weight_args(w_ffn))


def kernel(x, ffn1_norm, ffn1_w_gu, ffn1_w_down, mix_norm, w_in, b_forget, b_gate, conv_w, sgu_ln_g, sgu_ln_b,
           sgu_w, sgu_b, q_norm_g, k_norm_g, w_out_conv, w_out_sgu, w_out_attn, w_o, ffn2_norm, ffn2_w_gu,
           ffn2_w_down):
    batch, seq_len, d = x.shape
    depth = w_in.shape[0]
    heads = d // HEAD_DIM
    bf = lambda a: a.astype(BF16)
    rows = lambda a: a.reshape(depth, 1, -1)

    ffn1 = {"norm": rows(ffn1_norm), "w_gu": bf(ffn1_w_gu), "w_down": bf(ffn1_w_down)}
    ffn2 = {"norm": rows(ffn2_norm), "w_gu": bf(ffn2_w_gu), "w_down": bf(ffn2_w_down)}
    mix = {
        "mix_norm": rows(mix_norm),
        "w_in": bf(w_in),
        "w_gate": bf(w_in[:, :, 8 * d + heads:]),
        "b_forget": jnp.pad(rows(b_forget), ((0, 0), (0, 0), (0, V7X_LANES - heads))),
        "b_gate": rows(b_gate),
        "conv_w": conv_w,
        "sgu_ln_g": rows(sgu_ln_g),
        "sgu_ln_b": rows(sgu_ln_b),
        "sgu_w": sgu_w,
        "sgu_b_t": jnp.swapaxes(sgu_b, 1, 2),
        "q_norm_g": rows(q_norm_g),
        "k_norm_g": rows(k_norm_g),
        "w_out_conv": bf(w_out_conv),
        "w_out_sgu": bf(w_out_sgu),
        "w_out_attn": bf(w_out_attn),
        "w_o": bf(w_o),
    }

    xt = x.reshape(batch * seq_len, d)
    for layer in range(depth):
        xt = _ffn(xt, ffn1, layer)
        part, g2, qt, k, vt, c_cols, c_rows = _mixer_in(xt, mix, layer, seq_len=seq_len)
        attn = _attention(qt, k, vt, c_cols, c_rows, batch=batch, seq_len=seq_len)
        xt = _mixer_out(xt, attn, part, g2, mix, ffn2, layer)
    return xt.reshape(batch, seq_len, d)
```

```python
import functools
import math

import jax
import jax.numpy as jnp
from jax import lax
from jax.experimental import pallas as pl
from jax.experimental.pallas import tpu as pltpu

F32 = jnp.float32
BF16 = jnp.bfloat16

RMS_EPS = 1e-6
LN_EPS = 1e-5
SGU_CHUNK = 128
HEAD_DIM = 128
N_BRANCH = 3
MASKED = -1e30
FORGET_SPLIT = 3
ONES_ROWS = 16

V7X_LANES = 128
V7X_SUBLANES = 8
V7X_VMEM_BYTES = 64 * 1024 * 1024
FF_CHUNK = 1024

TOKEN_TILE_FFN = 512
SEQ_BLOCK = 256


def _layer(arr, layer, cols=None, col_block=0):
    block = (None,) + tuple(arr.shape[1:-1]) + (arr.shape[-1] if cols is None else cols,)
    index = (layer,) + (0,) * (arr.ndim - 2) + (col_block,)
    return pl.BlockSpec(block, lambda *_: index, pipeline_mode=pl.Buffered(1))


def _nbytes(shape, dtype):
    return math.prod(shape) * jnp.dtype(dtype).itemsize


def _params(semantics, vmem_bytes):
    assert vmem_bytes <= V7X_VMEM_BYTES, vmem_bytes
    return pltpu.CompilerParams(dimension_semantics=semantics, vmem_limit_bytes=int(vmem_bytes))


def _rms_norm(x, g):
    return x * lax.rsqrt(jnp.mean(x * x, axis=-1, keepdims=True) + RMS_EPS) * g


def _gelu(x):
    return 0.5 * x * (1.0 + lax.erf(x * (2.0 ** -0.5)))


def _ff_chunks(d_ff):
    return [(c, min(c + FF_CHUNK, d_ff)) for c in range(0, d_ff, FF_CHUNK)]


def _row_groups(x):
    return [x[r:r + V7X_SUBLANES] for r in range(0, x.shape[0], V7X_SUBLANES)]


def _swiglu_half_step(x, g_ref, wg_ref, wu_ref, wd_ref, act_ref):
    h = _rms_norm(x, g_ref[...]).astype(BF16)
    for c0, c1 in _ff_chunks(wg_ref.shape[1]):
        g = jnp.dot(h, wg_ref[:, c0:c1], preferred_element_type=F32)
        u = jnp.dot(h, wu_ref[:, c0:c1], preferred_element_type=F32)
        act_ref[:, c0:c1] = (g * jax.nn.sigmoid(g) * u).astype(BF16)
    return x + 0.5 * jnp.dot(act_ref[...], wd_ref[...], preferred_element_type=F32)


def _ffn_weight_specs(w, layer):
    d_ff = w["w_down"].shape[1]
    return [_layer(w["norm"], layer), _layer(w["w_gu"], layer, d_ff, 0), _layer(w["w_gu"], layer, d_ff, 1),
            _layer(w["w_down"], layer)]


def _ffn_weight_args(w):
    return [w["norm"], w["w_gu"], w["w_gu"], w["w_down"]]


def _ffn_vmem(tm, d, d_ff):
    return (_nbytes((d, 2 * d_ff), BF16) + _nbytes((d_ff, d), BF16)
            + _nbytes((tm, d_ff), BF16)
            + 6 * _nbytes((tm, FF_CHUNK), F32))


def _ffn_kernel(x_ref, g_ref, wg_ref, wu_ref, wd_ref, o_ref, act_ref):
    o_ref[...] = _swiglu_half_step(x_ref[...], g_ref, wg_ref, wu_ref, wd_ref, act_ref)


def _ffn(x, w, layer):
    t, d = x.shape
    d_ff = w["w_down"].shape[1]
    tm = TOKEN_TILE_FFN
    assert t % tm == 0
    tile = pl.BlockSpec((tm, d), lambda i: (i, 0))
    vmem = _ffn_vmem(tm, d, d_ff) + 4 * _nbytes((tm, d), F32)
    return pl.pallas_call(
        _ffn_kernel,
        grid=(t // tm,),
        in_specs=[tile] + _ffn_weight_specs(w, layer),
        out_specs=tile,
        out_shape=jax.ShapeDtypeStruct((t, d), F32),
        scratch_shapes=[pltpu.VMEM((tm, d_ff), BF16)],
        compiler_params=_params(("arbitrary",), vmem),
        name="ffn",
    )(x, *_ffn_weight_args(w))


def _mixer_in_kernel(x_ref, ng_ref, wm_ref, wf_ref, wgt_ref, bf_ref, bg_ref, cw_ref, lng_ref, lnb_ref,
                     sw_ref, sb_ref, qg_ref, kg_ref, woc_ref, wos_ref,
                     part_ref, g2_ref, qaug_ref, kaug_ref, vaug_ref, crow_ref,
                     zs_ref, ccarry_ref, vn_ref, yb_ref, *, tiles_per_seq, q_scale, c_scale):
    tm, d = x_ref.shape
    heads = d // HEAD_DIM
    pad = V7X_SUBLANES

    @pl.when(pl.program_id(0) % tiles_per_seq == 0)
    def _():
        zs_ref[0:pad, :] = jnp.zeros((pad, d), F32)
        ccarry_ref[...] = jnp.zeros_like(ccarry_ref)

    h = _rms_norm(x_ref[...], ng_ref[...]).astype(BF16)

    def proj(w_ref, c0, c1):
        return jnp.dot(h, w_ref[:, c0:c1], preferred_element_type=F32)

    pa = proj(wm_ref, 0, 3 * d)
    f = proj(wf_ref, 0, V7X_LANES) + bf_ref[...]
    ps = proj(wm_ref, 3 * d, 5 * d)
    pq = proj(wm_ref, 5 * d, 8 * d)
    gates = [jax.nn.sigmoid(proj(wgt_ref, b * d, (b + 1) * d) + bg_ref[:, b * d:(b + 1) * d])
             for b in range(N_BRANCH)]

    zs_ref[pad:pad + tm, :] = pa[:, d:2 * d] * pa[:, 2 * d:3 * d]
    conv = (cw_ref[0:1, :] * zs_ref[pad - 2:pad - 2 + tm, :]
            + cw_ref[1:2, :] * zs_ref[pad - 1:pad - 1 + tm, :]
            + cw_ref[2:3, :] * zs_ref[pad:pad + tm, :])
    ya_in = (pa[:, 0:d] * conv).astype(BF16)
    zs_ref[0:pad, :] = zs_ref[tm:tm + pad, :]

    c = jnp.minimum(f, 0.0) - jnp.log1p(jnp.exp(-jnp.abs(f)))
    t_idx = lax.broadcasted_iota(jnp.int32, c.shape, 0)
    shift = 1
    while shift < tm:
        c = c + jnp.where(t_idx >= shift, pltpu.roll(c, shift, axis=0), 0.0)
        shift *= 2
    c = c + ccarry_ref[0:1, :]
    ccarry_ref[0:1, :] = c[tm - 1:tm, :]
    c = c * c_scale
    crow_ref[...] = c.T[0:heads, :]

    u = _gelu(ps[:, 0:d])
    vv = _gelu(ps[:, d:2 * d])
    mu = jnp.mean(vv, axis=-1, keepdims=True)
    vc = vv - mu
    var = jnp.mean(vc * vc, axis=-1, keepdims=True)
    vn_ref[...] = (vc * lax.rsqrt(var + LN_EPS) * lng_ref[...] + lnb_ref[...]).astype(BF16)

    lane = lax.broadcasted_iota(jnp.int32, (tm, HEAD_DIM), 1)
    feature = lax.broadcasted_iota(jnp.int32, (HEAD_DIM, tm), 0)
    ones_rows = jnp.where(feature < FORGET_SPLIT, 1.0, 0.0).astype(BF16)
    qn = []
    for g in range(heads):
        hs = slice(g * HEAD_DIM, (g + 1) * HEAD_DIM)
        qn.append(_rms_norm(pq[:, hs], qg_ref[:, hs]) * q_scale)
        ks = slice(d + g * HEAD_DIM, d + (g + 1) * HEAD_DIM)
        kaug_ref[:, 2 * g * HEAD_DIM:(2 * g + 1) * HEAD_DIM] = _rms_norm(pq[:, ks], kg_ref[:, hs]).astype(BF16)
        rest = -jnp.broadcast_to(c[:, g:g + 1], (tm, HEAD_DIM))
        slab = jnp.zeros((tm, HEAD_DIM), F32)
        for term in range(FORGET_SPLIT):
            piece = rest.astype(BF16).astype(F32)
            slab = jnp.where(lane == term, piece, slab)
            rest = rest - piece
        kaug_ref[:, (2 * g + 1) * HEAD_DIM:(2 * g + 2) * HEAD_DIM] = slab.astype(BF16)
    qt = jnp.concatenate(qn, axis=1).T.astype(BF16)
    vt = pq[:, 2 * d:3 * d].T.astype(BF16)
    v_rows = HEAD_DIM + ONES_ROWS
    for g in range(heads):
        hs = slice(g * HEAD_DIM, (g + 1) * HEAD_DIM)
        qaug_ref[2 * g * HEAD_DIM:(2 * g + 1) * HEAD_DIM, :] = qt[hs, :]
        qaug_ref[(2 * g + 1) * HEAD_DIM:(2 * g + 2) * HEAD_DIM, :] = ones_rows
        vaug_ref[g * v_rows:g * v_rows + HEAD_DIM, :] = vt[hs, :]
        vaug_ref[g * v_rows + HEAD_DIM:(g + 1) * v_rows, :] = jnp.ones((ONES_ROWS, tm), BF16)

    ya = jnp.dot(ya_in, woc_ref[...], preferred_element_type=F32)
    n_chunks = tm // SGU_CHUNK
    pos_t = lax.broadcasted_iota(jnp.int32, (SGU_CHUNK, SGU_CHUNK), 0)
    pos_s = lax.broadcasted_iota(jnp.int32, (SGU_CHUNK, SGU_CHUNK), 1)
    for g in range(heads):
        hs = slice(g * HEAD_DIM, (g + 1) * HEAD_DIM)
        w = jnp.where(pos_s <= pos_t, sw_ref[g], 0.0).astype(BF16)
        rhs = jnp.concatenate([vn_ref[c0 * SGU_CHUNK:(c0 + 1) * SGU_CHUNK, hs] for c0 in range(n_chunks)], axis=1)
        s = jnp.dot(w, rhs, preferred_element_type=F32) + sb_ref[:, g:g + 1]
        for c0 in range(n_chunks):
            rows = slice(c0 * SGU_CHUNK, (c0 + 1) * SGU_CHUNK)
            yb_ref[rows, hs] = (u[rows, hs] * s[:, c0 * SGU_CHUNK:(c0 + 1) * SGU_CHUNK]).astype(BF16)
    yb = jnp.dot(yb_ref[...], wos_ref[...], preferred_element_type=F32)
    part_ref[...] = gates[0] * ya + gates[1] * yb
    g2_ref[...] = gates[2]


def _mixer_in(x, w, layer, *, seq_len):
    t, d = x.shape
    tm = SEQ_BLOCK
    heads = d // HEAD_DIM
    assert seq_len % tm == 0 and tm % SGU_CHUNK == 0 and d % HEAD_DIM == 0 and heads <= V7X_SUBLANES
    kernel = functools.partial(
        _mixer_in_kernel, tiles_per_seq=seq_len // tm,
        q_scale=HEAD_DIM ** -0.5 * math.log2(math.e), c_scale=math.log2(math.e))
    names = ["mix_norm", "w_main", "w_forget", "w_gate", "b_forget", "b_gate", "conv_w", "sgu_ln_g", "sgu_ln_b",
             "sgu_w", "sgu_b_t", "q_norm_g", "k_norm_g", "w_out_conv", "w_out_sgu"]
    weight_specs = [_layer(w[n], layer) for n in names]
    weight_args = [w[n] for n in names]
    v_rows = heads * (HEAD_DIM + ONES_ROWS)
    tile = pl.BlockSpec((tm, d), lambda i: (i, 0))
    vmem = (sum(_nbytes(w[n].shape[1:], w[n].dtype) for n in names)
            + 2 * (3 * _nbytes((tm, d), F32) + 4 * _nbytes((tm, d), BF16) + _nbytes((v_rows, tm), BF16)
                   + _nbytes((V7X_SUBLANES, tm), F32))
            + _nbytes((tm + V7X_SUBLANES, d), F32) + 2 * _nbytes((tm, d), BF16)
            + 20 * _nbytes((tm, d), F32))
    return pl.pallas_call(
        kernel,
        grid=(t // tm,),
        in_specs=[tile] + weight_specs,
        out_specs=[tile, tile,
                   pl.BlockSpec((None, 2 * d, tm), lambda i: (i, 0, 0)),
                   pl.BlockSpec((tm, 2 * d), lambda i: (i, 0)),
                   pl.BlockSpec((None, v_rows, tm), lambda i: (i, 0, 0)),
                   pl.BlockSpec((None, heads, tm), lambda i: (i, 0, 0))],
        out_shape=[jax.ShapeDtypeStruct((t, d), F32), jax.ShapeDtypeStruct((t, d), F32),
                   jax.ShapeDtypeStruct((t // tm, 2 * d, tm), BF16), jax.ShapeDtypeStruct((t, 2 * d), BF16),
                   jax.ShapeDtypeStruct((t // tm, v_rows, tm), BF16),
                   jax.ShapeDtypeStruct((t // tm, heads, tm), F32)],
        scratch_shapes=[pltpu.VMEM((tm + V7X_SUBLANES, d), F32), pltpu.VMEM((V7X_SUBLANES, V7X_LANES), F32),
                        pltpu.VMEM((tm, d), BF16), pltpu.VMEM((tm, d), BF16)],
        compiler_params=_params(("arbitrary",), vmem),
        name="mixer_in",
    )(x, *weight_args)


def _attn_kernel(qaug_ref, kaug_ref, vaug_ref, crow_ref, o_ref, s0_ref, s1_ref, m_ref, acc_ref):
    tq = qaug_ref.shape[1]
    heads = m_ref.shape[0]
    i = pl.program_id(1)
    head_cols = [slice(h * HEAD_DIM, (h + 1) * HEAD_DIM) for h in range(heads)]
    qk_cols = [slice(h * 2 * HEAD_DIM, (h + 1) * 2 * HEAD_DIM) for h in range(heads)]
    v_rows = [slice(h * (HEAD_DIM + ONES_ROWS), (h + 1) * (HEAD_DIM + ONES_ROWS)) for h in range(heads)]
    key_pos = lax.broadcasted_iota(jnp.int32, (tq, tq), 0)
    query_pos = lax.broadcasted_iota(jnp.int32, (tq, tq), 1)
    causal = key_pos <= query_pos

    m_ref[...] = jnp.full(m_ref.shape, MASKED, F32)
    acc_ref[...] = jnp.zeros(acc_ref.shape, F32)

    s_refs = (s0_ref, s1_ref)

    def logits(j, slot, diagonal):
        rows = pl.ds(pl.multiple_of(j * tq, tq), tq)
        for h in range(heads):
            s = jnp.dot(kaug_ref[rows, qk_cols[h]], qaug_ref[qk_cols[h], :], preferred_element_type=F32)
            s_refs[slot][h] = jnp.where(causal, s, MASKED) if diagonal else s

    def softmax_pv(j, slot):
        probs, rescale = [], []
        for h in range(heads):
            s = s_refs[slot][h]
            cq = crow_ref[i, h:h + 1, :]
            m_old = m_ref[h]
            block_max = jnp.max(functools.reduce(jnp.maximum, _row_groups(s)), axis=0, keepdims=True)
            m_new = jnp.maximum(m_old, block_max + cq)
            m_ref[h] = m_new
            probs.append(jnp.exp2(s + (cq - m_new)).astype(BF16))
            rescale.append(jnp.exp2(m_old - m_new))
        for h in range(heads):
            acc_ref[h] = rescale[h] * acc_ref[h] + jnp.dot(vaug_ref[j, v_rows[h], :], probs[h],
                                                           preferred_element_type=F32)

    @pl.when(i == 0)
    def _():
        logits(0, 0, True)
        softmax_pv(0, 0)

    def stage(j_next, slot_next, diagonal, j, slot):
        logits(j_next, slot_next, diagonal)
        softmax_pv(j, slot)

    n_pairs = lax.shift_right_logical(i - 1, 1)
    j0 = 2 * n_pairs

    @pl.when(i > 0)
    def _():
        logits(0, 0, False)

        def pair(t, carry):
            j = 2 * t
            stage(j + 1, 1, False, j, 0)
            stage(j + 2, 0, False, j + 1, 1)
            return carry

        lax.fori_loop(0, n_pairs, pair, 0)

    @pl.when(jnp.logical_and(i > 0, i - j0 == 1))
    def _():
        stage(i, 1, True, j0, 0)
        softmax_pv(i, 1)

    @pl.when(jnp.logical_and(i > 0, i - j0 == 2))
    def _():
        stage(j0 + 1, 1, False, j0, 0)
        stage(i, 0, True, j0 + 1, 1)
        softmax_pv(i, 0)

    for h, hs in enumerate(head_cols):
        row_sum = acc_ref[h, HEAD_DIM:HEAD_DIM + 1, :]
        o_ref[:, hs] = (acc_ref[h, 0:HEAD_DIM, :] * (1.0 / row_sum)).T.astype(BF16)


def _attention(qaug, kaug, vaug, c_rows, *, batch, seq_len):
    nt, qk_rows, tq = qaug.shape
    nq = seq_len // tq
    heads = qk_rows // (2 * HEAD_DIM)
    d = heads * HEAD_DIM
    v_rows = vaug.shape[1]
    assert nt == batch * nq and tq % V7X_LANES == 0 and v_rows == heads * (HEAD_DIM + ONES_ROWS)
    scratch = [pltpu.VMEM((heads, tq, tq), F32), pltpu.VMEM((heads, tq, tq), F32),
               pltpu.VMEM((heads, 1, tq), F32),
               pltpu.VMEM((heads, HEAD_DIM + ONES_ROWS, tq), F32)]
    vmem = (2 * (_nbytes((seq_len, qk_rows), BF16) + _nbytes((nq, v_rows, tq), BF16)
                 + _nbytes((qk_rows, tq), BF16) + _nbytes((tq, d), BF16)
                 + _nbytes((nq, V7X_SUBLANES, tq), F32))
            + sum(_nbytes(s.shape, s.dtype) for s in scratch)
            + _nbytes((V7X_SUBLANES * heads, tq), F32)
            + 16 * _nbytes((tq, tq), F32))
    return pl.pallas_call(
        _attn_kernel,
        grid=(batch, nq),
        in_specs=[pl.BlockSpec((None, qk_rows, tq), lambda b, i: (b * nq + i, 0, 0)),
                  pl.BlockSpec((seq_len, qk_rows), lambda b, i: (b, 0)),
                  pl.BlockSpec((nq, v_rows, tq), lambda b, i: (b, 0, 0)),
                  pl.BlockSpec((nq, heads, tq), lambda b, i: (b, 0, 0))],
        out_specs=pl.BlockSpec((tq, d), lambda b, i: (b * nq + i, 0)),
        out_shape=jax.ShapeDtypeStruct((nt * tq, d), BF16),
        scratch_shapes=scratch,
        compiler_params=_params(("arbitrary", "arbitrary"), vmem),
        name="attention",
    )(qaug, kaug, vaug, c_rows)


def _mixer_out_kernel(x_ref, a_ref, part_ref, g2_ref, woa_ref, wo_ref, g_ref, wg_ref, wu_ref, wd_ref,
                      o_ref, act_ref):
    yc = jnp.dot(a_ref[...], woa_ref[...], preferred_element_type=F32)
    merged = part_ref[...] + g2_ref[...] * yc
    x = x_ref[...] + jnp.dot(merged.astype(BF16), wo_ref[...], preferred_element_type=F32)
    o_ref[...] = _swiglu_half_step(x, g_ref, wg_ref, wu_ref, wd_ref, act_ref)


def _mixer_out(x, attn, part, g2, w_mix, w_ffn, layer):
    t, d = x.shape
    d_ff = w_ffn["w_down"].shape[1]
    tm = TOKEN_TILE_FFN
    assert t % tm == 0
    tile = pl.BlockSpec((tm, d), lambda i: (i, 0))
    vmem = (_ffn_vmem(tm, d, d_ff) + 2 * _nbytes((d, d), BF16)
            + 2 * (4 * _nbytes((tm, d), F32) + _nbytes((tm, d), BF16)) + 4 * _nbytes((tm, d), F32))
    return pl.pallas_call(
        _mixer_out_kernel,
        grid=(t // tm,),
        in_specs=[tile, tile, tile, tile, _layer(w_mix["w_out_attn"], layer), _layer(w_mix["w_o"], layer)]
        + _ffn_weight_specs(w_ffn, layer),
        out_specs=tile,
        out_shape=jax.ShapeDtypeStruct((t, d), F32),
        scratch_shapes=[pltpu.VMEM((tm, d_ff), BF16)],
        compiler_params=_params(("arbitrary",), vmem),
        name="mixer_out",
    )(x, attn, part, g2, w_mix["w_out_attn"], w_mix["w_o"], *_ffn_weight_args(w_ffn))


def kernel(x, ffn1_norm, ffn1_w_gu, ffn1_w_down, mix_norm, w_in, b_forget, b_gate, conv_w, sgu_ln_g, sgu_ln_b,
           sgu_w, sgu_b, q_norm_g, k_norm_g, w_out_conv, w_out_sgu, w_out_attn, w_o, ffn2_norm, ffn2_w_gu,
           ffn2_w_down):
    batch, seq_len, d = x.shape
    depth = w_in.shape[0]
    heads = d // HEAD_DIM
    bf = lambda a: a.astype(BF16)
    rows = lambda a: a.reshape(depth, 1, -1)

    ffn1 = {"norm": rows(ffn1_norm), "w_gu": bf(ffn1_w_gu), "w_down": bf(ffn1_w_down)}
    ffn2 = {"norm": rows(ffn2_norm), "w_gu": bf(ffn2_w_gu), "w_down": bf(ffn2_w_down)}
    mix = {
        "mix_norm": rows(mix_norm),
        "w_main": bf(w_in[:, :, :8 * d]),
        "w_forget": bf(jnp.pad(w_in[:, :, 8 * d:8 * d + heads], ((0, 0), (0, 0), (0, V7X_LANES - heads)))),
        "w_gate": bf(w_in[:, :, 8 * d + heads:]),
        "b_forget": jnp.pad(rows(b_forget), ((0, 0), (0, 0), (0, V7X_LANES - heads))),
        "b_gate": rows(b_gate),
        "conv_w": conv_w,
        "sgu_ln_g": rows(sgu_ln_g),
        "sgu_ln_b": rows(sgu_ln_b),
        "sgu_w": sgu_w,
        "sgu_b_t": jnp.swapaxes(sgu_b, 1, 2),
        "q_norm_g": rows(q_norm_g),
        "k_norm_g": rows(k_norm_g),
        "w_out_conv": bf(w_out_conv),
        "w_out_sgu": bf(w_out_sgu),
        "w_out_attn": bf(w_out_attn),
        "w_o": bf(w_o),
    }

    xt = x.reshape(batch * seq_len, d)
    for layer in range(depth):
        xt = _ffn(xt, ffn1, layer)
        part, g2, qaug, kaug, vaug, c_rows = _mixer_in(xt, mix, layer, seq_len=seq_len)
        attn = _attention(qaug, kaug, vaug, c_rows, batch=batch, seq_len=seq_len)
        xt = _mixer_out(xt, attn, part, g2, mix, ffn2, layer)
    return xt.reshape(batch, seq_len, d)
```

```python
import functools
import math

import jax
import jax.numpy as jnp
from jax import lax
from jax.experimental import pallas as pl
from jax.experimental.pallas import tpu as pltpu

F32 = jnp.float32
BF16 = jnp.bfloat16

RMS_EPS = 1e-6
LN_EPS = 1e-5
SGU_CHUNK = 128
HEAD_DIM = 128
N_BRANCH = 3
MASKED = -1e30
FORGET_SPLIT = 3
ONES_ROWS = 16

V7X_LANES = 128
V7X_SUBLANES = 8
BF16_SUBLANES = 16
V7X_VMEM_BYTES = 64 * 1024 * 1024
LOAD_STEPS = 16
FF_CHUNK = 1024

TOKEN_TILE_FFN = 512
SEQ_BLOCK = 256


def _layer(arr, layer, cols=None, col_block=0):
    block = (None,) + tuple(arr.shape[1:-1]) + (arr.shape[-1] if cols is None else cols,)
    index = (layer,) + (0,) * (arr.ndim - 2) + (col_block,)
    return pl.BlockSpec(block, lambda *_: index, pipeline_mode=pl.Buffered(1))


def _weight_chunk(arr, layer, n_load):
    rows = arr.shape[1] // n_load
    assert rows * n_load == arr.shape[1] and rows % BF16_SUBLANES == 0, (arr.shape, n_load)
    return pl.BlockSpec((None, rows, arr.shape[2]), lambda s: (layer, jnp.minimum(s, n_load - 1), 0))


def _token_tile(block, n_load):
    return pl.BlockSpec(block, lambda s: (jnp.maximum(s - n_load, 0),) + (0,) * (len(block) - 1))


def _stash_rows(src_ref, dst_ref, step, cols=None):
    rows = src_ref.shape[0]
    chunk = src_ref[...] if cols is None else src_ref[:, cols]
    dst_ref[pl.ds(pl.multiple_of(step * rows, rows), rows), :] = chunk.astype(BF16)


def _nbytes(shape, dtype):
    return math.prod(shape) * jnp.dtype(dtype).itemsize


def _params(semantics, vmem_bytes):
    assert vmem_bytes <= V7X_VMEM_BYTES, vmem_bytes
    return pltpu.CompilerParams(dimension_semantics=semantics, vmem_limit_bytes=int(vmem_bytes))


def _rms_norm(x, g):
    return x * lax.rsqrt(jnp.mean(x * x, axis=-1, keepdims=True) + RMS_EPS) * g


def _gelu(x):
    return 0.5 * x * (1.0 + lax.erf(x * (2.0 ** -0.5)))


def _ff_chunks(d_ff):
    return [(c, min(c + FF_CHUNK, d_ff)) for c in range(0, d_ff, FF_CHUNK)]


def _row_groups(x):
    return [x[r:r + V7X_SUBLANES] for r in range(0, x.shape[0], V7X_SUBLANES)]


def _swiglu_half_step(x, g_ref, wgu_ref, wd_ref, act_ref):
    d_ff = wd_ref.shape[0]
    h = _rms_norm(x, g_ref[...]).astype(BF16)
    for c0, c1 in _ff_chunks(d_ff):
        g = jnp.dot(h, wgu_ref[:, c0:c1], preferred_element_type=F32)
        u = jnp.dot(h, wgu_ref[:, d_ff + c0:d_ff + c1], preferred_element_type=F32)
        act_ref[:, c0:c1] = (g * jax.nn.sigmoid(g) * u).astype(BF16)
    return x + 0.5 * jnp.dot(act_ref[...], wd_ref[...], preferred_element_type=F32)


def _ffn_weight_specs(w, layer, n_load):
    return [_layer(w["norm"], layer), _weight_chunk(w["w_gu"], layer, n_load),
            _weight_chunk(w["w_down"], layer, n_load)]


def _ffn_weight_args(w):
    return [w["norm"], w["w_gu"], w["w_down"]]


def _ffn_scratch(tm, w):
    d, d_gu = w["w_gu"].shape[1:]
    d_ff = w["w_down"].shape[1]
    return [pltpu.VMEM((d, d_gu), BF16), pltpu.VMEM((d_ff, d), BF16), pltpu.VMEM((tm, d_ff), BF16)]


def _ffn_vmem(tm, d, d_ff, n_load):
    return (_nbytes((d, 2 * d_ff), BF16) + _nbytes((d_ff, d), BF16)
            + 2 * (_nbytes((d, 2 * d_ff), F32) + _nbytes((d_ff, d), F32)) // n_load
            + _nbytes((tm, d_ff), BF16)
            + 4 * _nbytes((tm, FF_CHUNK), F32))


def _ffn_kernel(x_ref, g_ref, wgu32_ref, wd32_ref, o_ref, wgu_ref, wd_ref, act_ref, *, n_load):
    step = pl.program_id(0)

    @pl.when(step < n_load)
    def _():
        _stash_rows(wgu32_ref, wgu_ref, step)
        _stash_rows(wd32_ref, wd_ref, step)

    @pl.when(step >= n_load)
    def _():
        o_ref[...] = _swiglu_half_step(x_ref[...], g_ref, wgu_ref, wd_ref, act_ref)


def _ffn(x, w, layer):
    t, d = x.shape
    d_ff = w["w_down"].shape[1]
    tm = TOKEN_TILE_FFN
    n_load = LOAD_STEPS
    assert t % tm == 0
    tile = _token_tile((tm, d), n_load)
    vmem = _ffn_vmem(tm, d, d_ff, n_load) + 4 * _nbytes((tm, d), F32)
    return pl.pallas_call(
        functools.partial(_ffn_kernel, n_load=n_load),
        grid=(n_load + t // tm,),
        in_specs=[tile] + _ffn_weight_specs(w, layer, n_load),
        out_specs=tile,
        out_shape=jax.ShapeDtypeStruct((t, d), F32),
        scratch_shapes=_ffn_scratch(tm, w),
        compiler_params=_params(("arbitrary",), vmem),
        name="ffn",
    )(x, *_ffn_weight_args(w))


def _mixer_in_kernel(x_ref, ng_ref, win32_ref, woc32_ref, wos32_ref, bf_ref, bg_ref, cw_ref, lng_ref, lnb_ref,
                     sw_ref, sb_ref, qg_ref, kg_ref,
                     part_ref, g2_ref, qaug_ref, kaug_ref, vaug_ref, crow_ref,
                     wm_ref, wf_ref, wgt_ref, woc_ref, wos_ref, zs_ref, ccarry_ref, vn_ref, yb_ref, *,
                     n_load, tiles_per_seq, q_scale, c_scale):
    tm, d = x_ref.shape
    heads = d // HEAD_DIM
    step = pl.program_id(0)

    @pl.when(step < n_load)
    def _():
        _stash_rows(win32_ref, wm_ref, step, slice(0, 8 * d))
        _stash_rows(win32_ref, wf_ref, step, slice(8 * d, 8 * d + V7X_LANES))
        _stash_rows(win32_ref, wgt_ref, step, slice(8 * d + heads, 8 * d + heads + N_BRANCH * d))
        _stash_rows(woc32_ref, woc_ref, step)
        _stash_rows(wos32_ref, wos_ref, step)

    @pl.when(step >= n_load)
    def _():
        _mixer_in_tile(step - n_load, x_ref, ng_ref, wm_ref, wf_ref, wgt_ref, bf_ref, bg_ref, cw_ref, lng_ref,
                       lnb_ref, sw_ref, sb_ref, qg_ref, kg_ref, woc_ref, wos_ref,
                       part_ref, g2_ref, qaug_ref, kaug_ref, vaug_ref, crow_ref,
                       zs_ref, ccarry_ref, vn_ref, yb_ref,
                       tiles_per_seq=tiles_per_seq, q_scale=q_scale, c_scale=c_scale)


def _mixer_in_tile(tile_idx, x_ref, ng_ref, wm_ref, wf_ref, wgt_ref, bf_ref, bg_ref, cw_ref, lng_ref, lnb_ref,
                   sw_ref, sb_ref, qg_ref, kg_ref, woc_ref, wos_ref,
                   part_ref, g2_ref, qaug_ref, kaug_ref, vaug_ref, crow_ref,
                   zs_ref, ccarry_ref, vn_ref, yb_ref, *, tiles_per_seq, q_scale, c_scale):
    tm, d = x_ref.shape
    heads = d // HEAD_DIM
    pad = V7X_SUBLANES

    @pl.when(tile_idx % tiles_per_seq == 0)
    def _():
        zs_ref[0:pad, :] = jnp.zeros((pad, d), F32)
        ccarry_ref[...] = jnp.zeros_like(ccarry_ref)

    h = _rms_norm(x_ref[...], ng_ref[...]).astype(BF16)

    def proj(w_ref, c0, c1):
        return jnp.dot(h, w_ref[:, c0:c1], preferred_element_type=F32)

    pa = proj(wm_ref, 0, 3 * d)
    f = proj(wf_ref, 0, V7X_LANES) + bf_ref[...]
    ps = proj(wm_ref, 3 * d, 5 * d)
    pq = proj(wm_ref, 5 * d, 8 * d)
    gates = [jax.nn.sigmoid(proj(wgt_ref, b * d, (b + 1) * d) + bg_ref[:, b * d:(b + 1) * d])
             for b in range(N_BRANCH)]

    zs_ref[pad:pad + tm, :] = pa[:, d:2 * d] * pa[:, 2 * d:3 * d]
    conv = (cw_ref[0:1, :] * zs_ref[pad - 2:pad - 2 + tm, :]
            + cw_ref[1:2, :] * zs_ref[pad - 1:pad - 1 + tm, :]
            + cw_ref[2:3, :] * zs_ref[pad:pad + tm, :])
    ya_in = (pa[:, 0:d] * conv).astype(BF16)
    zs_ref[0:pad, :] = zs_ref[tm:tm + pad, :]

    c = jnp.minimum(f, 0.0) - jnp.log1p(jnp.exp(-jnp.abs(f)))
    t_idx = lax.broadcasted_iota(jnp.int32, c.shape, 0)
    shift = 1
    while shift < tm:
        c = c + jnp.where(t_idx >= shift, pltpu.roll(c, shift, axis=0), 0.0)
        shift *= 2
    c = c + ccarry_ref[0:1, :]
    ccarry_ref[0:1, :] = c[tm - 1:tm, :]
    c = c * c_scale
    crow_ref[...] = c.T[0:heads, :]

    u = _gelu(ps[:, 0:d])
    vv = _gelu(ps[:, d:2 * d])
    mu = jnp.mean(vv, axis=-1, keepdims=True)
    vc = vv - mu
    var = jnp.mean(vc * vc, axis=-1, keepdims=True)
    vn_ref[...] = (vc * lax.rsqrt(var + LN_EPS) * lng_ref[...] + lnb_ref[...]).astype(BF16)

    lane = lax.broadcasted_iota(jnp.int32, (tm, HEAD_DIM), 1)
    feature = lax.broadcasted_iota(jnp.int32, (HEAD_DIM, tm), 0)
    ones_rows = jnp.where(feature < FORGET_SPLIT, 1.0, 0.0).astype(BF16)
    qn = []
    for g in range(heads):
        hs = slice(g * HEAD_DIM, (g + 1) * HEAD_DIM)
        qn.append(_rms_norm(pq[:, hs], qg_ref[:, hs]) * q_scale)
        ks = slice(d + g * HEAD_DIM, d + (g + 1) * HEAD_DIM)
        kaug_ref[:, 2 * g * HEAD_DIM:(2 * g + 1) * HEAD_DIM] = _rms_norm(pq[:, ks], kg_ref[:, hs]).astype(BF16)
        rest = -jnp.broadcast_to(c[:, g:g + 1], (tm, HEAD_DIM))
        slab = jnp.zeros((tm, HEAD_DIM), F32)
        for term in range(FORGET_SPLIT):
            piece = rest.astype(BF16).astype(F32)
            slab = jnp.where(lane == term, piece, slab)
            rest = rest - piece
        kaug_ref[:, (2 * g + 1) * HEAD_DIM:(2 * g + 2) * HEAD_DIM] = slab.astype(BF16)
    qt = jnp.concatenate(qn, axis=1).T.astype(BF16)
    vt = pq[:, 2 * d:3 * d].T.astype(BF16)
    v_rows = HEAD_DIM + ONES_ROWS
    for g in range(heads):
        hs = slice(g * HEAD_DIM, (g + 1) * HEAD_DIM)
        qaug_ref[2 * g * HEAD_DIM:(2 * g + 1) * HEAD_DIM, :] = qt[hs, :]
        qaug_ref[(2 * g + 1) * HEAD_DIM:(2 * g + 2) * HEAD_DIM, :] = ones_rows
        vaug_ref[g * v_rows:g * v_rows + HEAD_DIM, :] = vt[hs, :]
        vaug_ref[g * v_rows + HEAD_DIM:(g + 1) * v_rows, :] = jnp.ones((ONES_ROWS, tm), BF16)

    ya = jnp.dot(ya_in, woc_ref[...], preferred_element_type=F32)
    n_chunks = tm // SGU_CHUNK
    pos_t = lax.broadcasted_iota(jnp.int32, (SGU_CHUNK, SGU_CHUNK), 0)
    pos_s = lax.broadcasted_iota(jnp.int32, (SGU_CHUNK, SGU_CHUNK), 1)
    for g in range(heads):
        hs = slice(g * HEAD_DIM, (g + 1) * HEAD_DIM)
        w = jnp.where(pos_s <= pos_t, sw_ref[g], 0.0).astype(BF16)
        rhs = jnp.concatenate([vn_ref[c0 * SGU_CHUNK:(c0 + 1) * SGU_CHUNK, hs] for c0 in range(n_chunks)], axis=1)
        s = jnp.dot(w, rhs, preferred_element_type=F32) + sb_ref[:, g:g + 1]
        for c0 in range(n_chunks):
            rows = slice(c0 * SGU_CHUNK, (c0 + 1) * SGU_CHUNK)
            yb_ref[rows, hs] = (u[rows, hs] * s[:, c0 * SGU_CHUNK:(c0 + 1) * SGU_CHUNK]).astype(BF16)
    yb = jnp.dot(yb_ref[...], wos_ref[...], preferred_element_type=F32)
    part_ref[...] = gates[0] * ya + gates[1] * yb
    g2_ref[...] = gates[2]


def _mixer_in(x, w, layer, *, seq_len):
    t, d = x.shape
    tm = SEQ_BLOCK
    heads = d // HEAD_DIM
    assert seq_len % tm == 0 and tm % SGU_CHUNK == 0 and d % HEAD_DIM == 0 and heads <= V7X_SUBLANES
    n_load = LOAD_STEPS
    n_in = w["w_in"].shape[2]
    assert n_in == 8 * d + heads + N_BRANCH * d
    kernel = functools.partial(
        _mixer_in_kernel, n_load=n_load, tiles_per_seq=seq_len // tm,
        q_scale=HEAD_DIM ** -0.5 * math.log2(math.e), c_scale=math.log2(math.e))
    streamed = ["w_in", "w_out_conv", "w_out_sgu"]
    small = ["b_forget", "b_gate", "conv_w", "sgu_ln_g", "sgu_ln_b", "sgu_w", "sgu_b_t", "q_norm_g", "k_norm_g"]
    weight_specs = ([_layer(w["mix_norm"], layer)] + [_weight_chunk(w[n], layer, n_load) for n in streamed]
                    + [_layer(w[n], layer) for n in small])
    weight_args = [w["mix_norm"]] + [w[n] for n in streamed] + [w[n] for n in small]
    v_rows = heads * (HEAD_DIM + ONES_ROWS)
    tile = _token_tile((tm, d), n_load)
    resident = [pltpu.VMEM((d, 8 * d), BF16), pltpu.VMEM((d, V7X_LANES), BF16), pltpu.VMEM((d, N_BRANCH * d), BF16),
                pltpu.VMEM((d, d), BF16), pltpu.VMEM((d, d), BF16)]
    scratch = resident + [pltpu.VMEM((tm + V7X_SUBLANES, d), F32), pltpu.VMEM((V7X_SUBLANES, V7X_LANES), F32),
                          pltpu.VMEM((tm, d), BF16), pltpu.VMEM((tm, d), BF16)]
    vmem = (sum(_nbytes(s.shape, s.dtype) for s in scratch)
            + 2 * _nbytes((d, n_in + 2 * d), F32) // n_load
            + sum(_nbytes(w[n].shape[1:], w[n].dtype) for n in small)
            + 2 * (3 * _nbytes((tm, d), F32) + 4 * _nbytes((tm, d), BF16) + _nbytes((v_rows, tm), BF16)
                   + _nbytes((V7X_SUBLANES, tm), F32))
            + 16 * _nbytes((tm, d), F32))
    return pl.pallas_call(
        kernel,
        grid=(n_load + t // tm,),
        in_specs=[tile] + weight_specs,
        out_specs=[tile, tile,
                   _token_tile((None, 2 * d, tm), n_load),
                   _token_tile((tm, 2 * d), n_load),
                   _token_tile((None, v_rows, tm), n_load),
                   _token_tile((None, heads, tm), n_load)],
        out_shape=[jax.ShapeDtypeStruct((t, d), F32), jax.ShapeDtypeStruct((t, d), F32),
                   jax.ShapeDtypeStruct((t // tm, 2 * d, tm), BF16), jax.ShapeDtypeStruct((t, 2 * d), BF16),
                   jax.ShapeDtypeStruct((t // tm, v_rows, tm), BF16),
                   jax.ShapeDtypeStruct((t // tm, heads, tm), F32)],
        scratch_shapes=scratch,
        compiler_params=_params(("arbitrary",), vmem),
        name="mixer_in",
    )(x, *weight_args)


def _attn_kernel(qaug_ref, kaug_ref, vaug_ref, crow_ref, o_ref, s0_ref, s1_ref, m_ref, acc_ref):
    tq = qaug_ref.shape[1]
    heads = m_ref.shape[0]
    i = pl.program_id(1)
    head_cols = [slice(h * HEAD_DIM, (h + 1) * HEAD_DIM) for h in range(heads)]
    qk_cols = [slice(h * 2 * HEAD_DIM, (h + 1) * 2 * HEAD_DIM) for h in range(heads)]
    v_rows = [slice(h * (HEAD_DIM + ONES_ROWS), (h + 1) * (HEAD_DIM + ONES_ROWS)) for h in range(heads)]
    key_pos = lax.broadcasted_iota(jnp.int32, (tq, tq), 0)
    query_pos = lax.broadcasted_iota(jnp.int32, (tq, tq), 1)
    causal = key_pos <= query_pos

    m_ref[...] = jnp.full(m_ref.shape, MASKED, F32)
    acc_ref[...] = jnp.zeros(acc_ref.shape, F32)

    s_refs = (s0_ref, s1_ref)

    def logits(j, slot, diagonal):
        rows = pl.ds(pl.multiple_of(j * tq, tq), tq)
        for h in range(heads):
            s = jnp.dot(kaug_ref[rows, qk_cols[h]], qaug_ref[qk_cols[h], :], preferred_element_type=F32)
            s_refs[slot][h] = jnp.where(causal, s, MASKED) if diagonal else s

    def softmax_pv(j, slot):
        probs, rescale = [], []
        for h in range(heads):
            s = s_refs[slot][h]
            cq = crow_ref[i, h:h + 1, :]
            m_old = m_ref[h]
            block_max = jnp.max(functools.reduce(jnp.maximum, _row_groups(s)), axis=0, keepdims=True)
            m_new = jnp.maximum(m_old, block_max + cq)
            m_ref[h] = m_new
            probs.append(jnp.exp2(s + (cq - m_new)).astype(BF16))
            rescale.append(jnp.exp2(m_old - m_new))
        for h in range(heads):
            acc_ref[h] = rescale[h] * acc_ref[h] + jnp.dot(vaug_ref[j, v_rows[h], :], probs[h],
                                                           preferred_element_type=F32)

    @pl.when(i == 0)
    def _():
        logits(0, 0, True)
        softmax_pv(0, 0)

    def stage(j_next, slot_next, diagonal, j, slot):
        logits(j_next, slot_next, diagonal)
        softmax_pv(j, slot)

    n_pairs = lax.shift_right_logical(i - 1, 1)
    j0 = 2 * n_pairs

    @pl.when(i > 0)
    def _():
        logits(0, 0, False)

        def pair(t, carry):
            j = 2 * t
            stage(j + 1, 1, False, j, 0)
            stage(j + 2, 0, False, j + 1, 1)
            return carry

        lax.fori_loop(0, n_pairs, pair, 0)

    @pl.when(jnp.logical_and(i > 0, i - j0 == 1))
    def _():
        stage(i, 1, True, j0, 0)
        softmax_pv(i, 1)

    @pl.when(jnp.logical_and(i > 0, i - j0 == 2))
    def _():
        stage(j0 + 1, 1, False, j0, 0)
        stage(i, 0, True, j0 + 1, 1)
        softmax_pv(i, 0)

    for h, hs in enumerate(head_cols):
        row_sum = acc_ref[h, HEAD_DIM:HEAD_DIM + 1, :]
        o_ref[:, hs] = (acc_ref[h, 0:HEAD_DIM, :] * (1.0 / row_sum)).T.astype(BF16)


def _attention(qaug, kaug, vaug, c_rows, *, batch, seq_len):
    nt, qk_rows, tq = qaug.shape
    nq = seq_len // tq
    heads = qk_rows // (2 * HEAD_DIM)
    d = heads * HEAD_DIM
    v_rows = vaug.shape[1]
    assert nt == batch * nq and tq % V7X_LANES == 0 and v_rows == heads * (HEAD_DIM + ONES_ROWS)
    scratch = [pltpu.VMEM((heads, tq, tq), F32), pltpu.VMEM((heads, tq, tq), F32),
               pltpu.VMEM((heads, 1, tq), F32),
               pltpu.VMEM((heads, HEAD_DIM + ONES_ROWS, tq), F32)]
    vmem = (2 * (_nbytes((seq_len, qk_rows), BF16) + _nbytes((nq, v_rows, tq), BF16)
                 + _nbytes((qk_rows, tq), BF16) + _nbytes((tq, d), BF16)
                 + _nbytes((nq, V7X_SUBLANES, tq), F32))
            + sum(_nbytes(s.shape, s.dtype) for s in scratch)
            + _nbytes((V7X_SUBLANES * heads, tq), F32)
            + 16 * _nbytes((tq, tq), F32))
    return pl.pallas_call(
        _attn_kernel,
        grid=(batch, nq),
        in_specs=[pl.BlockSpec((None, qk_rows, tq), lambda b, i: (b * nq + i, 0, 0)),
                  pl.BlockSpec((seq_len, qk_rows), lambda b, i: (b, 0)),
                  pl.BlockSpec((nq, v_rows, tq), lambda b, i: (b, 0, 0)),
                  pl.BlockSpec((nq, heads, tq), lambda b, i: (b, 0, 0))],
        out_specs=pl.BlockSpec((tq, d), lambda b, i: (b * nq + i, 0)),
        out_shape=jax.ShapeDtypeStruct((nt * tq, d), BF16),
        scratch_shapes=scratch,
        compiler_params=_params(("arbitrary", "arbitrary"), vmem),
        name="attention",
    )(qaug, kaug, vaug, c_rows)


def _mixer_out_kernel(x_ref, a_ref, part_ref, g2_ref, woa32_ref, wo32_ref, g_ref, wgu32_ref, wd32_ref,
                      o_ref, woa_ref, wo_ref, wgu_ref, wd_ref, act_ref, *, n_load):
    step = pl.program_id(0)

    @pl.when(step < n_load)
    def _():
        _stash_rows(woa32_ref, woa_ref, step)
        _stash_rows(wo32_ref, wo_ref, step)
        _stash_rows(wgu32_ref, wgu_ref, step)
        _stash_rows(wd32_ref, wd_ref, step)

    @pl.when(step >= n_load)
    def _():
        yc = jnp.dot(a_ref[...], woa_ref[...], preferred_element_type=F32)
        merged = part_ref[...] + g2_ref[...] * yc
        x = x_ref[...] + jnp.dot(merged.astype(BF16), wo_ref[...], preferred_element_type=F32)
        o_ref[...] = _swiglu_half_step(x, g_ref, wgu_ref, wd_ref, act_ref)


def _mixer_out(x, attn, part, g2, w_mix, w_ffn, layer):
    t, d = x.shape
    d_ff = w_ffn["w_down"].shape[1]
    tm = TOKEN_TILE_FFN
    n_load = LOAD_STEPS
    assert t % tm == 0
    tile = _token_tile((tm, d), n_load)
    vmem = (_ffn_vmem(tm, d, d_ff, n_load) + 2 * _nbytes((d, d), BF16) + 4 * _nbytes((d, d), F32) // n_load
            + 2 * (4 * _nbytes((tm, d), F32) + _nbytes((tm, d), BF16)) + 2 * _nbytes((tm, d), F32))
    return pl.pallas_call(
        functools.partial(_mixer_out_kernel, n_load=n_load),
        grid=(n_load + t // tm,),
        in_specs=[tile, tile, tile, tile,
                  _weight_chunk(w_mix["w_out_attn"], layer, n_load), _weight_chunk(w_mix["w_o"], layer, n_load)]
        + _ffn_weight_specs(w_ffn, layer, n_load),
        out_specs=tile,
        out_shape=jax.ShapeDtypeStruct((t, d), F32),
        scratch_shapes=[pltpu.VMEM((d, d), BF16), pltpu.VMEM((d, d), BF16)] + _ffn_scratch(tm, w_ffn),
        compiler_params=_params(("arbitrary",), vmem),
        name="mixer_out",
    )(x, attn, part, g2, w_mix["w_out_attn"], w_mix["w_o"], *_ffn_weight_args(w_ffn))


def kernel(x, ffn1_norm, ffn1_w_gu, ffn1_w_down, mix_norm, w_in, b_forget, b_gate, conv_w, sgu_ln_g, sgu_ln_b,
           sgu_w, sgu_b, q_norm_g, k_norm_g, w_out_conv, w_out_sgu, w_out_attn, w_o, ffn2_norm, ffn2_w_gu,
           ffn2_w_down):
    batch, seq_len, d = x.shape
    depth = w_in.shape[0]
    heads = d // HEAD_DIM
    rows = lambda a: a.reshape(depth, 1, -1)

    ffn1 = {"norm": rows(ffn1_norm), "w_gu": ffn1_w_gu, "w_down": ffn1_w_down}
    ffn2 = {"norm": rows(ffn2_norm), "w_gu": ffn2_w_gu, "w_down": ffn2_w_down}
    mix = {
        "mix_norm": rows(mix_norm),
        "w_in": w_in,
        "b_forget": jnp.pad(rows(b_forget), ((0, 0), (0, 0), (0, V7X_LANES - heads))),
        "b_gate": rows(b_gate),
        "conv_w": conv_w,
        "sgu_ln_g": rows(sgu_ln_g),
        "sgu_ln_b": rows(sgu_ln_b),
        "sgu_w": sgu_w,
        "sgu_b_t": jnp.swapaxes(sgu_b, 1, 2),
        "q_norm_g": rows(q_norm_g),
        "k_norm_g": rows(k_norm_g),
        "w_out_conv": w_out_conv,
        "w_out_sgu": w_out_sgu,
        "w_out_attn": w_out_attn,
        "w_o": w_o,
    }

    xt = x.reshape(batch * seq_len, d)
    for layer in range(depth):
        xt = _ffn(xt, ffn1, layer)
        part, g2, qaug, kaug, vaug, c_rows = _mixer_in(xt, mix, layer, seq_len=seq_len)
        attn = _attention(qaug, kaug, vaug, c_rows, batch=batch, seq_len=seq_len)
        xt = _mixer_out(xt, attn, part, g2, mix, ffn2, layer)
    return xt.reshape(batch, seq_len, d)
```

```python
import functools
import math

import jax
import jax.numpy as jnp
from jax import lax
from jax.experimental import pallas as pl
from jax.experimental.pallas import tpu as pltpu

F32 = jnp.float32
BF16 = jnp.bfloat16

RMS_EPS = 1e-6
LN_EPS = 1e-5
SGU_CHUNK = 128
HEAD_DIM = 128
N_BRANCH = 3
MASKED = -1e30
FORGET_SPLIT = 3
ONES_ROWS = 16

V7X_LANES = 128
V7X_SUBLANES = 8
BF16_SUBLANES = 16
V7X_VMEM_BYTES = 64 * 1024 * 1024
LOAD_STEPS = 16
FF_CHUNK = 1024

TOKEN_TILE_FFN = 512
SEQ_BLOCK = 256


def _layer(arr, layer, cols=None, col_block=0):
    block = (None,) + tuple(arr.shape[1:-1]) + (arr.shape[-1] if cols is None else cols,)
    index = (layer,) + (0,) * (arr.ndim - 2) + (col_block,)
    return pl.BlockSpec(block, lambda *_: index, pipeline_mode=pl.Buffered(1))


def _weight_chunk(arr, layer, n_load):
    rows = arr.shape[1] // n_load
    assert rows * n_load == arr.shape[1] and rows % BF16_SUBLANES == 0, (arr.shape, n_load)
    return pl.BlockSpec((None, rows, arr.shape[2]), lambda s: (layer, jnp.minimum(s, n_load - 1), 0))


def _token_tile(block, n_load):
    return pl.BlockSpec(block, lambda s: (jnp.maximum(s - n_load, 0),) + (0,) * (len(block) - 1))


def _stash_rows(src_ref, dst_ref, step, cols=None):
    rows = src_ref.shape[0]
    chunk = src_ref[...] if cols is None else src_ref[:, cols]
    dst_ref[pl.ds(pl.multiple_of(step * rows, rows), rows), :] = chunk.astype(BF16)


def _nbytes(shape, dtype):
    return math.prod(shape) * jnp.dtype(dtype).itemsize


def _params(semantics, vmem_bytes):
    assert vmem_bytes <= V7X_VMEM_BYTES, vmem_bytes
    return pltpu.CompilerParams(dimension_semantics=semantics, vmem_limit_bytes=int(vmem_bytes))


def _rms_norm(x, g):
    return x * lax.rsqrt(jnp.mean(x * x, axis=-1, keepdims=True) + RMS_EPS) * g


def _gelu(x):
    return 0.5 * x * (1.0 + lax.erf(x * (2.0 ** -0.5)))


def _ff_chunks(d_ff):
    return [(c, min(c + FF_CHUNK, d_ff)) for c in range(0, d_ff, FF_CHUNK)]


def _row_groups(x):
    return [x[r:r + V7X_SUBLANES] for r in range(0, x.shape[0], V7X_SUBLANES)]


def _swiglu_half_step(x, g_ref, wgu_ref, wd_ref, act_ref):
    d_ff = wd_ref.shape[0]
    h = _rms_norm(x, g_ref[...]).astype(BF16)
    for c0, c1 in _ff_chunks(d_ff):
        g = jnp.dot(h, wgu_ref[:, c0:c1], preferred_element_type=F32)
        u = jnp.dot(h, wgu_ref[:, d_ff + c0:d_ff + c1], preferred_element_type=F32)
        act_ref[:, c0:c1] = (g * jax.nn.sigmoid(g) * u).astype(BF16)
    return x + 0.5 * jnp.dot(act_ref[...], wd_ref[...], preferred_element_type=F32)


def _ffn_weight_specs(w, layer, n_load):
    return [_layer(w["norm"], layer), _weight_chunk(w["w_gu"], layer, n_load),
            _weight_chunk(w["w_down"], layer, n_load)]


def _ffn_weight_args(w):
    return [w["norm"], w["w_gu"], w["w_down"]]


def _ffn_scratch(tm, w):
    d, d_gu = w["w_gu"].shape[1:]
    d_ff = w["w_down"].shape[1]
    return [pltpu.VMEM((d, d_gu), BF16), pltpu.VMEM((d_ff, d), BF16), pltpu.VMEM((tm, d_ff), BF16)]


def _ffn_vmem(tm, d, d_ff, n_load):
    return (_nbytes((d, 2 * d_ff), BF16) + _nbytes((d_ff, d), BF16)
            + 2 * (_nbytes((d, 2 * d_ff), F32) + _nbytes((d_ff, d), F32)) // n_load
            + _nbytes((tm, d_ff), BF16)
            + 4 * _nbytes((tm, FF_CHUNK), F32))


def _ffn_kernel(x_ref, g_ref, wgu32_ref, wd32_ref, o_ref, wgu_ref, wd_ref, act_ref, *, n_load):
    step = pl.program_id(0)

    @pl.when(step < n_load)
    def _():
        _stash_rows(wgu32_ref, wgu_ref, step)
        _stash_rows(wd32_ref, wd_ref, step)

    @pl.when(step >= n_load)
    def _():
        o_ref[...] = _swiglu_half_step(x_ref[...], g_ref, wgu_ref, wd_ref, act_ref)


def _ffn(x, w, layer):
    t, d = x.shape
    d_ff = w["w_down"].shape[1]
    tm = TOKEN_TILE_FFN
    n_load = LOAD_STEPS
    assert t % tm == 0
    tile = _token_tile((tm, d), n_load)
    vmem = _ffn_vmem(tm, d, d_ff, n_load) + 4 * _nbytes((tm, d), F32)
    return pl.pallas_call(
        functools.partial(_ffn_kernel, n_load=n_load),
        grid=(n_load + t // tm,),
        in_specs=[tile] + _ffn_weight_specs(w, layer, n_load),
        out_specs=tile,
        out_shape=jax.ShapeDtypeStruct((t, d), F32),
        scratch_shapes=_ffn_scratch(tm, w),
        compiler_params=_params(("arbitrary",), vmem),
        name="ffn",
    )(x, *_ffn_weight_args(w))


def _mixer_in_kernel(x_ref, ng_ref, wm_ref, wf_ref, wgt_ref, woc32_ref, wos32_ref, bf_ref, bg_ref, cw_ref, lng_ref,
                     lnb_ref, sw_ref, sb_ref, qg_ref, kg_ref,
                     part_ref, g2_ref, qaug_ref, kaug_ref, vaug_ref, crow_ref,
                     woc_ref, wos_ref, zs_ref, ccarry_ref, vn_ref, yb_ref, *,
                     n_load, tiles_per_seq, q_scale, c_scale):
    step = pl.program_id(0)

    @pl.when(step < n_load)
    def _():
        _stash_rows(woc32_ref, woc_ref, step)
        _stash_rows(wos32_ref, wos_ref, step)

    @pl.when(step >= n_load)
    def _():
        _mixer_in_tile(step - n_load, x_ref, ng_ref, wm_ref, wf_ref, wgt_ref, bf_ref, bg_ref, cw_ref, lng_ref,
                       lnb_ref, sw_ref, sb_ref, qg_ref, kg_ref, woc_ref, wos_ref,
                       part_ref, g2_ref, qaug_ref, kaug_ref, vaug_ref, crow_ref,
                       zs_ref, ccarry_ref, vn_ref, yb_ref,
                       tiles_per_seq=tiles_per_seq, q_scale=q_scale, c_scale=c_scale)


def _mixer_in_tile(tile_idx, x_ref, ng_ref, wm_ref, wf_ref, wgt_ref, bf_ref, bg_ref, cw_ref, lng_ref, lnb_ref,
                   sw_ref, sb_ref, qg_ref, kg_ref, woc_ref, wos_ref,
                   part_ref, g2_ref, qaug_ref, kaug_ref, vaug_ref, crow_ref,
                   zs_ref, ccarry_ref, vn_ref, yb_ref, *, tiles_per_seq, q_scale, c_scale):
    tm, d = x_ref.shape
    heads = d // HEAD_DIM
    pad = V7X_SUBLANES

    @pl.when(tile_idx % tiles_per_seq == 0)
    def _():
        zs_ref[0:pad, :] = jnp.zeros((pad, d), F32)
        ccarry_ref[...] = jnp.zeros_like(ccarry_ref)

    h = _rms_norm(x_ref[...], ng_ref[...]).astype(BF16)

    def proj(w_ref, c0, c1):
        return jnp.dot(h, w_ref[:, c0:c1], preferred_element_type=F32)

    pa = proj(wm_ref, 0, 3 * d)
    f = proj(wf_ref, 0, V7X_LANES) + bf_ref[...]
    ps = proj(wm_ref, 3 * d, 5 * d)
    pq = proj(wm_ref, 5 * d, 8 * d)
    gates = [jax.nn.sigmoid(proj(wgt_ref, b * d, (b + 1) * d) + bg_ref[:, b * d:(b + 1) * d])
             for b in range(N_BRANCH)]

    zs_ref[pad:pad + tm, :] = pa[:, d:2 * d] * pa[:, 2 * d:3 * d]
    conv = (cw_ref[0:1, :] * zs_ref[pad - 2:pad - 2 + tm, :]
            + cw_ref[1:2, :] * zs_ref[pad - 1:pad - 1 + tm, :]
            + cw_ref[2:3, :] * zs_ref[pad:pad + tm, :])
    ya_in = (pa[:, 0:d] * conv).astype(BF16)
    zs_ref[0:pad, :] = zs_ref[tm:tm + pad, :]

    c = jnp.minimum(f, 0.0) - jnp.log1p(jnp.exp(-jnp.abs(f)))
    t_idx = lax.broadcasted_iota(jnp.int32, c.shape, 0)
    shift = 1
    while shift < tm:
        c = c + jnp.where(t_idx >= shift, pltpu.roll(c, shift, axis=0), 0.0)
        shift *= 2
    c = c + ccarry_ref[0:1, :]
    ccarry_ref[0:1, :] = c[tm - 1:tm, :]
    c = c * c_scale
    crow_ref[...] = c.T[0:heads, :]

    u = _gelu(ps[:, 0:d])
    vv = _gelu(ps[:, d:2 * d])
    mu = jnp.mean(vv, axis=-1, keepdims=True)
    vc = vv - mu
    var = jnp.mean(vc * vc, axis=-1, keepdims=True)
    vn_ref[...] = (vc * lax.rsqrt(var + LN_EPS) * lng_ref[...] + lnb_ref[...]).astype(BF16)

    lane = lax.broadcasted_iota(jnp.int32, (tm, HEAD_DIM), 1)
    feature = lax.broadcasted_iota(jnp.int32, (HEAD_DIM, tm), 0)
    ones_rows = jnp.where(feature < FORGET_SPLIT, 1.0, 0.0).astype(BF16)
    qn = []
    for g in range(heads):
        hs = slice(g * HEAD_DIM, (g + 1) * HEAD_DIM)
        qn.append(_rms_norm(pq[:, hs], qg_ref[:, hs]) * q_scale)
        ks = slice(d + g * HEAD_DIM, d + (g + 1) * HEAD_DIM)
        kaug_ref[:, 2 * g * HEAD_DIM:(2 * g + 1) * HEAD_DIM] = _rms_norm(pq[:, ks], kg_ref[:, hs]).astype(BF16)
        rest = -jnp.broadcast_to(c[:, g:g + 1], (tm, HEAD_DIM))
        slab = jnp.zeros((tm, HEAD_DIM), F32)
        for term in range(FORGET_SPLIT):
            piece = rest.astype(BF16).astype(F32)
            slab = jnp.where(lane == term, piece, slab)
            rest = rest - piece
        kaug_ref[:, (2 * g + 1) * HEAD_DIM:(2 * g + 2) * HEAD_DIM] = slab.astype(BF16)
    qt = jnp.concatenate(qn, axis=1).T.astype(BF16)
    vt = pq[:, 2 * d:3 * d].T.astype(BF16)
    v_rows = HEAD_DIM + ONES_ROWS
    for g in range(heads):
        hs = slice(g * HEAD_DIM, (g + 1) * HEAD_DIM)
        qaug_ref[2 * g * HEAD_DIM:(2 * g + 1) * HEAD_DIM, :] = qt[hs, :]
        qaug_ref[(2 * g + 1) * HEAD_DIM:(2 * g + 2) * HEAD_DIM, :] = ones_rows
        vaug_ref[g * v_rows:g * v_rows + HEAD_DIM, :] = vt[hs, :]
        vaug_ref[g * v_rows + HEAD_DIM:(g + 1) * v_rows, :] = jnp.ones((ONES_ROWS, tm), BF16)

    ya = jnp.dot(ya_in, woc_ref[...], preferred_element_type=F32)
    n_chunks = tm // SGU_CHUNK
    pos_t = lax.broadcasted_iota(jnp.int32, (SGU_CHUNK, SGU_CHUNK), 0)
    pos_s = lax.broadcasted_iota(jnp.int32, (SGU_CHUNK, SGU_CHUNK), 1)
    for g in range(heads):
        hs = slice(g * HEAD_DIM, (g + 1) * HEAD_DIM)
        w = jnp.where(pos_s <= pos_t, sw_ref[g], 0.0).astype(BF16)
        rhs = jnp.concatenate([vn_ref[c0 * SGU_CHUNK:(c0 + 1) * SGU_CHUNK, hs] for c0 in range(n_chunks)], axis=1)
        s = jnp.dot(w, rhs, preferred_element_type=F32) + sb_ref[:, g:g + 1]
        for c0 in range(n_chunks):
            rows = slice(c0 * SGU_CHUNK, (c0 + 1) * SGU_CHUNK)
            yb_ref[rows, hs] = (u[rows, hs] * s[:, c0 * SGU_CHUNK:(c0 + 1) * SGU_CHUNK]).astype(BF16)
    yb = jnp.dot(yb_ref[...], wos_ref[...], preferred_element_type=F32)
    part_ref[...] = (gates[0] * ya + gates[1] * yb).astype(BF16)
    g2_ref[...] = gates[2].astype(BF16)


def _mixer_in(x, w, layer, *, seq_len):
    t, d = x.shape
    tm = SEQ_BLOCK
    heads = d // HEAD_DIM
    assert seq_len % tm == 0 and tm % SGU_CHUNK == 0 and d % HEAD_DIM == 0 and heads <= V7X_SUBLANES
    n_load = LOAD_STEPS
    kernel = functools.partial(
        _mixer_in_kernel, n_load=n_load, tiles_per_seq=seq_len // tm,
        q_scale=HEAD_DIM ** -0.5 * math.log2(math.e), c_scale=math.log2(math.e))
    resident = ["mix_norm", "w_main", "w_forget", "w_gate"]
    streamed = ["w_out_conv", "w_out_sgu"]
    small = ["b_forget", "b_gate", "conv_w", "sgu_ln_g", "sgu_ln_b", "sgu_w", "sgu_b_t", "q_norm_g", "k_norm_g"]
    weight_specs = ([_layer(w[n], layer) for n in resident] + [_weight_chunk(w[n], layer, n_load) for n in streamed]
                    + [_layer(w[n], layer) for n in small])
    weight_args = [w[n] for n in resident + streamed + small]
    v_rows = heads * (HEAD_DIM + ONES_ROWS)
    tile = _token_tile((tm, d), n_load)
    scratch = [pltpu.VMEM((d, d), BF16), pltpu.VMEM((d, d), BF16),
               pltpu.VMEM((tm + V7X_SUBLANES, d), F32), pltpu.VMEM((V7X_SUBLANES, V7X_LANES), F32),
               pltpu.VMEM((tm, d), BF16), pltpu.VMEM((tm, d), BF16)]
    vmem = (sum(_nbytes(s.shape, s.dtype) for s in scratch)
            + 2 * _nbytes((d, 2 * d), F32) // n_load
            + sum(_nbytes(w[n].shape[1:], w[n].dtype) for n in resident + small)
            + 2 * (_nbytes((tm, d), F32) + 6 * _nbytes((tm, d), BF16) + _nbytes((v_rows, tm), BF16)
                   + _nbytes((V7X_SUBLANES, tm), F32))
            + 16 * _nbytes((tm, d), F32))
    return pl.pallas_call(
        kernel,
        grid=(n_load + t // tm,),
        in_specs=[tile] + weight_specs,
        out_specs=[tile, tile,
                   _token_tile((None, 2 * d, tm), n_load),
                   _token_tile((tm, 2 * d), n_load),
                   _token_tile((None, v_rows, tm), n_load),
                   _token_tile((None, heads, tm), n_load)],
        out_shape=[jax.ShapeDtypeStruct((t, d), BF16), jax.ShapeDtypeStruct((t, d), BF16),
                   jax.ShapeDtypeStruct((t // tm, 2 * d, tm), BF16), jax.ShapeDtypeStruct((t, 2 * d), BF16),
                   jax.ShapeDtypeStruct((t // tm, v_rows, tm), BF16),
                   jax.ShapeDtypeStruct((t // tm, heads, tm), F32)],
        scratch_shapes=scratch,
        compiler_params=_params(("arbitrary",), vmem),
        name="mixer_in",
    )(x, *weight_args)


def _attn_kernel(qaug_ref, kaug_ref, vaug_ref, crow_ref, o_ref, s0_ref, s1_ref, m_ref, acc_ref):
    tq = qaug_ref.shape[1]
    heads = m_ref.shape[0]
    i = pl.program_id(1)
    head_cols = [slice(h * HEAD_DIM, (h + 1) * HEAD_DIM) for h in range(heads)]
    qk_cols = [slice(h * 2 * HEAD_DIM, (h + 1) * 2 * HEAD_DIM) for h in range(heads)]
    v_rows = [slice(h * (HEAD_DIM + ONES_ROWS), (h + 1) * (HEAD_DIM + ONES_ROWS)) for h in range(heads)]
    key_pos = lax.broadcasted_iota(jnp.int32, (tq, tq), 0)
    query_pos = lax.broadcasted_iota(jnp.int32, (tq, tq), 1)
    causal = key_pos <= query_pos

    m_ref[...] = jnp.full(m_ref.shape, MASKED, F32)
    acc_ref[...] = jnp.zeros(acc_ref.shape, F32)

    s_refs = (s0_ref, s1_ref)

    def logits(j, slot, diagonal):
        rows = pl.ds(pl.multiple_of(j * tq, tq), tq)
        for h in range(heads):
            s = jnp.dot(kaug_ref[rows, qk_cols[h]], qaug_ref[qk_cols[h], :], preferred_element_type=F32)
            s_refs[slot][h] = jnp.where(causal, s, MASKED) if diagonal else s

    def softmax_pv(j, slot):
        probs, rescale = [], []
        for h in range(heads):
            s = s_refs[slot][h]
            cq = crow_ref[i, h:h + 1, :]
            m_old = m_ref[h]
            block_max = jnp.max(functools.reduce(jnp.maximum, _row_groups(s)), axis=0, keepdims=True)
            m_new = jnp.maximum(m_old, block_max + cq)
            m_ref[h] = m_new
            probs.append(jnp.exp2(s + (cq - m_new)).astype(BF16))
            rescale.append(jnp.exp2(m_old - m_new))
        for h in range(heads):
            acc_ref[h] = rescale[h] * acc_ref[h] + jnp.dot(vaug_ref[j, v_rows[h], :], probs[h],
                                                           preferred_element_type=F32)

    @pl.when(i == 0)
    def _():
        logits(0, 0, True)
        softmax_pv(0, 0)

    def stage(j_next, slot_next, diagonal, j, slot):
        logits(j_next, slot_next, diagonal)
        softmax_pv(j, slot)

    n_pairs = lax.shift_right_logical(i - 1, 1)
    j0 = 2 * n_pairs

    @pl.when(i > 0)
    def _():
        logits(0, 0, False)

        def pair(t, carry):
            j = 2 * t
            stage(j + 1, 1, False, j, 0)
            stage(j + 2, 0, False, j + 1, 1)
            return carry

        lax.fori_loop(0, n_pairs, pair, 0)

    @pl.when(jnp.logical_and(i > 0, i - j0 == 1))
    def _():
        stage(i, 1, True, j0, 0)
        softmax_pv(i, 1)

    @pl.when(jnp.logical_and(i > 0, i - j0 == 2))
    def _():
        stage(j0 + 1, 1, False, j0, 0)
        stage(i, 0, True, j0 + 1, 1)
        softmax_pv(i, 0)

    for h, hs in enumerate(head_cols):
        row_sum = acc_ref[h, HEAD_DIM:HEAD_DIM + 1, :]
        o_ref[:, hs] = (acc_ref[h, 0:HEAD_DIM, :] * (1.0 / row_sum)).T.astype(BF16)


def _attention(qaug, kaug, vaug, c_rows, *, batch, seq_len):
    nt, qk_rows, tq = qaug.shape
    nq = seq_len // tq
    heads = qk_rows // (2 * HEAD_DIM)
    d = heads * HEAD_DIM
    v_rows = vaug.shape[1]
    assert nt == batch * nq and tq % V7X_LANES == 0 and v_rows == heads * (HEAD_DIM + ONES_ROWS)
    scratch = [pltpu.VMEM((heads, tq, tq), F32), pltpu.VMEM((heads, tq, tq), F32),
               pltpu.VMEM((heads, 1, tq), F32),
               pltpu.VMEM((heads, HEAD_DIM + ONES_ROWS, tq), F32)]
    vmem = (2 * (_nbytes((seq_len, qk_rows), BF16) + _nbytes((nq, v_rows, tq), BF16)
                 + _nbytes((qk_rows, tq), BF16) + _nbytes((tq, d), BF16)
                 + _nbytes((nq, V7X_SUBLANES, tq), F32))
            + sum(_nbytes(s.shape, s.dtype) for s in scratch)
            + _nbytes((V7X_SUBLANES * heads, tq), F32)
            + 16 * _nbytes((tq, tq), F32))
    return pl.pallas_call(
        _attn_kernel,
        grid=(batch, nq),
        in_specs=[pl.BlockSpec((None, qk_rows, tq), lambda b, i: (b * nq + i, 0, 0)),
                  pl.BlockSpec((seq_len, qk_rows), lambda b, i: (b, 0)),
                  pl.BlockSpec((nq, v_rows, tq), lambda b, i: (b, 0, 0)),
                  pl.BlockSpec((nq, heads, tq), lambda b, i: (b, 0, 0))],
        out_specs=pl.BlockSpec((tq, d), lambda b, i: (b * nq + i, 0)),
        out_shape=jax.ShapeDtypeStruct((nt * tq, d), BF16),
        scratch_shapes=scratch,
        compiler_params=_params(("arbitrary", "arbitrary"), vmem),
        name="attention",
    )(qaug, kaug, vaug, c_rows)


def _mixer_out_kernel(x_ref, a_ref, part_ref, g2_ref, woa32_ref, wo32_ref, g_ref, wgu32_ref, wd32_ref,
                      o_ref, woa_ref, wo_ref, wgu_ref, wd_ref, act_ref, *, n_load):
    step = pl.program_id(0)

    @pl.when(step < n_load)
    def _():
        _stash_rows(woa32_ref, woa_ref, step)
        _stash_rows(wo32_ref, wo_ref, step)
        _stash_rows(wgu32_ref, wgu_ref, step)
        _stash_rows(wd32_ref, wd_ref, step)

    @pl.when(step >= n_load)
    def _():
        yc = jnp.dot(a_ref[...], woa_ref[...], preferred_element_type=F32)
        merged = part_ref[...].astype(F32) + g2_ref[...].astype(F32) * yc
        x = x_ref[...] + jnp.dot(merged.astype(BF16), wo_ref[...], preferred_element_type=F32)
        o_ref[...] = _swiglu_half_step(x, g_ref, wgu_ref, wd_ref, act_ref)


def _mixer_out(x, attn, part, g2, w_mix, w_ffn, layer):
    t, d = x.shape
    d_ff = w_ffn["w_down"].shape[1]
    tm = TOKEN_TILE_FFN
    n_load = LOAD_STEPS
    assert t % tm == 0
    tile = _token_tile((tm, d), n_load)
    vmem = (_ffn_vmem(tm, d, d_ff, n_load) + 2 * _nbytes((d, d), BF16) + 4 * _nbytes((d, d), F32) // n_load
            + 2 * (2 * _nbytes((tm, d), F32) + 3 * _nbytes((tm, d), BF16)) + 2 * _nbytes((tm, d), F32))
    return pl.pallas_call(
        functools.partial(_mixer_out_kernel, n_load=n_load),
        grid=(n_load + t // tm,),
        in_specs=[tile, tile, tile, tile,
                  _weight_chunk(w_mix["w_out_attn"], layer, n_load), _weight_chunk(w_mix["w_o"], layer, n_load)]
        + _ffn_weight_specs(w_ffn, layer, n_load),
        out_specs=tile,
        out_shape=jax.ShapeDtypeStruct((t, d), F32),
        scratch_shapes=[pltpu.VMEM((d, d), BF16), pltpu.VMEM((d, d), BF16)] + _ffn_scratch(tm, w_ffn),
        compiler_params=_params(("arbitrary",), vmem),
        name="mixer_out",
    )(x, attn, part, g2, w_mix["w_out_attn"], w_mix["w_o"], *_ffn_weight_args(w_ffn))


def kernel(x, ffn1_norm, ffn1_w_gu, ffn1_w_down, mix_norm, w_in, b_forget, b_gate, conv_w, sgu_ln_g, sgu_ln_b,
           sgu_w, sgu_b, q_norm_g, k_norm_g, w_out_conv, w_out_sgu, w_out_attn, w_o, ffn2_norm, ffn2_w_gu,
           ffn2_w_down):
    batch, seq_len, d = x.shape
    depth = w_in.shape[0]
    heads = d // HEAD_DIM
    rows = lambda a: a.reshape(depth, 1, -1)

    ffn1 = {"norm": rows(ffn1_norm), "w_gu": ffn1_w_gu, "w_down": ffn1_w_down}
    ffn2 = {"norm": rows(ffn2_norm), "w_gu": ffn2_w_gu, "w_down": ffn2_w_down}
    mix = {
        "mix_norm": rows(mix_norm),
        "w_main": w_in[:, :, :8 * d].astype(BF16),
        "w_forget": jnp.pad(w_in[:, :, 8 * d:8 * d + heads], ((0, 0), (0, 0), (0, V7X_LANES - heads))).astype(BF16),
        "w_gate": w_in[:, :, 8 * d + heads:].astype(BF16),
        "b_forget": jnp.pad(rows(b_forget), ((0, 0), (0, 0), (0, V7X_LANES - heads))),
        "b_gate": rows(b_gate),
        "conv_w": conv_w,
        "sgu_ln_g": rows(sgu_ln_g),
        "sgu_ln_b": rows(sgu_ln_b),
        "sgu_w": sgu_w,
        "sgu_b_t": jnp.swapaxes(sgu_b, 1, 2),
        "q_norm_g": rows(q_norm_g),
        "k_norm_g": rows(k_norm_g),
        "w_out_conv": w_out_conv,
        "w_out_sgu": w_out_sgu,
        "w_out_attn": w_out_attn,
        "w_o": w_o,
    }

    xt = x.reshape(batch * seq_len, d)
    for layer in range(depth):
        xt = _ffn(xt, ffn1, layer)
        part, g2, qaug, kaug, vaug, c_rows = _mixer_in(xt, mix, layer, seq_len=seq_len)
        attn = _attention(qaug, kaug, vaug, c_rows, batch=batch, seq_len=seq_len)
        xt = _mixer_out(xt, attn, part, g2, mix, ffn2, layer)
    return xt.reshape(batch, seq_len, d)
```

```python
import functools
import math

import jax
import jax.numpy as jnp
from jax import lax
from jax.experimental import pallas as pl
from jax.experimental.pallas import tpu as pltpu

F32 = jnp.float32
BF16 = jnp.bfloat16

RMS_EPS = 1e-6
LN_EPS = 1e-5
SGU_CHUNK = 128
HEAD_DIM = 128
N_BRANCH = 3
MASKED = -1e30
FORGET_SPLIT = 3
ONES_ROWS = 16

V7X_LANES = 128
V7X_SUBLANES = 8
BF16_SUBLANES = 16
V7X_VMEM_BYTES = 64 * 1024 * 1024
LOAD_STEPS = 16
FF_CHUNK = 1024

TOKEN_TILE_FFN = 512
SEQ_BLOCK = 256


def _layer(arr, layer, cols=None, col_block=0):
    block = (None,) + tuple(arr.shape[1:-1]) + (arr.shape[-1] if cols is None else cols,)
    index = (layer,) + (0,) * (arr.ndim - 2) + (col_block,)
    return pl.BlockSpec(block, lambda *_: index, pipeline_mode=pl.Buffered(1))


def _weight_chunk(arr, layer, n_load, layer_rows=None):
    rows = (arr.shape[1] if layer_rows is None else layer_rows) // n_load
    assert rows % BF16_SUBLANES == 0, (arr.shape, n_load)
    if layer_rows is None:
        assert rows * n_load == arr.shape[1]
        return pl.BlockSpec((None, rows, arr.shape[2]), lambda s: (layer, jnp.minimum(s, n_load - 1), 0))
    assert rows * n_load == layer_rows and arr.shape[0] % layer_rows == 0
    return pl.BlockSpec((rows, arr.shape[1]), lambda s: (layer * n_load + jnp.minimum(s, n_load - 1), 0))


def _token_tile(block, n_load):
    return pl.BlockSpec(block, lambda s: (jnp.maximum(s - n_load, 0),) + (0,) * (len(block) - 1))


def _stash_rows(src_ref, dst_ref, step, cols=None):
    rows = src_ref.shape[0]
    chunk = src_ref[...] if cols is None else src_ref[:, cols]
    dst_ref[pl.ds(pl.multiple_of(step * rows, rows), rows), :] = chunk.astype(BF16)


def _nbytes(shape, dtype):
    return math.prod(shape) * jnp.dtype(dtype).itemsize


def _params(semantics, vmem_bytes):
    assert vmem_bytes <= V7X_VMEM_BYTES, vmem_bytes
    return pltpu.CompilerParams(dimension_semantics=semantics, vmem_limit_bytes=int(vmem_bytes))


def _rms_norm(x, g):
    return x * lax.rsqrt(jnp.mean(x * x, axis=-1, keepdims=True) + RMS_EPS) * g


def _gelu(x):
    return 0.5 * x * (1.0 + lax.erf(x * (2.0 ** -0.5)))


def _ff_chunks(d_ff):
    return [(c, min(c + FF_CHUNK, d_ff)) for c in range(0, d_ff, FF_CHUNK)]


def _row_groups(x):
    return [x[r:r + V7X_SUBLANES] for r in range(0, x.shape[0], V7X_SUBLANES)]


def _swiglu_half_step(x, g_ref, wgu_ref, wd_ref, act_ref):
    d_ff = wd_ref.shape[0]
    h = _rms_norm(x, g_ref[...]).astype(BF16)
    for c0, c1 in _ff_chunks(d_ff):
        g = jnp.dot(h, wgu_ref[:, c0:c1], preferred_element_type=F32)
        u = jnp.dot(h, wgu_ref[:, d_ff + c0:d_ff + c1], preferred_element_type=F32)
        act_ref[:, c0:c1] = (g * jax.nn.sigmoid(g) * u).astype(BF16)
    return x + 0.5 * jnp.dot(act_ref[...], wd_ref[...], preferred_element_type=F32)


def _ffn_weight_specs(w, layer, n_load):
    return [_layer(w["norm"], layer), _weight_chunk(w["w_gu"], layer, n_load),
            _weight_chunk(w["w_down"], layer, n_load)]


def _ffn_weight_args(w):
    return [w["norm"], w["w_gu"], w["w_down"]]


def _ffn_scratch(tm, w):
    d, d_gu = w["w_gu"].shape[1:]
    d_ff = w["w_down"].shape[1]
    return [pltpu.VMEM((d, d_gu), BF16), pltpu.VMEM((d_ff, d), BF16), pltpu.VMEM((tm, d_ff), BF16)]


def _ffn_vmem(tm, d, d_ff, n_load):
    return (_nbytes((d, 2 * d_ff), BF16) + _nbytes((d_ff, d), BF16)
            + 2 * (_nbytes((d, 2 * d_ff), F32) + _nbytes((d_ff, d), F32)) // n_load
            + _nbytes((tm, d_ff), BF16)
            + 4 * _nbytes((tm, FF_CHUNK), F32))


def _ffn_kernel(x_ref, g_ref, wgu32_ref, wd32_ref, o_ref, wgu_ref, wd_ref, act_ref, *, n_load):
    step = pl.program_id(0)

    @pl.when(step < n_load)
    def _():
        _stash_rows(wgu32_ref, wgu_ref, step)
        _stash_rows(wd32_ref, wd_ref, step)

    @pl.when(step >= n_load)
    def _():
        o_ref[...] = _swiglu_half_step(x_ref[...], g_ref, wgu_ref, wd_ref, act_ref)


def _ffn(x, w, layer):
    t, d = x.shape
    d_ff = w["w_down"].shape[1]
    tm = TOKEN_TILE_FFN
    n_load = LOAD_STEPS
    assert t % tm == 0
    tile = _token_tile((tm, d), n_load)
    vmem = _ffn_vmem(tm, d, d_ff, n_load) + 4 * _nbytes((tm, d), F32)
    return pl.pallas_call(
        functools.partial(_ffn_kernel, n_load=n_load),
        grid=(n_load + t // tm,),
        in_specs=[tile] + _ffn_weight_specs(w, layer, n_load),
        out_specs=tile,
        out_shape=jax.ShapeDtypeStruct((t, d), F32),
        scratch_shapes=_ffn_scratch(tm, w),
        compiler_params=_params(("arbitrary",), vmem),
        name="ffn",
    )(x, *_ffn_weight_args(w))


def _mixer_in_kernel(x_ref, ng_ref, win32_ref, woc32_ref, wos32_ref, bf_ref, bg_ref, cw_ref, lng_ref, lnb_ref,
                     sw_ref, sb_ref, qg_ref, kg_ref,
                     part_ref, g2_ref, qaug_ref, kaug_ref, vaug_ref, crow_ref,
                     wm_ref, wf_ref, wgt_ref, woc_ref, wos_ref, zs_ref, ccarry_ref, vn_ref, yb_ref, *,
                     n_load, tiles_per_seq, q_scale, c_scale):
    tm, d = x_ref.shape
    heads = d // HEAD_DIM
    step = pl.program_id(0)

    @pl.when(step < n_load)
    def _():
        _stash_rows(win32_ref, wm_ref, step, slice(0, 8 * d))
        _stash_rows(win32_ref, wf_ref, step, slice(8 * d, 8 * d + V7X_LANES))
        _stash_rows(win32_ref, wgt_ref, step, slice(8 * d + heads, 8 * d + heads + N_BRANCH * d))
        _stash_rows(woc32_ref, woc_ref, step)
        _stash_rows(wos32_ref, wos_ref, step)

    @pl.when(step >= n_load)
    def _():
        _mixer_in_tile(step - n_load, x_ref, ng_ref, wm_ref, wf_ref, wgt_ref, bf_ref, bg_ref, cw_ref, lng_ref,
                       lnb_ref, sw_ref, sb_ref, qg_ref, kg_ref, woc_ref, wos_ref,
                       part_ref, g2_ref, qaug_ref, kaug_ref, vaug_ref, crow_ref,
                       zs_ref, ccarry_ref, vn_ref, yb_ref,
                       tiles_per_seq=tiles_per_seq, q_scale=q_scale, c_scale=c_scale)


def _mixer_in_tile(tile_idx, x_ref, ng_ref, wm_ref, wf_ref, wgt_ref, bf_ref, bg_ref, cw_ref, lng_ref, lnb_ref,
                   sw_ref, sb_ref, qg_ref, kg_ref, woc_ref, wos_ref,
                   part_ref, g2_ref, qaug_ref, kaug_ref, vaug_ref, crow_ref,
                   zs_ref, ccarry_ref, vn_ref, yb_ref, *, tiles_per_seq, q_scale, c_scale):
    tm, d = x_ref.shape
    heads = d // HEAD_DIM
    pad = V7X_SUBLANES

    @pl.when(tile_idx % tiles_per_seq == 0)
    def _():
        zs_ref[0:pad, :] = jnp.zeros((pad, d), F32)
        ccarry_ref[...] = jnp.zeros_like(ccarry_ref)

    h = _rms_norm(x_ref[...], ng_ref[...]).astype(BF16)

    def proj(w_ref, c0, c1):
        return jnp.dot(h, w_ref[:, c0:c1], preferred_element_type=F32)

    pa = proj(wm_ref, 0, 3 * d)
    f = proj(wf_ref, 0, V7X_LANES) + bf_ref[...]
    ps = proj(wm_ref, 3 * d, 5 * d)
    pq = proj(wm_ref, 5 * d, 8 * d)
    gates = [jax.nn.sigmoid(proj(wgt_ref, b * d, (b + 1) * d) + bg_ref[:, b * d:(b + 1) * d])
             for b in range(N_BRANCH)]

    zs_ref[pad:pad + tm, :] = pa[:, d:2 * d] * pa[:, 2 * d:3 * d]
    conv = (cw_ref[0:1, :] * zs_ref[pad - 2:pad - 2 + tm, :]
            + cw_ref[1:2, :] * zs_ref[pad - 1:pad - 1 + tm, :]
            + cw_ref[2:3, :] * zs_ref[pad:pad + tm, :])
    ya_in = (pa[:, 0:d] * conv).astype(BF16)
    zs_ref[0:pad, :] = zs_ref[tm:tm + pad, :]

    c = jnp.minimum(f, 0.0) - jnp.log1p(jnp.exp(-jnp.abs(f)))
    t_idx = lax.broadcasted_iota(jnp.int32, c.shape, 0)
    shift = 1
    while shift < tm:
        c = c + jnp.where(t_idx >= shift, pltpu.roll(c, shift, axis=0), 0.0)
        shift *= 2
    c = c + ccarry_ref[0:1, :]
    ccarry_ref[0:1, :] = c[tm - 1:tm, :]
    c = c * c_scale
    crow_ref[...] = c.T[0:heads, :]

    u = _gelu(ps[:, 0:d])
    vv = _gelu(ps[:, d:2 * d])
    mu = jnp.mean(vv, axis=-1, keepdims=True)
    vc = vv - mu
    var = jnp.mean(vc * vc, axis=-1, keepdims=True)
    vn_ref[...] = (vc * lax.rsqrt(var + LN_EPS) * lng_ref[...] + lnb_ref[...]).astype(BF16)

    lane = lax.broadcasted_iota(jnp.int32, (tm, HEAD_DIM), 1)
    feature = lax.broadcasted_iota(jnp.int32, (HEAD_DIM, tm), 0)
    ones_rows = jnp.where(feature < FORGET_SPLIT, 1.0, 0.0).astype(BF16)
    qn = []
    for g in range(heads):
        hs = slice(g * HEAD_DIM, (g + 1) * HEAD_DIM)
        qn.append(_rms_norm(pq[:, hs], qg_ref[:, hs]) * q_scale)
        ks = slice(d + g * HEAD_DIM, d + (g + 1) * HEAD_DIM)
        kaug_ref[:, 2 * g * HEAD_DIM:(2 * g + 1) * HEAD_DIM] = _rms_norm(pq[:, ks], kg_ref[:, hs]).astype(BF16)
        rest = -jnp.broadcast_to(c[:, g:g + 1], (tm, HEAD_DIM))
        slab = jnp.zeros((tm, HEAD_DIM), F32)
        for term in range(FORGET_SPLIT):
            piece = rest.astype(BF16).astype(F32)
            slab = jnp.where(lane == term, piece, slab)
            rest = rest - piece
        kaug_ref[:, (2 * g + 1) * HEAD_DIM:(2 * g + 2) * HEAD_DIM] = slab.astype(BF16)
    qt = jnp.concatenate(qn, axis=1).T.astype(BF16)
    vt = pq[:, 2 * d:3 * d].T.astype(BF16)
    v_rows = HEAD_DIM + ONES_ROWS
    for g in range(heads):
        hs = slice(g * HEAD_DIM, (g + 1) * HEAD_DIM)
        qaug_ref[2 * g * HEAD_DIM:(2 * g + 1) * HEAD_DIM, :] = qt[hs, :]
        qaug_ref[(2 * g + 1) * HEAD_DIM:(2 * g + 2) * HEAD_DIM, :] = ones_rows
        vaug_ref[g * v_rows:g * v_rows + HEAD_DIM, :] = vt[hs, :]
        vaug_ref[g * v_rows + HEAD_DIM:(g + 1) * v_rows, :] = jnp.ones((ONES_ROWS, tm), BF16)

    ya = jnp.dot(ya_in, woc_ref[...], preferred_element_type=F32)
    n_chunks = tm // SGU_CHUNK
    pos_t = lax.broadcasted_iota(jnp.int32, (SGU_CHUNK, SGU_CHUNK), 0)
    pos_s = lax.broadcasted_iota(jnp.int32, (SGU_CHUNK, SGU_CHUNK), 1)
    for g in range(heads):
        hs = slice(g * HEAD_DIM, (g + 1) * HEAD_DIM)
        w = jnp.where(pos_s <= pos_t, sw_ref[g], 0.0).astype(BF16)
        rhs = jnp.concatenate([vn_ref[c0 * SGU_CHUNK:(c0 + 1) * SGU_CHUNK, hs] for c0 in range(n_chunks)], axis=1)
        s = jnp.dot(w, rhs, preferred_element_type=F32) + sb_ref[:, g:g + 1]
        for c0 in range(n_chunks):
            rows = slice(c0 * SGU_CHUNK, (c0 + 1) * SGU_CHUNK)
            yb_ref[rows, hs] = (u[rows, hs] * s[:, c0 * SGU_CHUNK:(c0 + 1) * SGU_CHUNK]).astype(BF16)
    yb = jnp.dot(yb_ref[...], wos_ref[...], preferred_element_type=F32)
    part_ref[...] = gates[0] * ya + gates[1] * yb
    g2_ref[...] = gates[2]


def _mixer_in(x, w, layer, *, seq_len):
    t, d = x.shape
    tm = SEQ_BLOCK
    heads = d // HEAD_DIM
    assert seq_len % tm == 0 and tm % SGU_CHUNK == 0 and d % HEAD_DIM == 0 and heads <= V7X_SUBLANES
    n_load = LOAD_STEPS
    n_in = w["w_in"].shape[-1]
    assert n_in == 8 * d + heads + N_BRANCH * d
    kernel = functools.partial(
        _mixer_in_kernel, n_load=n_load, tiles_per_seq=seq_len // tm,
        q_scale=HEAD_DIM ** -0.5 * math.log2(math.e), c_scale=math.log2(math.e))
    streamed = ["w_in", "w_out_conv", "w_out_sgu"]
    small = ["b_forget", "b_gate", "conv_w", "sgu_ln_g", "sgu_ln_b", "sgu_w", "sgu_b_t", "q_norm_g", "k_norm_g"]
    weight_specs = ([_layer(w["mix_norm"], layer), _weight_chunk(w["w_in"], layer, n_load, layer_rows=d)]
                    + [_weight_chunk(w[n], layer, n_load) for n in streamed[1:]]
                    + [_layer(w[n], layer) for n in small])
    weight_args = [w["mix_norm"]] + [w[n] for n in streamed] + [w[n] for n in small]
    v_rows = heads * (HEAD_DIM + ONES_ROWS)
    tile = _token_tile((tm, d), n_load)
    resident = [pltpu.VMEM((d, 8 * d), BF16), pltpu.VMEM((d, V7X_LANES), BF16), pltpu.VMEM((d, N_BRANCH * d), BF16),
                pltpu.VMEM((d, d), BF16), pltpu.VMEM((d, d), BF16)]
    scratch = resident + [pltpu.VMEM((tm + V7X_SUBLANES, d), F32), pltpu.VMEM((V7X_SUBLANES, V7X_LANES), F32),
                          pltpu.VMEM((tm, d), BF16), pltpu.VMEM((tm, d), BF16)]
    vmem = (sum(_nbytes(s.shape, s.dtype) for s in scratch)
            + 2 * _nbytes((d, n_in + 2 * d), F32) // n_load
            + sum(_nbytes(w[n].shape[1:], w[n].dtype) for n in small)
            + 2 * (3 * _nbytes((tm, d), F32) + 4 * _nbytes((tm, d), BF16) + _nbytes((v_rows, tm), BF16)
                   + _nbytes((V7X_SUBLANES, tm), F32))
            + 16 * _nbytes((tm, d), F32))
    return pl.pallas_call(
        kernel,
        grid=(n_load + t // tm,),
        in_specs=[tile] + weight_specs,
        out_specs=[tile, tile,
                   _token_tile((None, 2 * d, tm), n_load),
                   _token_tile((tm, 2 * d), n_load),
                   _token_tile((None, v_rows, tm), n_load),
                   _token_tile((None, heads, tm), n_load)],
        out_shape=[jax.ShapeDtypeStruct((t, d), F32), jax.ShapeDtypeStruct((t, d), F32),
                   jax.ShapeDtypeStruct((t // tm, 2 * d, tm), BF16), jax.ShapeDtypeStruct((t, 2 * d), BF16),
                   jax.ShapeDtypeStruct((t // tm, v_rows, tm), BF16),
                   jax.ShapeDtypeStruct((t // tm, heads, tm), F32)],
        scratch_shapes=scratch,
        compiler_params=_params(("arbitrary",), vmem),
        name="mixer_in",
    )(x, *weight_args)


def _attn_kernel(qaug_ref, kaug_ref, vaug_ref, crow_ref, o_ref, s0_ref, s1_ref, m_ref, acc_ref):
    tq = qaug_ref.shape[1]
    heads = m_ref.shape[0]
    i = pl.program_id(1)
    head_cols = [slice(h * HEAD_DIM, (h + 1) * HEAD_DIM) for h in range(heads)]
    qk_cols = [slice(h * 2 * HEAD_DIM, (h + 1) * 2 * HEAD_DIM) for h in range(heads)]
    v_rows = [slice(h * (HEAD_DIM + ONES_ROWS), (h + 1) * (HEAD_DIM + ONES_ROWS)) for h in range(heads)]
    key_pos = lax.broadcasted_iota(jnp.int32, (tq, tq), 0)
    query_pos = lax.broadcasted_iota(jnp.int32, (tq, tq), 1)
    causal = key_pos <= query_pos

    m_ref[...] = jnp.full(m_ref.shape, MASKED, F32)
    acc_ref[...] = jnp.zeros(acc_ref.shape, F32)

    s_refs = (s0_ref, s1_ref)

    def logits(j, slot, diagonal):
        rows = pl.ds(pl.multiple_of(j * tq, tq), tq)
        for h in range(heads):
            s = jnp.dot(kaug_ref[rows, qk_cols[h]], qaug_ref[qk_cols[h], :], preferred_element_type=F32)
            s_refs[slot][h] = jnp.where(causal, s, MASKED) if diagonal else s

    def softmax_pv(j, slot):
        probs, rescale = [], []
        for h in range(heads):
            s = s_refs[slot][h]
            cq = crow_ref[i, h:h + 1, :]
            m_old = m_ref[h]
            block_max = jnp.max(functools.reduce(jnp.maximum, _row_groups(s)), axis=0, keepdims=True)
            m_new = jnp.maximum(m_old, block_max + cq)
            m_ref[h] = m_new
            probs.append(jnp.exp2(s + (cq - m_new)).astype(BF16))
            rescale.append(jnp.exp2(m_old - m_new))
        for h in range(heads):
            acc_ref[h] = rescale[h] * acc_ref[h] + jnp.dot(vaug_ref[j, v_rows[h], :], probs[h],
                                                           preferred_element_type=F32)

    @pl.when(i == 0)
    def _():
        logits(0, 0, True)
        softmax_pv(0, 0)

    def stage(j_next, slot_next, diagonal, j, slot):
        logits(j_next, slot_next, diagonal)
        softmax_pv(j, slot)

    n_pairs = lax.shift_right_logical(i - 1, 1)
    j0 = 2 * n_pairs

    @pl.when(i > 0)
    def _():
        logits(0, 0, False)

        def pair(t, carry):
            j = 2 * t
            stage(j + 1, 1, False, j, 0)
            stage(j + 2, 0, False, j + 1, 1)
            return carry

        lax.fori_loop(0, n_pairs, pair, 0)

    @pl.when(jnp.logical_and(i > 0, i - j0 == 1))
    def _():
        stage(i, 1, True, j0, 0)
        softmax_pv(i, 1)

    @pl.when(jnp.logical_and(i > 0, i - j0 == 2))
    def _():
        stage(j0 + 1, 1, False, j0, 0)
        stage(i, 0, True, j0 + 1, 1)
        softmax_pv(i, 0)

    for h, hs in enumerate(head_cols):
        row_sum = acc_ref[h, HEAD_DIM:HEAD_DIM + 1, :]
        o_ref[:, hs] = (acc_ref[h, 0:HEAD_DIM, :] * (1.0 / row_sum)).T.astype(BF16)


def _attention(qaug, kaug, vaug, c_rows, *, batch, seq_len):
    nt, qk_rows, tq = qaug.shape
    nq = seq_len // tq
    heads = qk_rows // (2 * HEAD_DIM)
    d = heads * HEAD_DIM
    v_rows = vaug.shape[1]
    assert nt == batch * nq and tq % V7X_LANES == 0 and v_rows == heads * (HEAD_DIM + ONES_ROWS)
    scratch = [pltpu.VMEM((heads, tq, tq), F32), pltpu.VMEM((heads, tq, tq), F32),
               pltpu.VMEM((heads, 1, tq), F32),
               pltpu.VMEM((heads, HEAD_DIM + ONES_ROWS, tq), F32)]
    vmem = (2 * (_nbytes((seq_len, qk_rows), BF16) + _nbytes((nq, v_rows, tq), BF16)
                 + _nbytes((qk_rows, tq), BF16) + _nbytes((tq, d), BF16)
                 + _nbytes((nq, V7X_SUBLANES, tq), F32))
            + sum(_nbytes(s.shape, s.dtype) for s in scratch)
            + _nbytes((V7X_SUBLANES * heads, tq), F32)
            + 16 * _nbytes((tq, tq), F32))
    return pl.pallas_call(
        _attn_kernel,
        grid=(batch, nq),
        in_specs=[pl.BlockSpec((None, qk_rows, tq), lambda b, i: (b * nq + i, 0, 0)),
                  pl.BlockSpec((seq_len, qk_rows), lambda b, i: (b, 0)),
                  pl.BlockSpec((nq, v_rows, tq), lambda b, i: (b, 0, 0)),
                  pl.BlockSpec((nq, heads, tq), lambda b, i: (b, 0, 0))],
        out_specs=pl.BlockSpec((tq, d), lambda b, i: (b * nq + i, 0)),
        out_shape=jax.ShapeDtypeStruct((nt * tq, d), BF16),
        scratch_shapes=scratch,
        compiler_params=_params(("arbitrary", "arbitrary"), vmem),
        name="attention",
    )(qaug, kaug, vaug, c_rows)


def _mixer_out_kernel(x_ref, a_ref, part_ref, g2_ref, woa32_ref, wo32_ref, g_ref, wgu32_ref, wd32_ref,
                      o_ref, woa_ref, wo_ref, wgu_ref, wd_ref, act_ref, *, n_load):
    step = pl.program_id(0)

    @pl.when(step < n_load)
    def _():
        _stash_rows(woa32_ref, woa_ref, step)
        _stash_rows(wo32_ref, wo_ref, step)
        _stash_rows(wgu32_ref, wgu_ref, step)
        _stash_rows(wd32_ref, wd_ref, step)

    @pl.when(step >= n_load)
    def _():
        yc = jnp.dot(a_ref[...], woa_ref[...], preferred_element_type=F32)
        merged = part_ref[...] + g2_ref[...] * yc
        x = x_ref[...] + jnp.dot(merged.astype(BF16), wo_ref[...], preferred_element_type=F32)
        o_ref[...] = _swiglu_half_step(x, g_ref, wgu_ref, wd_ref, act_ref)


def _mixer_out(x, attn, part, g2, w_mix, w_ffn, layer):
    t, d = x.shape
    d_ff = w_ffn["w_down"].shape[1]
    tm = TOKEN_TILE_FFN
    n_load = LOAD_STEPS
    assert t % tm == 0
    tile = _token_tile((tm, d), n_load)
    vmem = (_ffn_vmem(tm, d, d_ff, n_load) + 2 * _nbytes((d, d), BF16) + 4 * _nbytes((d, d), F32) // n_load
            + 2 * (4 * _nbytes((tm, d), F32) + _nbytes((tm, d), BF16)) + 2 * _nbytes((tm, d), F32))
    return pl.pallas_call(
        functools.partial(_mixer_out_kernel, n_load=n_load),
        grid=(n_load + t // tm,),
        in_specs=[tile, tile, tile, tile,
                  _weight_chunk(w_mix["w_out_attn"], layer, n_load), _weight_chunk(w_mix["w_o"], layer, n_load)]
        + _ffn_weight_specs(w_ffn, layer, n_load),
        out_specs=tile,
        out_shape=jax.ShapeDtypeStruct((t, d), F32),
        scratch_shapes=[pltpu.VMEM((d, d), BF16), pltpu.VMEM((d, d), BF16)] + _ffn_scratch(tm, w_ffn),
        compiler_params=_params(("arbitrary",), vmem),
        name="mixer_out",
    )(x, attn, part, g2, w_mix["w_out_attn"], w_mix["w_o"], *_ffn_weight_args(w_ffn))


def kernel(x, ffn1_norm, ffn1_w_gu, ffn1_w_down, mix_norm, w_in, b_forget, b_gate, conv_w, sgu_ln_g, sgu_ln_b,
           sgu_w, sgu_b, q_norm_g, k_norm_g, w_out_conv, w_out_sgu, w_out_attn, w_o, ffn2_norm, ffn2_w_gu,
           ffn2_w_down):
    batch, seq_len, d = x.shape
    depth = w_in.shape[0]
    heads = d // HEAD_DIM
    rows = lambda a: a.reshape(depth, 1, -1)

    ffn1 = {"norm": rows(ffn1_norm), "w_gu": ffn1_w_gu, "w_down": ffn1_w_down}
    ffn2 = {"norm": rows(ffn2_norm), "w_gu": ffn2_w_gu, "w_down": ffn2_w_down}
    mix = {
        "mix_norm": rows(mix_norm),
        "w_in": w_in.reshape(depth * d, -1),
        "b_forget": jnp.pad(rows(b_forget), ((0, 0), (0, 0), (0, V7X_LANES - heads))),
        "b_gate": rows(b_gate),
        "conv_w": conv_w,
        "sgu_ln_g": rows(sgu_ln_g),
        "sgu_ln_b": rows(sgu_ln_b),
        "sgu_w": sgu_w,
        "sgu_b_t": jnp.swapaxes(sgu_b, 1, 2),
        "q_norm_g": rows(q_norm_g),
        "k_norm_g": rows(k_norm_g),
        "w_out_conv": w_out_conv,
        "w_out_sgu": w_out_sgu,
        "w_out_attn": w_out_attn,
        "w_o": w_o,
    }

    xt = x.reshape(batch * seq_len, d)
    for layer in range(depth):
        xt = _ffn(xt, ffn1, layer)
        part, g2, qaug, kaug, vaug, c_rows = _mixer_in(xt, mix, layer, seq_len=seq_len)
        attn = _attention(qaug, kaug, vaug, c_rows, batch=batch, seq_len=seq_len)
        xt = _mixer_out(xt, attn, part, g2, mix, ffn2, layer)
    return xt.reshape(batch, seq_len, d)
```

```python
import functools
import math

import jax
import jax.numpy as jnp
from jax import lax
from jax.experimental import pallas as pl
from jax.experimental.pallas import tpu as pltpu

F32 = jnp.float32
BF16 = jnp.bfloat16

RMS_EPS = 1e-6
LN_EPS = 1e-5
SGU_CHUNK = 128
HEAD_DIM = 128
N_BRANCH = 3
MASKED = -1e30
FORGET_SPLIT = 3
ONES_ROWS = 16

V7X_LANES = 128
V7X_SUBLANES = 8
BF16_SUBLANES = 16
V7X_VMEM_BYTES = 64 * 1024 * 1024
LOAD_STEPS = 16
FF_CHUNK = 1024
GATE_CHUNK = 256

TOKEN_TILE_FFN = 512
SEQ_BLOCK = 256


def _layer(arr, layer, cols=None, col_block=0):
    block = (None,) + tuple(arr.shape[1:-1]) + (arr.shape[-1] if cols is None else cols,)
    index = (layer,) + (0,) * (arr.ndim - 2) + (col_block,)
    return pl.BlockSpec(block, lambda *_: index, pipeline_mode=pl.Buffered(1))


def _weight_chunk(arr, layer, n_load, layer_rows=None):
    rows = (arr.shape[1] if layer_rows is None else layer_rows) // n_load
    assert rows % BF16_SUBLANES == 0, (arr.shape, n_load)
    if layer_rows is None:
        assert rows * n_load == arr.shape[1]
        return pl.BlockSpec((None, rows, arr.shape[2]), lambda s: (layer, jnp.minimum(s, n_load - 1), 0))
    assert rows * n_load == layer_rows and arr.shape[0] % layer_rows == 0
    return pl.BlockSpec((rows, arr.shape[1]), lambda s: (layer * n_load + jnp.minimum(s, n_load - 1), 0))


def _token_tile(block, n_load):
    return pl.BlockSpec(block, lambda s: (jnp.maximum(s - n_load, 0),) + (0,) * (len(block) - 1))


def _stash_rows(src_ref, dst_ref, step, cols=None):
    rows = src_ref.shape[0]
    chunk = src_ref[...] if cols is None else src_ref[:, cols]
    dst_ref[pl.ds(pl.multiple_of(step * rows, rows), rows), :] = chunk.astype(BF16)


def _nbytes(shape, dtype):
    return math.prod(shape) * jnp.dtype(dtype).itemsize


def _params(semantics, vmem_bytes):
    assert vmem_bytes <= V7X_VMEM_BYTES, vmem_bytes
    return pltpu.CompilerParams(dimension_semantics=semantics, vmem_limit_bytes=int(vmem_bytes))


def _rms_norm(x, g):
    return x * lax.rsqrt(jnp.mean(x * x, axis=-1, keepdims=True) + RMS_EPS) * g


def _gelu(x):
    return 0.5 * x * (1.0 + lax.erf(x * (2.0 ** -0.5)))


def _ff_chunks(d_ff):
    return [(c, min(c + FF_CHUNK, d_ff)) for c in range(0, d_ff, FF_CHUNK)]


def _row_groups(x):
    return [x[r:r + V7X_SUBLANES] for r in range(0, x.shape[0], V7X_SUBLANES)]


def _swiglu_half_step(x, g_ref, wgu_ref, wd_ref, act_ref):
    d_ff = wd_ref.shape[0]
    h = _rms_norm(x, g_ref[...]).astype(BF16)
    for c0, c1 in _ff_chunks(d_ff):
        g = jnp.dot(h, wgu_ref[:, c0:c1], preferred_element_type=F32)
        u = jnp.dot(h, wgu_ref[:, d_ff + c0:d_ff + c1], preferred_element_type=F32)
        act_ref[:, c0:c1] = (g * jax.nn.sigmoid(g) * u).astype(BF16)
    return x + 0.5 * jnp.dot(act_ref[...], wd_ref[...], preferred_element_type=F32)


def _ffn_weight_specs(w, layer, n_load):
    return [_layer(w["norm"], layer), _weight_chunk(w["w_gu"], layer, n_load),
            _weight_chunk(w["w_down"], layer, n_load)]


def _ffn_weight_args(w):
    return [w["norm"], w["w_gu"], w["w_down"]]


def _ffn_scratch(tm, w):
    d, d_gu = w["w_gu"].shape[1:]
    d_ff = w["w_down"].shape[1]
    return [pltpu.VMEM((d, d_gu), BF16), pltpu.VMEM((d_ff, d), BF16), pltpu.VMEM((tm, d_ff), BF16)]


def _ffn_vmem(tm, d, d_ff, n_load):
    return (_nbytes((d, 2 * d_ff), BF16) + _nbytes((d_ff, d), BF16)
            + 2 * (_nbytes((d, 2 * d_ff), F32) + _nbytes((d_ff, d), F32)) // n_load
            + _nbytes((tm, d_ff), BF16)
            + 4 * _nbytes((tm, FF_CHUNK), F32))


def _ffn_kernel(x_ref, g_ref, wgu32_ref, wd32_ref, o_ref, wgu_ref, wd_ref, act_ref, *, n_load):
    step = pl.program_id(0)

    @pl.when(step < n_load)
    def _():
        _stash_rows(wgu32_ref, wgu_ref, step)
        _stash_rows(wd32_ref, wd_ref, step)

    @pl.when(step >= n_load)
    def _():
        o_ref[...] = _swiglu_half_step(x_ref[...], g_ref, wgu_ref, wd_ref, act_ref)


def _ffn(x, w, layer):
    t, d = x.shape
    d_ff = w["w_down"].shape[1]
    tm = TOKEN_TILE_FFN
    n_load = LOAD_STEPS
    assert t % tm == 0
    tile = _token_tile((tm, d), n_load)
    vmem = _ffn_vmem(tm, d, d_ff, n_load) + 4 * _nbytes((tm, d), F32)
    return pl.pallas_call(
        functools.partial(_ffn_kernel, n_load=n_load),
        grid=(n_load + t // tm,),
        in_specs=[tile] + _ffn_weight_specs(w, layer, n_load),
        out_specs=tile,
        out_shape=jax.ShapeDtypeStruct((t, d), F32),
        scratch_shapes=_ffn_scratch(tm, w),
        compiler_params=_params(("arbitrary",), vmem),
        name="ffn",
    )(x, *_ffn_weight_args(w))


def _mixer_in_kernel(x_ref, ng_ref, wtm32_ref, wtf32_ref, wtg32_ref, woc32_ref, wos32_ref, bf_ref, bg_ref, cw_ref,
                     lng_ref, lnb_ref, sw_ref, sb_ref, qg_ref, kg_ref,
                     part_ref, g2_ref, qaug_ref, kaug_ref, vaug_ref, crow_ref,
                     wm_ref, wf_ref, wgt_ref, woc_ref, wos_ref, zs_ref, ccarry_ref, vn_ref, yb_ref, *,
                     n_load, tiles_per_seq, q_scale, c_scale):
    step = pl.program_id(0)

    @pl.when(step < n_load)
    def _():
        wm_ref[step] = wtm32_ref[...].T.astype(BF16)
        _stash_rows(woc32_ref, woc_ref, step)
        _stash_rows(wos32_ref, wos_ref, step)

    @pl.when(step < wgt_ref.shape[0])
    def _():
        wgt_ref[step] = wtg32_ref[...].T.astype(BF16)

    @pl.when(step == 0)
    def _():
        rows = wtf32_ref[...]
        slab = jnp.concatenate([rows, jnp.zeros((V7X_LANES - rows.shape[0], rows.shape[1]), F32)], axis=0)
        wf_ref[...] = slab.T.astype(BF16)

    @pl.when(step >= n_load)
    def _():
        _mixer_in_tile(step - n_load, x_ref, ng_ref, wm_ref, wf_ref, wgt_ref, bf_ref, bg_ref, cw_ref, lng_ref,
                       lnb_ref, sw_ref, sb_ref, qg_ref, kg_ref, woc_ref, wos_ref,
                       part_ref, g2_ref, qaug_ref, kaug_ref, vaug_ref, crow_ref,
                       zs_ref, ccarry_ref, vn_ref, yb_ref,
                       tiles_per_seq=tiles_per_seq, q_scale=q_scale, c_scale=c_scale)


def _mixer_in_tile(tile_idx, x_ref, ng_ref, wm_ref, wf_ref, wgt_ref, bf_ref, bg_ref, cw_ref, lng_ref, lnb_ref,
                   sw_ref, sb_ref, qg_ref, kg_ref, woc_ref, wos_ref,
                   part_ref, g2_ref, qaug_ref, kaug_ref, vaug_ref, crow_ref,
                   zs_ref, ccarry_ref, vn_ref, yb_ref, *, tiles_per_seq, q_scale, c_scale):
    tm, d = x_ref.shape
    heads = d // HEAD_DIM
    pad = V7X_SUBLANES

    @pl.when(tile_idx % tiles_per_seq == 0)
    def _():
        zs_ref[0:pad, :] = jnp.zeros((pad, d), F32)
        ccarry_ref[...] = jnp.zeros_like(ccarry_ref)

    h = _rms_norm(x_ref[...], ng_ref[...]).astype(BF16)

    def proj(w_ref, c0, c1):
        width = w_ref.shape[2]
        assert c0 % width == 0 and c1 % width == 0
        return jnp.concatenate([jnp.dot(h, w_ref[j], preferred_element_type=F32)
                                for j in range(c0 // width, c1 // width)], axis=1)

    pa = proj(wm_ref, 0, 3 * d)
    f = jnp.dot(h, wf_ref[...], preferred_element_type=F32) + bf_ref[...]
    ps = proj(wm_ref, 3 * d, 5 * d)
    pq = proj(wm_ref, 5 * d, 8 * d)
    gates = [jax.nn.sigmoid(proj(wgt_ref, b * d, (b + 1) * d) + bg_ref[:, b * d:(b + 1) * d])
             for b in range(N_BRANCH)]

    zs_ref[pad:pad + tm, :] = pa[:, d:2 * d] * pa[:, 2 * d:3 * d]
    conv = (cw_ref[0:1, :] * zs_ref[pad - 2:pad - 2 + tm, :]
            + cw_ref[1:2, :] * zs_ref[pad - 1:pad - 1 + tm, :]
            + cw_ref[2:3, :] * zs_ref[pad:pad + tm, :])
    ya_in = (pa[:, 0:d] * conv).astype(BF16)
    zs_ref[0:pad, :] = zs_ref[tm:tm + pad, :]

    c = jnp.minimum(f, 0.0) - jnp.log1p(jnp.exp(-jnp.abs(f)))
    t_idx = lax.broadcasted_iota(jnp.int32, c.shape, 0)
    shift = 1
    while shift < tm:
        c = c + jnp.where(t_idx >= shift, pltpu.roll(c, shift, axis=0), 0.0)
        shift *= 2
    c = c + ccarry_ref[0:1, :]
    ccarry_ref[0:1, :] = c[tm - 1:tm, :]
    c = c * c_scale
    crow_ref[...] = c.T[0:heads, :]

    u = _gelu(ps[:, 0:d])
    vv = _gelu(ps[:, d:2 * d])
    mu = jnp.mean(vv, axis=-1, keepdims=True)
    vc = vv - mu
    var = jnp.mean(vc * vc, axis=-1, keepdims=True)
    vn_ref[...] = (vc * lax.rsqrt(var + LN_EPS) * lng_ref[...] + lnb_ref[...]).astype(BF16)

    lane = lax.broadcasted_iota(jnp.int32, (tm, HEAD_DIM), 1)
    feature = lax.broadcasted_iota(jnp.int32, (HEAD_DIM, tm), 0)
    ones_rows = jnp.where(feature < FORGET_SPLIT, 1.0, 0.0).astype(BF16)
    qn = []
    for g in range(heads):
        hs = slice(g * HEAD_DIM, (g + 1) * HEAD_DIM)
        qn.append(_rms_norm(pq[:, hs], qg_ref[:, hs]) * q_scale)
        ks = slice(d + g * HEAD_DIM, d + (g + 1) * HEAD_DIM)
        kaug_ref[:, 2 * g * HEAD_DIM:(2 * g + 1) * HEAD_DIM] = _rms_norm(pq[:, ks], kg_ref[:, hs]).astype(BF16)
        rest = -jnp.broadcast_to(c[:, g:g + 1], (tm, HEAD_DIM))
        slab = jnp.zeros((tm, HEAD_DIM), F32)
        for term in range(FORGET_SPLIT):
            piece = rest.astype(BF16).astype(F32)
            slab = jnp.where(lane == term, piece, slab)
            rest = rest - piece
        kaug_ref[:, (2 * g + 1) * HEAD_DIM:(2 * g + 2) * HEAD_DIM] = slab.astype(BF16)
    qt = jnp.concatenate(qn, axis=1).T.astype(BF16)
    vt = pq[:, 2 * d:3 * d].T.astype(BF16)
    v_rows = HEAD_DIM + ONES_ROWS
    for g in range(heads):
        hs = slice(g * HEAD_DIM, (g + 1) * HEAD_DIM)
        qaug_ref[2 * g * HEAD_DIM:(2 * g + 1) * HEAD_DIM, :] = qt[hs, :]
        qaug_ref[(2 * g + 1) * HEAD_DIM:(2 * g + 2) * HEAD_DIM, :] = ones_rows
        vaug_ref[g * v_rows:g * v_rows + HEAD_DIM, :] = vt[hs, :]
        vaug_ref[g * v_rows + HEAD_DIM:(g + 1) * v_rows, :] = jnp.ones((ONES_ROWS, tm), BF16)

    ya = jnp.dot(ya_in, woc_ref[...], preferred_element_type=F32)
    n_chunks = tm // SGU_CHUNK
    pos_t = lax.broadcasted_iota(jnp.int32, (SGU_CHUNK, SGU_CHUNK), 0)
    pos_s = lax.broadcasted_iota(jnp.int32, (SGU_CHUNK, SGU_CHUNK), 1)
    for g in range(heads):
        hs = slice(g * HEAD_DIM, (g + 1) * HEAD_DIM)
        w = jnp.where(pos_s <= pos_t, sw_ref[g], 0.0).astype(BF16)
        rhs = jnp.concatenate([vn_ref[c0 * SGU_CHUNK:(c0 + 1) * SGU_CHUNK, hs] for c0 in range(n_chunks)], axis=1)
        s = jnp.dot(w, rhs, preferred_element_type=F32) + sb_ref[:, g:g + 1]
        for c0 in range(n_chunks):
            rows = slice(c0 * SGU_CHUNK, (c0 + 1) * SGU_CHUNK)
            yb_ref[rows, hs] = (u[rows, hs] * s[:, c0 * SGU_CHUNK:(c0 + 1) * SGU_CHUNK]).astype(BF16)
    yb = jnp.dot(yb_ref[...], wos_ref[...], preferred_element_type=F32)
    part_ref[...] = gates[0] * ya + gates[1] * yb
    g2_ref[...] = gates[2]


def _mixer_in(x, w, layer, *, seq_len):
    t, d = x.shape
    tm = SEQ_BLOCK
    heads = d // HEAD_DIM
    assert seq_len % tm == 0 and tm % SGU_CHUNK == 0 and d % HEAD_DIM == 0 and heads <= V7X_SUBLANES
    n_load = LOAD_STEPS
    w_t = w["w_in_t"]
    n_in = w_t.shape[1]
    n_main = 8 * d
    main_rows = n_main // n_load
    n_gate = N_BRANCH * d // GATE_CHUNK
    assert n_in == n_main + heads + N_BRANCH * d and main_rows * n_load == n_main and main_rows % V7X_LANES == 0
    assert n_gate * GATE_CHUNK == N_BRANCH * d and n_gate <= n_load and d % GATE_CHUNK == 0
    kernel = functools.partial(
        _mixer_in_kernel, n_load=n_load, tiles_per_seq=seq_len // tm,
        q_scale=HEAD_DIM ** -0.5 * math.log2(math.e), c_scale=math.log2(math.e))
    main_spec = pl.BlockSpec((None, main_rows, d), lambda s: (layer, jnp.minimum(s, n_load - 1), 0))
    forget_spec = pl.BlockSpec((pl.Squeezed(), pl.Element(V7X_SUBLANES), pl.Element(d)),
                               lambda s: (layer, n_main, 0))
    gate_align = math.gcd(n_main + heads, GATE_CHUNK, V7X_SUBLANES)
    gate_spec = pl.BlockSpec(
        (pl.Squeezed(), pl.Element(GATE_CHUNK), pl.Element(d)),
        lambda s: (layer, pl.multiple_of(n_main + heads + GATE_CHUNK * jnp.minimum(s, n_gate - 1), gate_align), 0))
    streamed = ["w_out_conv", "w_out_sgu"]
    small = ["b_forget", "b_gate", "conv_w", "sgu_ln_g", "sgu_ln_b", "sgu_w", "sgu_b_t", "q_norm_g", "k_norm_g"]
    weight_specs = ([_layer(w["mix_norm"], layer), main_spec, forget_spec, gate_spec]
                    + [_weight_chunk(w[n], layer, n_load) for n in streamed] + [_layer(w[n], layer) for n in small])
    weight_args = [w["mix_norm"], w_t, w_t, w_t] + [w[n] for n in streamed] + [w[n] for n in small]
    v_rows = heads * (HEAD_DIM + ONES_ROWS)
    tile = _token_tile((tm, d), n_load)
    resident = [pltpu.VMEM((n_load, d, main_rows), BF16), pltpu.VMEM((d, V7X_LANES), BF16),
                pltpu.VMEM((n_gate, d, GATE_CHUNK), BF16), pltpu.VMEM((d, d), BF16), pltpu.VMEM((d, d), BF16)]
    scratch = resident + [pltpu.VMEM((tm + V7X_SUBLANES, d), F32), pltpu.VMEM((V7X_SUBLANES, V7X_LANES), F32),
                          pltpu.VMEM((tm, d), BF16), pltpu.VMEM((tm, d), BF16)]
    vmem = (sum(_nbytes(s.shape, s.dtype) for s in scratch)
            + 2 * _nbytes((main_rows + GATE_CHUNK + V7X_SUBLANES + 2 * d // n_load, d), F32)
            + 2 * _nbytes((main_rows, d), F32)
            + sum(_nbytes(w[n].shape[1:], w[n].dtype) for n in small)
            + 2 * (3 * _nbytes((tm, d), F32) + 4 * _nbytes((tm, d), BF16) + _nbytes((v_rows, tm), BF16)
                   + _nbytes((V7X_SUBLANES, tm), F32))
            + 10 * _nbytes((tm, d), F32))
    return pl.pallas_call(
        kernel,
        grid=(n_load + t // tm,),
        in_specs=[tile] + weight_specs,
        out_specs=[tile, tile,
                   _token_tile((None, 2 * d, tm), n_load),
                   _token_tile((tm, 2 * d), n_load),
                   _token_tile((None, v_rows, tm), n_load),
                   _token_tile((None, heads, tm), n_load)],
        out_shape=[jax.ShapeDtypeStruct((t, d), F32), jax.ShapeDtypeStruct((t, d), F32),
                   jax.ShapeDtypeStruct((t // tm, 2 * d, tm), BF16), jax.ShapeDtypeStruct((t, 2 * d), BF16),
                   jax.ShapeDtypeStruct((t // tm, v_rows, tm), BF16),
                   jax.ShapeDtypeStruct((t // tm, heads, tm), F32)],
        scratch_shapes=scratch,
        compiler_params=_params(("arbitrary",), vmem),
        name="mixer_in",
    )(x, *weight_args)


def _attn_kernel(qaug_ref, kaug_ref, vaug_ref, crow_ref, o_ref, s0_ref, s1_ref, m_ref, acc_ref):
    tq = qaug_ref.shape[1]
    heads = m_ref.shape[0]
    i = pl.program_id(1)
    head_cols = [slice(h * HEAD_DIM, (h + 1) * HEAD_DIM) for h in range(heads)]
    qk_cols = [slice(h * 2 * HEAD_DIM, (h + 1) * 2 * HEAD_DIM) for h in range(heads)]
    v_rows = [slice(h * (HEAD_DIM + ONES_ROWS), (h + 1) * (HEAD_DIM + ONES_ROWS)) for h in range(heads)]
    key_pos = lax.broadcasted_iota(jnp.int32, (tq, tq), 0)
    query_pos = lax.broadcasted_iota(jnp.int32, (tq, tq), 1)
    causal = key_pos <= query_pos

    m_ref[...] = jnp.full(m_ref.shape, MASKED, F32)
    acc_ref[...] = jnp.zeros(acc_ref.shape, F32)

    s_refs = (s0_ref, s1_ref)

    def logits(j, slot, diagonal):
        rows = pl.ds(pl.multiple_of(j * tq, tq), tq)
        for h in range(heads):
            s = jnp.dot(kaug_ref[rows, qk_cols[h]], qaug_ref[qk_cols[h], :], preferred_element_type=F32)
            s_refs[slot][h] = jnp.where(causal, s, MASKED) if diagonal else s

    def softmax_pv(j, slot):
        probs, rescale = [], []
        for h in range(heads):
            s = s_refs[slot][h]
            cq = crow_ref[i, h:h + 1, :]
            m_old = m_ref[h]
            block_max = jnp.max(functools.reduce(jnp.maximum, _row_groups(s)), axis=0, keepdims=True)
            m_new = jnp.maximum(m_old, block_max + cq)
            m_ref[h] = m_new
            probs.append(jnp.exp2(s + (cq - m_new)).astype(BF16))
            rescale.append(jnp.exp2(m_old - m_new))
        for h in range(heads):
            acc_ref[h] = rescale[h] * acc_ref[h] + jnp.dot(vaug_ref[j, v_rows[h], :], probs[h],
                                                           preferred_element_type=F32)

    @pl.when(i == 0)
    def _():
        logits(0, 0, True)
        softmax_pv(0, 0)

    def stage(j_next, slot_next, diagonal, j, slot):
        logits(j_next, slot_next, diagonal)
        softmax_pv(j, slot)

    n_pairs = lax.shift_right_logical(i - 1, 1)
    j0 = 2 * n_pairs

    @pl.when(i > 0)
    def _():
        logits(0, 0, False)

        def pair(t, carry):
            j = 2 * t
            stage(j + 1, 1, False, j, 0)
            stage(j + 2, 0, False, j + 1, 1)
            return carry

        lax.fori_loop(0, n_pairs, pair, 0)

    @pl.when(jnp.logical_and(i > 0, i - j0 == 1))
    def _():
        stage(i, 1, True, j0, 0)
        softmax_pv(i, 1)

    @pl.when(jnp.logical_and(i > 0, i - j0 == 2))
    def _():
        stage(j0 + 1, 1, False, j0, 0)
        stage(i, 0, True, j0 + 1, 1)
        softmax_pv(i, 0)

    for h, hs in enumerate(head_cols):
        row_sum = acc_ref[h, HEAD_DIM:HEAD_DIM + 1, :]
        o_ref[:, hs] = (acc_ref[h, 0:HEAD_DIM, :] * (1.0 / row_sum)).T.astype(BF16)


def _attention(qaug, kaug, vaug, c_rows, *, batch, seq_len):
    nt, qk_rows, tq = qaug.shape
    nq = seq_len // tq
    heads = qk_rows // (2 * HEAD_DIM)
    d = heads * HEAD_DIM
    v_rows = vaug.shape[1]
    assert nt == batch * nq and tq % V7X_LANES == 0 and v_rows == heads * (HEAD_DIM + ONES_ROWS)
    scratch = [pltpu.VMEM((heads, tq, tq), F32), pltpu.VMEM((heads, tq, tq), F32),
               pltpu.VMEM((heads, 1, tq), F32),
               pltpu.VMEM((heads, HEAD_DIM + ONES_ROWS, tq), F32)]
    vmem = (2 * (_nbytes((seq_len, qk_rows), BF16) + _nbytes((nq, v_rows, tq), BF16)
                 + _nbytes((qk_rows, tq), BF16) + _nbytes((tq, d), BF16)
                 + _nbytes((nq, V7X_SUBLANES, tq), F32))
            + sum(_nbytes(s.shape, s.dtype) for s in scratch)
            + _nbytes((V7X_SUBLANES * heads, tq), F32)
            + 16 * _nbytes((tq, tq), F32))
    return pl.pallas_call(
        _attn_kernel,
        grid=(batch, nq),
        in_specs=[pl.BlockSpec((None, qk_rows, tq), lambda b, i: (b * nq + i, 0, 0)),
                  pl.BlockSpec((seq_len, qk_rows), lambda b, i: (b, 0)),
                  pl.BlockSpec((nq, v_rows, tq), lambda b, i: (b, 0, 0)),
                  pl.BlockSpec((nq, heads, tq), lambda b, i: (b, 0, 0))],
        out_specs=pl.BlockSpec((tq, d), lambda b, i: (b * nq + i, 0)),
        out_shape=jax.ShapeDtypeStruct((nt * tq, d), BF16),
        scratch_shapes=scratch,
        compiler_params=_params(("arbitrary", "arbitrary"), vmem),
        name="attention",
    )(qaug, kaug, vaug, c_rows)


def _mixer_out_kernel(x_ref, a_ref, part_ref, g2_ref, woa32_ref, wo32_ref, g_ref, wgu32_ref, wd32_ref,
                      o_ref, woa_ref, wo_ref, wgu_ref, wd_ref, act_ref, *, n_load):
    step = pl.program_id(0)

    @pl.when(step < n_load)
    def _():
        _stash_rows(woa32_ref, woa_ref, step)
        _stash_rows(wo32_ref, wo_ref, step)
        _stash_rows(wgu32_ref, wgu_ref, step)
        _stash_rows(wd32_ref, wd_ref, step)

    @pl.when(step >= n_load)
    def _():
        yc = jnp.dot(a_ref[...], woa_ref[...], preferred_element_type=F32)
        merged = part_ref[...] + g2_ref[...] * yc
        x = x_ref[...] + jnp.dot(merged.astype(BF16), wo_ref[...], preferred_element_type=F32)
        o_ref[...] = _swiglu_half_step(x, g_ref, wgu_ref, wd_ref, act_ref)


def _mixer_out(x, attn, part, g2, w_mix, w_ffn, layer):
    t, d = x.shape
    d_ff = w_ffn["w_down"].shape[1]
    tm = TOKEN_TILE_FFN
    n_load = LOAD_STEPS
    assert t % tm == 0
    tile = _token_tile((tm, d), n_load)
    vmem = (_ffn_vmem(tm, d, d_ff, n_load) + 2 * _nbytes((d, d), BF16) + 4 * _nbytes((d, d), F32) // n_load
            + 2 * (4 * _nbytes((tm, d), F32) + _nbytes((tm, d), BF16)) + 2 * _nbytes((tm, d), F32))
    return pl.pallas_call(
        functools.partial(_mixer_out_kernel, n_load=n_load),
        grid=(n_load + t // tm,),
        in_specs=[tile, tile, tile, tile,
                  _weight_chunk(w_mix["w_out_attn"], layer, n_load), _weight_chunk(w_mix["w_o"], layer, n_load)]
        + _ffn_weight_specs(w_ffn, layer, n_load),
        out_specs=tile,
        out_shape=jax.ShapeDtypeStruct((t, d), F32),
        scratch_shapes=[pltpu.VMEM((d, d), BF16), pltpu.VMEM((d, d), BF16)] + _ffn_scratch(tm, w_ffn),
        compiler_params=_params(("arbitrary",), vmem),
        name="mixer_out",
    )(x, attn, part, g2, w_mix["w_out_attn"], w_mix["w_o"], *_ffn_weight_args(w_ffn))


def kernel(x, ffn1_norm, ffn1_w_gu, ffn1_w_down, mix_norm, w_in, b_forget, b_gate, conv_w, sgu_ln_g, sgu_ln_b,
           sgu_w, sgu_b, q_norm_g, k_norm_g, w_out_conv, w_out_sgu, w_out_attn, w_o, ffn2_norm, ffn2_w_gu,
           ffn2_w_down):
    batch, seq_len, d = x.shape
    depth = w_in.shape[0]
    heads = d // HEAD_DIM
    rows = lambda a: a.reshape(depth, 1, -1)

    ffn1 = {"norm": rows(ffn1_norm), "w_gu": ffn1_w_gu, "w_down": ffn1_w_down}
    ffn2 = {"norm": rows(ffn2_norm), "w_gu": ffn2_w_gu, "w_down": ffn2_w_down}
    mix = {
        "mix_norm": rows(mix_norm),
        "w_in_t": jnp.swapaxes(w_in, 1, 2),
        "b_forget": jnp.pad(rows(b_forget), ((0, 0), (0, 0), (0, V7X_LANES - heads))),
        "b_gate": rows(b_gate),
        "conv_w": conv_w,
        "sgu_ln_g": rows(sgu_ln_g),
        "sgu_ln_b": rows(sgu_ln_b),
        "sgu_w": sgu_w,
        "sgu_b_t": jnp.swapaxes(sgu_b, 1, 2),
        "q_norm_g": rows(q_norm_g),
        "k_norm_g": rows(k_norm_g),
        "w_out_conv": w_out_conv,
        "w_out_sgu": w_out_sgu,
        "w_out_attn": w_out_attn,
        "w_o": w_o,
    }

    xt = x.reshape(batch * seq_len, d)
    for layer in range(depth):
        xt = _ffn(xt, ffn1, layer)
        part, g2, qaug, kaug, vaug, c_rows = _mixer_in(xt, mix, layer, seq_len=seq_len)
        attn = _attention(qaug, kaug, vaug, c_rows, batch=batch, seq_len=seq_len)
        xt = _mixer_out(xt, attn, part, g2, mix, ffn2, layer)
    return xt.reshape(batch, seq_len, d)
```

```python
import functools
import math

import jax
import jax.numpy as jnp
from jax import lax
from jax.experimental import pallas as pl
from jax.experimental.pallas import tpu as pltpu

F32 = jnp.float32
BF16 = jnp.bfloat16

RMS_EPS = 1e-6
LN_EPS = 1e-5
SGU_CHUNK = 128
HEAD_DIM = 128
N_BRANCH = 3
MASKED = -1e30
FORGET_SPLIT = 3
ONES_ROWS = 16

V7X_LANES = 128
V7X_SUBLANES = 8
BF16_SUBLANES = 16
V7X_VMEM_BYTES = 64 * 1024 * 1024
LOAD_STEPS = 16
FF_CHUNK = 1024
GATE_CHUNK = 256

TOKEN_TILE_FFN = 512
SEQ_BLOCK = 256


def _layer(arr, layer, cols=None, col_block=0):
    block = (None,) + tuple(arr.shape[1:-1]) + (arr.shape[-1] if cols is None else cols,)
    index = (layer,) + (0,) * (arr.ndim - 2) + (col_block,)
    return pl.BlockSpec(block, lambda *_: index, pipeline_mode=pl.Buffered(1))


def _weight_chunk(arr, layer, n_load, layer_rows=None):
    rows = (arr.shape[1] if layer_rows is None else layer_rows) // n_load
    assert rows % BF16_SUBLANES == 0, (arr.shape, n_load)
    if layer_rows is None:
        assert rows * n_load == arr.shape[1]
        return pl.BlockSpec((None, rows, arr.shape[2]), lambda s: (layer, jnp.minimum(s, n_load - 1), 0))
    assert rows * n_load == layer_rows and arr.shape[0] % layer_rows == 0
    return pl.BlockSpec((rows, arr.shape[1]), lambda s: (layer * n_load + jnp.minimum(s, n_load - 1), 0))


def _token_tile(block, n_load):
    return pl.BlockSpec(block, lambda s: (jnp.maximum(s - n_load, 0),) + (0,) * (len(block) - 1))


def _stash_rows(src_ref, dst_ref, step, cols=None):
    rows = src_ref.shape[0]
    chunk = src_ref[...] if cols is None else src_ref[:, cols]
    dst_ref[pl.ds(pl.multiple_of(step * rows, rows), rows), :] = chunk.astype(BF16)


def _nbytes(shape, dtype):
    return math.prod(shape) * jnp.dtype(dtype).itemsize


def _params(semantics, vmem_bytes):
    assert vmem_bytes <= V7X_VMEM_BYTES, vmem_bytes
    return pltpu.CompilerParams(dimension_semantics=semantics, vmem_limit_bytes=int(vmem_bytes))


def _rms_norm(x, g):
    return x * lax.rsqrt(jnp.mean(x * x, axis=-1, keepdims=True) + RMS_EPS) * g


def _gelu(x):
    return 0.5 * x * (1.0 + lax.erf(x * (2.0 ** -0.5)))


def _ff_chunks(d_ff):
    return [(c, min(c + FF_CHUNK, d_ff)) for c in range(0, d_ff, FF_CHUNK)]


def _row_groups(x):
    return [x[r:r + V7X_SUBLANES] for r in range(0, x.shape[0], V7X_SUBLANES)]


def _swiglu_half_step(x, g_ref, wgu_ref, wd_ref, act_ref):
    d_ff = wd_ref.shape[0]
    h = _rms_norm(x, g_ref[...]).astype(BF16)
    for c0, c1 in _ff_chunks(d_ff):
        g = jnp.dot(h, wgu_ref[:, c0:c1], preferred_element_type=F32)
        u = jnp.dot(h, wgu_ref[:, d_ff + c0:d_ff + c1], preferred_element_type=F32)
        act_ref[:, c0:c1] = (g * jax.nn.sigmoid(g) * u).astype(BF16)
    return x + 0.5 * jnp.dot(act_ref[...], wd_ref[...], preferred_element_type=F32)


def _ffn_weight_specs(w, layer, n_load):
    return [_layer(w["norm"], layer), _weight_chunk(w["w_gu"], layer, n_load),
            _weight_chunk(w["w_down"], layer, n_load)]


def _ffn_weight_args(w):
    return [w["norm"], w["w_gu"], w["w_down"]]


def _ffn_scratch(tm, w):
    d, d_gu = w["w_gu"].shape[1:]
    d_ff = w["w_down"].shape[1]
    return [pltpu.VMEM((d, d_gu), BF16), pltpu.VMEM((d_ff, d), BF16), pltpu.VMEM((tm, d_ff), BF16)]


def _ffn_vmem(tm, d, d_ff, n_load):
    return (_nbytes((d, 2 * d_ff), BF16) + _nbytes((d_ff, d), BF16)
            + 2 * (_nbytes((d, 2 * d_ff), F32) + _nbytes((d_ff, d), F32)) // n_load
            + _nbytes((tm, d_ff), BF16)
            + 4 * _nbytes((tm, FF_CHUNK), F32))


def _ffn_kernel(x_ref, g_ref, wgu32_ref, wd32_ref, o_ref, wgu_ref, wd_ref, act_ref, *, n_load):
    step = pl.program_id(0)

    @pl.when(step < n_load)
    def _():
        _stash_rows(wgu32_ref, wgu_ref, step)
        _stash_rows(wd32_ref, wd_ref, step)

    @pl.when(step >= n_load)
    def _():
        o_ref[...] = _swiglu_half_step(x_ref[...], g_ref, wgu_ref, wd_ref, act_ref)


def _ffn(x, w, layer):
    t, d = x.shape
    d_ff = w["w_down"].shape[1]
    tm = TOKEN_TILE_FFN
    n_load = LOAD_STEPS
    assert t % tm == 0
    tile = _token_tile((tm, d), n_load)
    vmem = _ffn_vmem(tm, d, d_ff, n_load) + 4 * _nbytes((tm, d), F32)
    return pl.pallas_call(
        functools.partial(_ffn_kernel, n_load=n_load),
        grid=(n_load + t // tm,),
        in_specs=[tile] + _ffn_weight_specs(w, layer, n_load),
        out_specs=tile,
        out_shape=jax.ShapeDtypeStruct((t, d), F32),
        scratch_shapes=_ffn_scratch(tm, w),
        compiler_params=_params(("arbitrary",), vmem),
        name="ffn",
    )(x, *_ffn_weight_args(w))


def _mixer_in_kernel(x_ref, ng_ref, wtm32_ref, wtf32_ref, wtg32_ref, woc32_ref, wos32_ref, bf_ref, bg_ref, cw_ref,
                     lng_ref, lnb_ref, sw_ref, sb_ref, qg_ref, kg_ref,
                     part_ref, g2_ref, qaug_ref, kaug_ref, vaug_ref, crow_ref,
                     wm_ref, wf_ref, wgt_ref, woc_ref, wos_ref, zs_ref, ccarry_ref, vn_ref, yb_ref, *,
                     n_load, tiles_per_seq, q_scale, c_scale):
    step = pl.program_id(0)

    @pl.when(step < n_load)
    def _():
        wm_ref[step] = wtm32_ref[...].T.astype(BF16)
        _stash_rows(woc32_ref, woc_ref, step)
        _stash_rows(wos32_ref, wos_ref, step)

    @pl.when(step < wgt_ref.shape[0])
    def _():
        wgt_ref[step] = wtg32_ref[...].T.astype(BF16)

    @pl.when(step == 0)
    def _():
        rows = wtf32_ref[...]
        slab = jnp.concatenate([rows, jnp.zeros((V7X_LANES - rows.shape[0], rows.shape[1]), F32)], axis=0)
        wf_ref[...] = slab.T.astype(BF16)

    @pl.when(step >= n_load)
    def _():
        _mixer_in_tile(step - n_load, x_ref, ng_ref, wm_ref, wf_ref, wgt_ref, bf_ref, bg_ref, cw_ref, lng_ref,
                       lnb_ref, sw_ref, sb_ref, qg_ref, kg_ref, woc_ref, wos_ref,
                       part_ref, g2_ref, qaug_ref, kaug_ref, vaug_ref, crow_ref,
                       zs_ref, ccarry_ref, vn_ref, yb_ref,
                       tiles_per_seq=tiles_per_seq, q_scale=q_scale, c_scale=c_scale)


def _mixer_in_tile(tile_idx, x_ref, ng_ref, wm_ref, wf_ref, wgt_ref, bf_ref, bg_ref, cw_ref, lng_ref, lnb_ref,
                   sw_ref, sb_ref, qg_ref, kg_ref, woc_ref, wos_ref,
                   part_ref, g2_ref, qaug_ref, kaug_ref, vaug_ref, crow_ref,
                   zs_ref, ccarry_ref, vn_ref, yb_ref, *, tiles_per_seq, q_scale, c_scale):
    tm, d = x_ref.shape
    heads = d // HEAD_DIM
    pad = V7X_SUBLANES

    @pl.when(tile_idx % tiles_per_seq == 0)
    def _():
        zs_ref[0:pad, :] = jnp.zeros((pad, d), F32)
        ccarry_ref[...] = jnp.zeros_like(ccarry_ref)

    h = _rms_norm(x_ref[...], ng_ref[...]).astype(BF16)

    def proj(w_ref, c0, c1):
        width = w_ref.shape[2]
        assert c0 % width == 0 and c1 % width == 0
        return jnp.concatenate([jnp.dot(h, w_ref[j], preferred_element_type=F32)
                                for j in range(c0 // width, c1 // width)], axis=1)

    pa = proj(wm_ref, 0, 3 * d)
    f = jnp.dot(h, wf_ref[...], preferred_element_type=F32) + bf_ref[...]
    ps = proj(wm_ref, 3 * d, 5 * d)
    pq = proj(wm_ref, 5 * d, 8 * d)
    gates = [jax.nn.sigmoid(proj(wgt_ref, b * d, (b + 1) * d) + bg_ref[:, b * d:(b + 1) * d])
             for b in range(N_BRANCH)]

    zs_ref[pad:pad + tm, :] = pa[:, d:2 * d] * pa[:, 2 * d:3 * d]
    conv = (cw_ref[0:1, :] * zs_ref[pad - 2:pad - 2 + tm, :]
            + cw_ref[1:2, :] * zs_ref[pad - 1:pad - 1 + tm, :]
            + cw_ref[2:3, :] * zs_ref[pad:pad + tm, :])
    ya_in = (pa[:, 0:d] * conv).astype(BF16)
    zs_ref[0:pad, :] = zs_ref[tm:tm + pad, :]

    c = jnp.minimum(f, 0.0) - jnp.log1p(jnp.exp(-jnp.abs(f)))
    t_idx = lax.broadcasted_iota(jnp.int32, c.shape, 0)
    shift = 1
    while shift < tm:
        c = c + jnp.where(t_idx >= shift, pltpu.roll(c, shift, axis=0), 0.0)
        shift *= 2
    c = c + ccarry_ref[0:1, :]
    ccarry_ref[0:1, :] = c[tm - 1:tm, :]
    c = c * c_scale
    crow_ref[...] = c.T[0:heads, :]

    u = _gelu(ps[:, 0:d])
    vv = _gelu(ps[:, d:2 * d])
    mu = jnp.mean(vv, axis=-1, keepdims=True)
    vc = vv - mu
    var = jnp.mean(vc * vc, axis=-1, keepdims=True)
    vn_ref[...] = (vc * lax.rsqrt(var + LN_EPS) * lng_ref[...] + lnb_ref[...]).astype(BF16)

    lane = lax.broadcasted_iota(jnp.int32, (tm, HEAD_DIM), 1)
    feature = lax.broadcasted_iota(jnp.int32, (HEAD_DIM, tm), 0)
    ones_rows = jnp.where(feature < FORGET_SPLIT, 1.0, 0.0).astype(BF16)
    qn = []
    for g in range(heads):
        hs = slice(g * HEAD_DIM, (g + 1) * HEAD_DIM)
        qn.append(_rms_norm(pq[:, hs], qg_ref[:, hs]) * q_scale)
        ks = slice(d + g * HEAD_DIM, d + (g + 1) * HEAD_DIM)
        kaug_ref[:, 2 * g * HEAD_DIM:(2 * g + 1) * HEAD_DIM] = _rms_norm(pq[:, ks], kg_ref[:, hs]).astype(BF16)
        rest = -jnp.broadcast_to(c[:, g:g + 1], (tm, HEAD_DIM))
        slab = jnp.zeros((tm, HEAD_DIM), F32)
        for term in range(FORGET_SPLIT):
            piece = rest.astype(BF16).astype(F32)
            slab = jnp.where(lane == term, piece, slab)
            rest = rest - piece
        kaug_ref[:, (2 * g + 1) * HEAD_DIM:(2 * g + 2) * HEAD_DIM] = slab.astype(BF16)
    qt = jnp.concatenate(qn, axis=1).T.astype(BF16)
    vt = pq[:, 2 * d:3 * d].T.astype(BF16)
    v_rows = HEAD_DIM + ONES_ROWS
    for g in range(heads):
        hs = slice(g * HEAD_DIM, (g + 1) * HEAD_DIM)
        qaug_ref[2 * g * HEAD_DIM:(2 * g + 1) * HEAD_DIM, :] = qt[hs, :]
        qaug_ref[(2 * g + 1) * HEAD_DIM:(2 * g + 2) * HEAD_DIM, :] = ones_rows
        vaug_ref[g * v_rows:g * v_rows + HEAD_DIM, :] = vt[hs, :]
        vaug_ref[g * v_rows + HEAD_DIM:(g + 1) * v_rows, :] = jnp.ones((ONES_ROWS, tm), BF16)

    ya = jnp.dot(ya_in, woc_ref[...], preferred_element_type=F32)
    n_chunks = tm // SGU_CHUNK
    pos_t = lax.broadcasted_iota(jnp.int32, (SGU_CHUNK, SGU_CHUNK), 0)
    pos_s = lax.broadcasted_iota(jnp.int32, (SGU_CHUNK, SGU_CHUNK), 1)
    for g in range(heads):
        hs = slice(g * HEAD_DIM, (g + 1) * HEAD_DIM)
        w = jnp.where(pos_s <= pos_t, sw_ref[g], 0.0).astype(BF16)
        rhs = jnp.concatenate([vn_ref[c0 * SGU_CHUNK:(c0 + 1) * SGU_CHUNK, hs] for c0 in range(n_chunks)], axis=1)
        s = jnp.dot(w, rhs, preferred_element_type=F32) + sb_ref[:, g:g + 1]
        for c0 in range(n_chunks):
            rows = slice(c0 * SGU_CHUNK, (c0 + 1) * SGU_CHUNK)
            yb_ref[rows, hs] = (u[rows, hs] * s[:, c0 * SGU_CHUNK:(c0 + 1) * SGU_CHUNK]).astype(BF16)
    yb = jnp.dot(yb_ref[...], wos_ref[...], preferred_element_type=F32)
    part_ref[...] = gates[0] * ya + gates[1] * yb
    g2_ref[...] = gates[2]


def _mixer_in(x, w, layer, *, seq_len):
    t, d = x.shape
    tm = SEQ_BLOCK
    heads = d // HEAD_DIM
    assert seq_len % tm == 0 and tm % SGU_CHUNK == 0 and d % HEAD_DIM == 0 and heads <= V7X_SUBLANES
    n_load = LOAD_STEPS
    w_t = w["w_in_t"]
    n_in = w_t.shape[1]
    n_main = 8 * d
    main_rows = n_main // n_load
    n_gate = N_BRANCH * d // GATE_CHUNK
    assert n_in == n_main + heads + N_BRANCH * d and main_rows * n_load == n_main and main_rows % V7X_LANES == 0
    assert n_gate * GATE_CHUNK == N_BRANCH * d and n_gate <= n_load and d % GATE_CHUNK == 0
    kernel = functools.partial(
        _mixer_in_kernel, n_load=n_load, tiles_per_seq=seq_len // tm,
        q_scale=HEAD_DIM ** -0.5 * math.log2(math.e), c_scale=math.log2(math.e))
    main_spec = pl.BlockSpec((None, main_rows, d), lambda s: (layer, jnp.minimum(s, n_load - 1), 0))
    forget_spec = pl.BlockSpec((pl.Squeezed(), pl.Element(V7X_SUBLANES), pl.Element(d)),
                               lambda s: (layer, n_main, 0))
    gate_align = math.gcd(n_main + heads, GATE_CHUNK, V7X_SUBLANES)
    gate_spec = pl.BlockSpec(
        (pl.Squeezed(), pl.Element(GATE_CHUNK), pl.Element(d)),
        lambda s: (layer, pl.multiple_of(n_main + heads + GATE_CHUNK * jnp.minimum(s, n_gate - 1), gate_align), 0))
    streamed = ["w_out_conv", "w_out_sgu"]
    small = ["b_forget", "b_gate", "conv_w", "sgu_ln_g", "sgu_ln_b", "sgu_w", "sgu_b_t", "q_norm_g", "k_norm_g"]
    weight_specs = ([_layer(w["mix_norm"], layer), main_spec, forget_spec, gate_spec]
                    + [_weight_chunk(w[n], layer, n_load) for n in streamed] + [_layer(w[n], layer) for n in small])
    weight_args = [w["mix_norm"], w_t, w_t, w_t] + [w[n] for n in streamed] + [w[n] for n in small]
    v_rows = heads * (HEAD_DIM + ONES_ROWS)
    tile = _token_tile((tm, d), n_load)
    resident = [pltpu.VMEM((n_load, d, main_rows), BF16), pltpu.VMEM((d, V7X_LANES), BF16),
                pltpu.VMEM((n_gate, d, GATE_CHUNK), BF16), pltpu.VMEM((d, d), BF16), pltpu.VMEM((d, d), BF16)]
    scratch = resident + [pltpu.VMEM((tm + V7X_SUBLANES, d), F32), pltpu.VMEM((V7X_SUBLANES, V7X_LANES), F32),
                          pltpu.VMEM((tm, d), BF16), pltpu.VMEM((tm, d), BF16)]
    vmem = (sum(_nbytes(s.shape, s.dtype) for s in scratch)
            + 2 * _nbytes((main_rows + GATE_CHUNK + V7X_SUBLANES + 2 * d // n_load, d), F32)
            + 2 * _nbytes((main_rows, d), F32)
            + sum(_nbytes(w[n].shape[1:], w[n].dtype) for n in small)
            + 2 * (3 * _nbytes((tm, d), F32) + 4 * _nbytes((tm, d), BF16) + _nbytes((v_rows, tm), BF16)
                   + _nbytes((V7X_SUBLANES, tm), F32))
            + 10 * _nbytes((tm, d), F32))
    return pl.pallas_call(
        kernel,
        grid=(n_load + t // tm,),
        in_specs=[tile] + weight_specs,
        out_specs=[tile, tile,
                   _token_tile((None, 2 * d, tm), n_load),
                   _token_tile((tm, 2 * d), n_load),
                   _token_tile((None, v_rows, tm), n_load),
                   _token_tile((None, heads, tm), n_load)],
        out_shape=[jax.ShapeDtypeStruct((t, d), F32), jax.ShapeDtypeStruct((t, d), F32),
                   jax.ShapeDtypeStruct((t // tm, 2 * d, tm), BF16), jax.ShapeDtypeStruct((t, 2 * d), BF16),
                   jax.ShapeDtypeStruct((t // tm, v_rows, tm), BF16),
                   jax.ShapeDtypeStruct((t // tm, heads, tm), F32)],
        scratch_shapes=scratch,
        compiler_params=_params(("arbitrary",), vmem),
        name="mixer_in",
    )(x, *weight_args)


def _attn_kernel(qaug_ref, kaug_ref, vaug_ref, crow_ref, o_ref, s0_ref, s1_ref, m_ref, acc_ref):
    tq = qaug_ref.shape[2]
    heads = m_ref.shape[0]
    g = pl.program_id(1)
    ia, ib = 2 * g, 2 * g + 1
    head_cols = [slice(h * HEAD_DIM, (h + 1) * HEAD_DIM) for h in range(heads)]
    qk_cols = [slice(h * 2 * HEAD_DIM, (h + 1) * 2 * HEAD_DIM) for h in range(heads)]
    v_rows = [slice(h * (HEAD_DIM + ONES_ROWS), (h + 1) * (HEAD_DIM + ONES_ROWS)) for h in range(heads)]
    key_pos = lax.broadcasted_iota(jnp.int32, (tq, tq), 0)
    query_pos = lax.broadcasted_iota(jnp.int32, (tq, tq), 1)
    causal = key_pos <= query_pos

    s_refs = (s0_ref, s1_ref)

    def reset():
        m_ref[...] = jnp.full(m_ref.shape, MASKED, F32)
        acc_ref[...] = jnp.zeros(acc_ref.shape, F32)

    def logits(qb, j, slot, diagonal):
        rows = pl.ds(pl.multiple_of(j * tq, tq), tq)
        for h in range(heads):
            s = jnp.dot(kaug_ref[rows, qk_cols[h]], qaug_ref[qb, qk_cols[h], :], preferred_element_type=F32)
            s_refs[slot][h] = jnp.where(causal, s, MASKED) if diagonal else s

    def softmax_pv(i, j, slot):
        probs, rescale = [], []
        for h in range(heads):
            s = s_refs[slot][h]
            cq = crow_ref[i, h:h + 1, :]
            m_old = m_ref[h]
            block_max = jnp.max(functools.reduce(jnp.maximum, _row_groups(s)), axis=0, keepdims=True)
            m_new = jnp.maximum(m_old, block_max + cq)
            m_ref[h] = m_new
            probs.append(jnp.exp2(s + (cq - m_new)).astype(BF16))
            rescale.append(jnp.exp2(m_old - m_new))
        for h in range(heads):
            acc_ref[h] = rescale[h] * acc_ref[h] + jnp.dot(vaug_ref[j, v_rows[h], :], probs[h],
                                                           preferred_element_type=F32)

    def finish(qb):
        for h, hs in enumerate(head_cols):
            row_sum = acc_ref[h, HEAD_DIM:HEAD_DIM + 1, :]
            o_ref[qb * tq:(qb + 1) * tq, hs] = (acc_ref[h, 0:HEAD_DIM, :] * (1.0 / row_sum)).T.astype(BF16)

    def stage(qb_next, j_next, slot_next, diagonal, i, j, slot):
        logits(qb_next, j_next, slot_next, diagonal)
        softmax_pv(i, j, slot)

    reset()

    @pl.when(g == 0)
    def _():
        logits(0, 0, 0, True)
        stage(1, 0, 1, False, ia, 0, 0)
        finish(0)
        reset()
        stage(1, 1, 0, True, ib, 0, 1)
        softmax_pv(ib, 1, 0)
        finish(1)

    @pl.when(g > 0)
    def _():
        logits(0, 0, 0, False)

        def pair_a(t, carry):
            j = 2 * t
            stage(0, j + 1, 1, False, ia, j, 0)
            stage(0, j + 2, 0, False, ia, j + 1, 1)
            return carry

        lax.fori_loop(0, g - 1, pair_a, 0)
        stage(0, ia - 1, 1, False, ia, ia - 2, 0)
        stage(0, ia, 0, True, ia, ia - 1, 1)
        stage(1, 0, 1, False, ia, ia, 0)
        finish(0)
        reset()

        def pair_b(t, carry):
            j = 2 * t
            stage(1, j + 1, 0, False, ib, j, 1)
            stage(1, j + 2, 1, False, ib, j + 1, 0)
            return carry

        lax.fori_loop(0, g, pair_b, 0)
        stage(1, ib, 0, True, ib, ia, 1)
        softmax_pv(ib, ib, 0)
        finish(1)


def _attention(qaug, kaug, vaug, c_rows, *, batch, seq_len):
    nt, qk_rows, tq = qaug.shape
    nq = seq_len // tq
    heads = qk_rows // (2 * HEAD_DIM)
    d = heads * HEAD_DIM
    v_rows = vaug.shape[1]
    assert nt == batch * nq and tq % V7X_LANES == 0 and v_rows == heads * (HEAD_DIM + ONES_ROWS)
    assert nq % 2 == 0
    steps = nq // 2
    scratch = [pltpu.VMEM((heads, tq, tq), F32), pltpu.VMEM((heads, tq, tq), F32),
               pltpu.VMEM((heads, 1, tq), F32),
               pltpu.VMEM((heads, HEAD_DIM + ONES_ROWS, tq), F32)]
    vmem = (2 * (_nbytes((seq_len, qk_rows), BF16) + _nbytes((nq, v_rows, tq), BF16)
                 + 2 * _nbytes((qk_rows, tq), BF16) + 2 * _nbytes((tq, d), BF16)
                 + _nbytes((nq, V7X_SUBLANES, tq), F32))
            + sum(_nbytes(s.shape, s.dtype) for s in scratch)
            + _nbytes((V7X_SUBLANES * heads, tq), F32)
            + 16 * _nbytes((tq, tq), F32))
    return pl.pallas_call(
        _attn_kernel,
        grid=(batch, steps),
        in_specs=[pl.BlockSpec((2, qk_rows, tq), lambda b, g: (b * steps + g, 0, 0)),
                  pl.BlockSpec((seq_len, qk_rows), lambda b, g: (b, 0)),
                  pl.BlockSpec((nq, v_rows, tq), lambda b, g: (b, 0, 0)),
                  pl.BlockSpec((nq, heads, tq), lambda b, g: (b, 0, 0))],
        out_specs=pl.BlockSpec((2 * tq, d), lambda b, g: (b * steps + g, 0)),
        out_shape=jax.ShapeDtypeStruct((nt * tq, d), BF16),
        scratch_shapes=scratch,
        compiler_params=_params(("arbitrary", "arbitrary"), vmem),
        name="attention",
    )(qaug, kaug, vaug, c_rows)


def _mixer_out_kernel(x_ref, a_ref, part_ref, g2_ref, woa32_ref, wo32_ref, g_ref, wgu32_ref, wd32_ref,
                      o_ref, woa_ref, wo_ref, wgu_ref, wd_ref, act_ref, *, n_load):
    step = pl.program_id(0)

    @pl.when(step < n_load)
    def _():
        _stash_rows(woa32_ref, woa_ref, step)
        _stash_rows(wo32_ref, wo_ref, step)
        _stash_rows(wgu32_ref, wgu_ref, step)
        _stash_rows(wd32_ref, wd_ref, step)

    @pl.when(step >= n_load)
    def _():
        yc = jnp.dot(a_ref[...], woa_ref[...], preferred_element_type=F32)
        merged = part_ref[...] + g2_ref[...] * yc
        x = x_ref[...] + jnp.dot(merged.astype(BF16), wo_ref[...], preferred_element_type=F32)
        o_ref[...] = _swiglu_half_step(x, g_ref, wgu_ref, wd_ref, act_ref)


def _mixer_out(x, attn, part, g2, w_mix, w_ffn, layer):
    t, d = x.shape
    d_ff = w_ffn["w_down"].shape[1]
    tm = TOKEN_TILE_FFN
    n_load = LOAD_STEPS
    assert t % tm == 0
    tile = _token_tile((tm, d), n_load)
    vmem = (_ffn_vmem(tm, d, d_ff, n_load) + 2 * _nbytes((d, d), BF16) + 4 * _nbytes((d, d), F32) // n_load
            + 2 * (4 * _nbytes((tm, d), F32) + _nbytes((tm, d), BF16)) + 2 * _nbytes((tm, d), F32))
    return pl.pallas_call(
        functools.partial(_mixer_out_kernel, n_load=n_load),
        grid=(n_load + t // tm,),
        in_specs=[tile, tile, tile, tile,
                  _weight_chunk(w_mix["w_out_attn"], layer, n_load), _weight_chunk(w_mix["w_o"], layer, n_load)]
        + _ffn_weight_specs(w_ffn, layer, n_load),
        out_specs=tile,
        out_shape=jax.ShapeDtypeStruct((t, d), F32),
        scratch_shapes=[pltpu.VMEM((d, d), BF16), pltpu.VMEM((d, d), BF16)] + _ffn_scratch(tm, w_ffn),
        compiler_params=_params(("arbitrary",), vmem),
        name="mixer_out",
    )(x, attn, part, g2, w_mix["w_out_attn"], w_mix["w_o"], *_ffn_weight_args(w_ffn))


def kernel(x, ffn1_norm, ffn1_w_gu, ffn1_w_down, mix_norm, w_in, b_forget, b_gate, conv_w, sgu_ln_g, sgu_ln_b,
           sgu_w, sgu_b, q_norm_g, k_norm_g, w_out_conv, w_out_sgu, w_out_attn, w_o, ffn2_norm, ffn2_w_gu,
           ffn2_w_down):
    batch, seq_len, d = x.shape
    depth = w_in.shape[0]
    heads = d // HEAD_DIM
    rows = lambda a: a.reshape(depth, 1, -1)

    ffn1 = {"norm": rows(ffn1_norm), "w_gu": ffn1_w_gu, "w_down": ffn1_w_down}
    ffn2 = {"norm": rows(ffn2_norm), "w_gu": ffn2_w_gu, "w_down": ffn2_w_down}
    mix = {
        "mix_norm": rows(mix_norm),
        "w_in_t": jnp.swapaxes(w_in, 1, 2),
        "b_forget": jnp.pad(rows(b_forget), ((0, 0), (0, 0), (0, V7X_LANES - heads))),
        "b_gate": rows(b_gate),
        "conv_w": conv_w,
        "sgu_ln_g": rows(sgu_ln_g),
        "sgu_ln_b": rows(sgu_ln_b),
        "sgu_w": sgu_w,
        "sgu_b_t": jnp.swapaxes(sgu_b, 1, 2),
        "q_norm_g": rows(q_norm_g),
        "k_norm_g": rows(k_norm_g),
        "w_out_conv": w_out_conv,
        "w_out_sgu": w_out_sgu,
        "w_out_attn": w_out_attn,
        "w_o": w_o,
    }

    xt = x.reshape(batch * seq_len, d)
    for layer in range(depth):
        xt = _ffn(xt, ffn1, layer)
        part, g2, qaug, kaug, vaug, c_rows = _mixer_in(xt, mix, layer, seq_len=seq_len)
        attn = _attention(qaug, kaug, vaug, c_rows, batch=batch, seq_len=seq_len)
        xt = _mixer_out(xt, attn, part, g2, mix, ffn2, layer)
    return xt.reshape(batch, seq_len, d)
```

```python
import functools
import math

import jax
import jax.numpy as jnp
from jax import lax
from jax.experimental import pallas as pl
from jax.experimental.pallas import tpu as pltpu

F32 = jnp.float32
BF16 = jnp.bfloat16

RMS_EPS = 1e-6
LN_EPS = 1e-5
SGU_CHUNK = 128
HEAD_DIM = 128
N_BRANCH = 3
MASKED = -1e30
FORGET_SPLIT = 3
ONES_ROWS = 16

V7X_LANES = 128
V7X_SUBLANES = 8
BF16_SUBLANES = 16
V7X_VMEM_BYTES = 64 * 1024 * 1024
LOAD_STEPS = 16
FF_CHUNK = 1024
GATE_CHUNK = 256

TOKEN_TILE_FFN = 512
SEQ_BLOCK = 256


def _layer(arr, layer, cols=None, col_block=0):
    block = (None,) + tuple(arr.shape[1:-1]) + (arr.shape[-1] if cols is None else cols,)
    index = (layer,) + (0,) * (arr.ndim - 2) + (col_block,)
    return pl.BlockSpec(block, lambda *_: index, pipeline_mode=pl.Buffered(1))


def _weight_chunk(arr, layer, n_load, layer_rows=None):
    rows = (arr.shape[1] if layer_rows is None else layer_rows) // n_load
    assert rows % BF16_SUBLANES == 0, (arr.shape, n_load)
    if layer_rows is None:
        assert rows * n_load == arr.shape[1]
        return pl.BlockSpec((None, rows, arr.shape[2]), lambda s: (layer, jnp.minimum(s, n_load - 1), 0))
    assert rows * n_load == layer_rows and arr.shape[0] % layer_rows == 0
    return pl.BlockSpec((rows, arr.shape[1]), lambda s: (layer * n_load + jnp.minimum(s, n_load - 1), 0))


def _token_tile(block, n_load):
    return pl.BlockSpec(block, lambda s: (jnp.maximum(s - n_load, 0),) + (0,) * (len(block) - 1))


def _stash_rows(src_ref, dst_ref, step, cols=None):
    rows = src_ref.shape[0]
    chunk = src_ref[...] if cols is None else src_ref[:, cols]
    dst_ref[pl.ds(pl.multiple_of(step * rows, rows), rows), :] = chunk.astype(BF16)


def _nbytes(shape, dtype):
    return math.prod(shape) * jnp.dtype(dtype).itemsize


def _params(semantics, vmem_bytes):
    assert vmem_bytes <= V7X_VMEM_BYTES, vmem_bytes
    return pltpu.CompilerParams(dimension_semantics=semantics, vmem_limit_bytes=int(vmem_bytes))


def _rms_norm(x, g):
    return x * lax.rsqrt(jnp.mean(x * x, axis=-1, keepdims=True) + RMS_EPS) * g


def _gelu(x):
    return 0.5 * x * (1.0 + lax.erf(x * (2.0 ** -0.5)))


def _ff_chunks(d_ff):
    return [(c, min(c + FF_CHUNK, d_ff)) for c in range(0, d_ff, FF_CHUNK)]


def _row_groups(x):
    return [x[r:r + V7X_SUBLANES] for r in range(0, x.shape[0], V7X_SUBLANES)]


def _swiglu_half_step(x, g_ref, wgu_ref, wd_ref, act_ref):
    d_ff = wd_ref.shape[0]
    h = _rms_norm(x, g_ref[...]).astype(BF16)
    for c0, c1 in _ff_chunks(d_ff):
        g = jnp.dot(h, wgu_ref[:, c0:c1], preferred_element_type=F32)
        u = jnp.dot(h, wgu_ref[:, d_ff + c0:d_ff + c1], preferred_element_type=F32)
        act_ref[:, c0:c1] = (g * jax.nn.sigmoid(g) * u).astype(BF16)
    return x + 0.5 * jnp.dot(act_ref[...], wd_ref[...], preferred_element_type=F32)


def _ffn_weight_specs(w, layer, n_load):
    return [_layer(w["norm"], layer), _weight_chunk(w["w_gu"], layer, n_load),
            _weight_chunk(w["w_down"], layer, n_load)]


def _ffn_weight_args(w):
    return [w["norm"], w["w_gu"], w["w_down"]]


def _ffn_scratch(tm, w):
    d, d_gu = w["w_gu"].shape[1:]
    d_ff = w["w_down"].shape[1]
    return [pltpu.VMEM((d, d_gu), BF16), pltpu.VMEM((d_ff, d), BF16), pltpu.VMEM((tm, d_ff), BF16)]


def _ffn_vmem(tm, d, d_ff, n_load):
    return (_nbytes((d, 2 * d_ff), BF16) + _nbytes((d_ff, d), BF16)
            + 2 * (_nbytes((d, 2 * d_ff), F32) + _nbytes((d_ff, d), F32)) // n_load
            + _nbytes((tm, d_ff), BF16)
            + 4 * _nbytes((tm, FF_CHUNK), F32))


def _ffn_kernel(x_ref, g_ref, wgu32_ref, wd32_ref, o_ref, wgu_ref, wd_ref, act_ref, *, n_load):
    step = pl.program_id(0)

    @pl.when(step < n_load)
    def _():
        _stash_rows(wgu32_ref, wgu_ref, step)
        _stash_rows(wd32_ref, wd_ref, step)

    @pl.when(step >= n_load)
    def _():
        o_ref[...] = _swiglu_half_step(x_ref[...], g_ref, wgu_ref, wd_ref, act_ref)


def _ffn(x, w, layer):
    t, d = x.shape
    d_ff = w["w_down"].shape[1]
    tm = TOKEN_TILE_FFN
    n_load = LOAD_STEPS
    assert t % tm == 0
    tile = _token_tile((tm, d), n_load)
    vmem = _ffn_vmem(tm, d, d_ff, n_load) + 4 * _nbytes((tm, d), F32)
    return pl.pallas_call(
        functools.partial(_ffn_kernel, n_load=n_load),
        grid=(n_load + t // tm,),
        in_specs=[tile] + _ffn_weight_specs(w, layer, n_load),
        out_specs=tile,
        out_shape=jax.ShapeDtypeStruct((t, d), F32),
        scratch_shapes=_ffn_scratch(tm, w),
        compiler_params=_params(("arbitrary",), vmem),
        name="ffn",
    )(x, *_ffn_weight_args(w))


def _mixer_in_kernel(x_ref, ng_ref, wtm32_ref, wtf32_ref, wtg32_ref, woc32_ref, wos32_ref, bf_ref, bg_ref, cw_ref,
                     lng_ref, lnb_ref, sw_ref, sb_ref, qg_ref, kg_ref,
                     part_ref, g2_ref, qaug_ref, kaug_ref, vaug_ref, crow_ref,
                     wm_ref, wf_ref, wgt_ref, woc_ref, wos_ref, zs_ref, ccarry_ref, vn_ref, yb_ref, *,
                     n_load, tiles_per_seq, q_scale, c_scale):
    step = pl.program_id(0)

    @pl.when(step < n_load)
    def _():
        wm_ref[step] = wtm32_ref[...].T.astype(BF16)
        _stash_rows(woc32_ref, woc_ref, step)
        _stash_rows(wos32_ref, wos_ref, step)

    @pl.when(step < wgt_ref.shape[0])
    def _():
        wgt_ref[step] = wtg32_ref[...].T.astype(BF16)

    @pl.when(step == 0)
    def _():
        rows = wtf32_ref[...]
        slab = jnp.concatenate([rows, jnp.zeros((V7X_LANES - rows.shape[0], rows.shape[1]), F32)], axis=0)
        wf_ref[...] = slab.T.astype(BF16)

    @pl.when(step >= n_load)
    def _():
        _mixer_in_tile(step - n_load, x_ref, ng_ref, wm_ref, wf_ref, wgt_ref, bf_ref, bg_ref, cw_ref, lng_ref,
                       lnb_ref, sw_ref, sb_ref, qg_ref, kg_ref, woc_ref, wos_ref,
                       part_ref, g2_ref, qaug_ref, kaug_ref, vaug_ref, crow_ref,
                       zs_ref, ccarry_ref, vn_ref, yb_ref,
                       tiles_per_seq=tiles_per_seq, q_scale=q_scale, c_scale=c_scale)


def _mixer_in_tile(tile_idx, x_ref, ng_ref, wm_ref, wf_ref, wgt_ref, bf_ref, bg_ref, cw_ref, lng_ref, lnb_ref,
                   sw_ref, sb_ref, qg_ref, kg_ref, woc_ref, wos_ref,
                   part_ref, g2_ref, qaug_ref, kaug_ref, vaug_ref, crow_ref,
                   zs_ref, ccarry_ref, vn_ref, yb_ref, *, tiles_per_seq, q_scale, c_scale):
    tm, d = x_ref.shape
    heads = d // HEAD_DIM
    pad = V7X_SUBLANES

    @pl.when(tile_idx % tiles_per_seq == 0)
    def _():
        zs_ref[0:pad, :] = jnp.zeros((pad, d), F32)
        ccarry_ref[...] = jnp.zeros_like(ccarry_ref)

    h = _rms_norm(x_ref[...], ng_ref[...]).astype(BF16)

    def proj(w_ref, c0, c1):
        width = w_ref.shape[2]
        assert c0 % width == 0 and c1 % width == 0
        return jnp.concatenate([jnp.dot(h, w_ref[j], preferred_element_type=F32)
                                for j in range(c0 // width, c1 // width)], axis=1)

    pa = proj(wm_ref, 0, 3 * d)
    f = jnp.dot(h, wf_ref[...], preferred_element_type=F32) + bf_ref[...]
    ps = proj(wm_ref, 3 * d, 5 * d)
    pq = proj(wm_ref, 5 * d, 8 * d)
    gates = [jax.nn.sigmoid(proj(wgt_ref, b * d, (b + 1) * d) + bg_ref[:, b * d:(b + 1) * d])
             for b in range(N_BRANCH)]

    zs_ref[pad:pad + tm, :] = pa[:, d:2 * d] * pa[:, 2 * d:3 * d]
    conv = (cw_ref[0:1, :] * zs_ref[pad - 2:pad - 2 + tm, :]
            + cw_ref[1:2, :] * zs_ref[pad - 1:pad - 1 + tm, :]
            + cw_ref[2:3, :] * zs_ref[pad:pad + tm, :])
    ya_in = (pa[:, 0:d] * conv).astype(BF16)
    zs_ref[0:pad, :] = zs_ref[tm:tm + pad, :]

    c = jnp.minimum(f, 0.0) - jnp.log1p(jnp.exp(-jnp.abs(f)))
    t_idx = lax.broadcasted_iota(jnp.int32, c.shape, 0)
    shift = 1
    while shift < tm:
        c = c + jnp.where(t_idx >= shift, pltpu.roll(c, shift, axis=0), 0.0)
        shift *= 2
    c = c + ccarry_ref[0:1, :]
    ccarry_ref[0:1, :] = c[tm - 1:tm, :]
    c = c * c_scale
    crow_ref[...] = c.T[0:heads, :]

    u = _gelu(ps[:, 0:d])
    vv = _gelu(ps[:, d:2 * d])
    mu = jnp.mean(vv, axis=-1, keepdims=True)
    vc = vv - mu
    var = jnp.mean(vc * vc, axis=-1, keepdims=True)
    vn_ref[...] = (vc * lax.rsqrt(var + LN_EPS) * lng_ref[...] + lnb_ref[...]).astype(BF16)

    lane = lax.broadcasted_iota(jnp.int32, (tm, HEAD_DIM), 1)
    feature = lax.broadcasted_iota(jnp.int32, (HEAD_DIM, tm), 0)
    ones_rows = jnp.where(feature < FORGET_SPLIT, 1.0, 0.0).astype(BF16)
    qn = []
    for g in range(heads):
        hs = slice(g * HEAD_DIM, (g + 1) * HEAD_DIM)
        qn.append(_rms_norm(pq[:, hs], qg_ref[:, hs]) * q_scale)
        ks = slice(d + g * HEAD_DIM, d + (g + 1) * HEAD_DIM)
        kaug_ref[:, 2 * g * HEAD_DIM:(2 * g + 1) * HEAD_DIM] = _rms_norm(pq[:, ks], kg_ref[:, hs]).astype(BF16)
        rest = -jnp.broadcast_to(c[:, g:g + 1], (tm, HEAD_DIM))
        slab = jnp.zeros((tm, HEAD_DIM), F32)
        for term in range(FORGET_SPLIT):
            piece = rest.astype(BF16).astype(F32)
            slab = jnp.where(lane == term, piece, slab)
            rest = rest - piece
        kaug_ref[:, (2 * g + 1) * HEAD_DIM:(2 * g + 2) * HEAD_DIM] = slab.astype(BF16)
    qt = jnp.concatenate(qn, axis=1).T.astype(BF16)
    vt = pq[:, 2 * d:3 * d].T.astype(BF16)
    v_rows = HEAD_DIM + ONES_ROWS
    for g in range(heads):
        hs = slice(g * HEAD_DIM, (g + 1) * HEAD_DIM)
        qaug_ref[2 * g * HEAD_DIM:(2 * g + 1) * HEAD_DIM, :] = qt[hs, :]
        qaug_ref[(2 * g + 1) * HEAD_DIM:(2 * g + 2) * HEAD_DIM, :] = ones_rows
        vaug_ref[g * v_rows:g * v_rows + HEAD_DIM, :] = vt[hs, :]
        vaug_ref[g * v_rows + HEAD_DIM:(g + 1) * v_rows, :] = jnp.ones((ONES_ROWS, tm), BF16)

    ya = jnp.dot(ya_in, woc_ref[...], preferred_element_type=F32)
    n_chunks = tm // SGU_CHUNK
    pos_t = lax.broadcasted_iota(jnp.int32, (SGU_CHUNK, SGU_CHUNK), 0)
    pos_s = lax.broadcasted_iota(jnp.int32, (SGU_CHUNK, SGU_CHUNK), 1)
    for g in range(heads):
        hs = slice(g * HEAD_DIM, (g + 1) * HEAD_DIM)
        w = jnp.where(pos_s <= pos_t, sw_ref[g], 0.0).astype(BF16)
        rhs = jnp.concatenate([vn_ref[c0 * SGU_CHUNK:(c0 + 1) * SGU_CHUNK, hs] for c0 in range(n_chunks)], axis=1)
        s = jnp.dot(w, rhs, preferred_element_type=F32) + sb_ref[:, g:g + 1]
        for c0 in range(n_chunks):
            rows = slice(c0 * SGU_CHUNK, (c0 + 1) * SGU_CHUNK)
            yb_ref[rows, hs] = (u[rows, hs] * s[:, c0 * SGU_CHUNK:(c0 + 1) * SGU_CHUNK]).astype(BF16)
    yb = jnp.dot(yb_ref[...], wos_ref[...], preferred_element_type=F32)
    part_ref[...] = gates[0] * ya + gates[1] * yb
    g2_ref[...] = gates[2]


def _mixer_in(x, w, layer, *, seq_len):
    t, d = x.shape
    tm = SEQ_BLOCK
    heads = d // HEAD_DIM
    assert seq_len % tm == 0 and tm % SGU_CHUNK == 0 and d % HEAD_DIM == 0 and heads <= V7X_SUBLANES
    n_load = LOAD_STEPS
    w_t = w["w_in_t"]
    n_in = w_t.shape[1]
    n_main = 8 * d
    main_rows = n_main // n_load
    n_gate = N_BRANCH * d // GATE_CHUNK
    assert n_in == n_main + heads + N_BRANCH * d and main_rows * n_load == n_main and main_rows % V7X_LANES == 0
    assert n_gate * GATE_CHUNK == N_BRANCH * d and n_gate <= n_load and d % GATE_CHUNK == 0
    kernel = functools.partial(
        _mixer_in_kernel, n_load=n_load, tiles_per_seq=seq_len // tm,
        q_scale=HEAD_DIM ** -0.5 * math.log2(math.e), c_scale=math.log2(math.e))
    main_spec = pl.BlockSpec((None, main_rows, d), lambda s: (layer, jnp.minimum(s, n_load - 1), 0))
    forget_spec = pl.BlockSpec((pl.Squeezed(), pl.Element(V7X_SUBLANES), pl.Element(d)),
                               lambda s: (layer, n_main, 0))
    gate_align = math.gcd(n_main + heads, GATE_CHUNK, V7X_SUBLANES)
    gate_spec = pl.BlockSpec(
        (pl.Squeezed(), pl.Element(GATE_CHUNK), pl.Element(d)),
        lambda s: (layer, pl.multiple_of(n_main + heads + GATE_CHUNK * jnp.minimum(s, n_gate - 1), gate_align), 0))
    streamed = ["w_out_conv", "w_out_sgu"]
    small = ["b_forget", "b_gate", "conv_w", "sgu_ln_g", "sgu_ln_b", "sgu_w", "sgu_b_t", "q_norm_g", "k_norm_g"]
    weight_specs = ([_layer(w["mix_norm"], layer), main_spec, forget_spec, gate_spec]
                    + [_weight_chunk(w[n], layer, n_load) for n in streamed] + [_layer(w[n], layer) for n in small])
    weight_args = [w["mix_norm"], w_t, w_t, w_t] + [w[n] for n in streamed] + [w[n] for n in small]
    v_rows = heads * (HEAD_DIM + ONES_ROWS)
    tile = _token_tile((tm, d), n_load)
    resident = [pltpu.VMEM((n_load, d, main_rows), BF16), pltpu.VMEM((d, V7X_LANES), BF16),
                pltpu.VMEM((n_gate, d, GATE_CHUNK), BF16), pltpu.VMEM((d, d), BF16), pltpu.VMEM((d, d), BF16)]
    scratch = resident + [pltpu.VMEM((tm + V7X_SUBLANES, d), F32), pltpu.VMEM((V7X_SUBLANES, V7X_LANES), F32),
                          pltpu.VMEM((tm, d), BF16), pltpu.VMEM((tm, d), BF16)]
    vmem = (sum(_nbytes(s.shape, s.dtype) for s in scratch)
            + 2 * _nbytes((main_rows + GATE_CHUNK + V7X_SUBLANES + 2 * d // n_load, d), F32)
            + 2 * _nbytes((main_rows, d), F32)
            + sum(_nbytes(w[n].shape[1:], w[n].dtype) for n in small)
            + 2 * (3 * _nbytes((tm, d), F32) + 4 * _nbytes((tm, d), BF16) + _nbytes((v_rows, tm), BF16)
                   + _nbytes((V7X_SUBLANES, tm), F32))
            + 10 * _nbytes((tm, d), F32))
    return pl.pallas_call(
        kernel,
        grid=(n_load + t // tm,),
        in_specs=[tile] + weight_specs,
        out_specs=[tile, tile,
                   _token_tile((None, 2 * d, tm), n_load),
                   _token_tile((tm, 2 * d), n_load),
                   _token_tile((None, v_rows, tm), n_load),
                   _token_tile((None, heads, tm), n_load)],
        out_shape=[jax.ShapeDtypeStruct((t, d), F32), jax.ShapeDtypeStruct((t, d), F32),
                   jax.ShapeDtypeStruct((t // tm, 2 * d, tm), BF16), jax.ShapeDtypeStruct((t, 2 * d), BF16),
                   jax.ShapeDtypeStruct((t // tm, v_rows, tm), BF16),
                   jax.ShapeDtypeStruct((t // tm, heads, tm), F32)],
        scratch_shapes=scratch,
        compiler_params=_params(("arbitrary",), vmem),
        name="mixer_in",
    )(x, *weight_args)


def _attn_kernel(qaug_ref, kaug_ref, vaug_ref, crow_ref, *refs):
    n_cast = (len(refs) - 5) // 2
    cast_in, o_ref, cast_out = refs[:n_cast], refs[n_cast], refs[n_cast + 1:2 * n_cast + 1]
    for src_ref, dst_ref in zip(cast_in, cast_out):
        dst_ref[...] = src_ref[...].astype(BF16)
    _attn_body(qaug_ref, kaug_ref, vaug_ref, crow_ref, o_ref, *refs[2 * n_cast + 1:])


def _attn_body(qaug_ref, kaug_ref, vaug_ref, crow_ref, o_ref, s0_ref, s1_ref, m_ref, acc_ref):
    tq = qaug_ref.shape[2]
    heads = m_ref.shape[0]
    g = pl.program_id(1)
    ia, ib = 2 * g, 2 * g + 1
    head_cols = [slice(h * HEAD_DIM, (h + 1) * HEAD_DIM) for h in range(heads)]
    qk_cols = [slice(h * 2 * HEAD_DIM, (h + 1) * 2 * HEAD_DIM) for h in range(heads)]
    v_rows = [slice(h * (HEAD_DIM + ONES_ROWS), (h + 1) * (HEAD_DIM + ONES_ROWS)) for h in range(heads)]
    key_pos = lax.broadcasted_iota(jnp.int32, (tq, tq), 0)
    query_pos = lax.broadcasted_iota(jnp.int32, (tq, tq), 1)
    causal = key_pos <= query_pos

    s_refs = (s0_ref, s1_ref)

    def reset():
        m_ref[...] = jnp.full(m_ref.shape, MASKED, F32)
        acc_ref[...] = jnp.zeros(acc_ref.shape, F32)

    def logits(qb, j, slot, diagonal):
        rows = pl.ds(pl.multiple_of(j * tq, tq), tq)
        for h in range(heads):
            s = jnp.dot(kaug_ref[rows, qk_cols[h]], qaug_ref[qb, qk_cols[h], :], preferred_element_type=F32)
            s_refs[slot][h] = jnp.where(causal, s, MASKED) if diagonal else s

    def softmax_pv(i, j, slot):
        probs, rescale = [], []
        for h in range(heads):
            s = s_refs[slot][h]
            cq = crow_ref[i, h:h + 1, :]
            m_old = m_ref[h]
            block_max = jnp.max(functools.reduce(jnp.maximum, _row_groups(s)), axis=0, keepdims=True)
            m_new = jnp.maximum(m_old, block_max + cq)
            m_ref[h] = m_new
            probs.append(jnp.exp2(s + (cq - m_new)).astype(BF16))
            rescale.append(jnp.exp2(m_old - m_new))
        for h in range(heads):
            acc_ref[h] = rescale[h] * acc_ref[h] + jnp.dot(vaug_ref[j, v_rows[h], :], probs[h],
                                                           preferred_element_type=F32)

    def finish(qb):
        for h, hs in enumerate(head_cols):
            row_sum = acc_ref[h, HEAD_DIM:HEAD_DIM + 1, :]
            o_ref[qb * tq:(qb + 1) * tq, hs] = (acc_ref[h, 0:HEAD_DIM, :] * (1.0 / row_sum)).T.astype(BF16)

    def stage(qb_next, j_next, slot_next, diagonal, i, j, slot):
        logits(qb_next, j_next, slot_next, diagonal)
        softmax_pv(i, j, slot)

    reset()

    @pl.when(g == 0)
    def _():
        logits(0, 0, 0, True)
        stage(1, 0, 1, False, ia, 0, 0)
        finish(0)
        reset()
        stage(1, 1, 0, True, ib, 0, 1)
        softmax_pv(ib, 1, 0)
        finish(1)

    @pl.when(g > 0)
    def _():
        logits(0, 0, 0, False)

        def pair_a(t, carry):
            j = 2 * t
            stage(0, j + 1, 1, False, ia, j, 0)
            stage(0, j + 2, 0, False, ia, j + 1, 1)
            return carry

        lax.fori_loop(0, g - 1, pair_a, 0)
        stage(0, ia - 1, 1, False, ia, ia - 2, 0)
        stage(0, ia, 0, True, ia, ia - 1, 1)
        stage(1, 0, 1, False, ia, ia, 0)
        finish(0)
        reset()

        def pair_b(t, carry):
            j = 2 * t
            stage(1, j + 1, 0, False, ib, j, 1)
            stage(1, j + 2, 1, False, ib, j + 1, 0)
            return carry

        lax.fori_loop(0, g, pair_b, 0)
        stage(1, ib, 0, True, ib, ia, 1)
        softmax_pv(ib, ib, 0)
        finish(1)


def _attention(qaug, kaug, vaug, c_rows, next_weights, layer, *, batch, seq_len):
    nt, qk_rows, tq = qaug.shape
    nq = seq_len // tq
    heads = qk_rows // (2 * HEAD_DIM)
    d = heads * HEAD_DIM
    v_rows = vaug.shape[1]
    assert nt == batch * nq and tq % V7X_LANES == 0 and v_rows == heads * (HEAD_DIM + ONES_ROWS)
    assert nq % 2 == 0
    steps = nq // 2
    scratch = [pltpu.VMEM((heads, tq, tq), F32), pltpu.VMEM((heads, tq, tq), F32),
               pltpu.VMEM((heads, 1, tq), F32),
               pltpu.VMEM((heads, HEAD_DIM + ONES_ROWS, tq), F32)]
    vmem = (2 * (_nbytes((seq_len, qk_rows), BF16) + _nbytes((nq, v_rows, tq), BF16)
                 + 2 * _nbytes((qk_rows, tq), BF16) + 2 * _nbytes((tq, d), BF16)
                 + _nbytes((nq, V7X_SUBLANES, tq), F32))
            + sum(_nbytes(s.shape, s.dtype) for s in scratch)
            + _nbytes((V7X_SUBLANES * heads, tq), F32)
            + 16 * _nbytes((tq, tq), F32))
    n_chunks = math.gcd(batch * steps, LOAD_STEPS)
    stride = batch * steps // n_chunks
    chunk = lambda b, g: (b * steps + g) // stride
    cast_in, cast_out, cast_shapes = [], [], []
    for w in next_weights:
        rows = w.shape[1] // n_chunks
        assert rows * n_chunks == w.shape[1] and rows % BF16_SUBLANES == 0, (w.shape, n_chunks)
        cast_in.append(pl.BlockSpec((None, rows, w.shape[2]), lambda b, g: (layer, chunk(b, g), 0)))
        cast_out.append(pl.BlockSpec((rows, w.shape[2]), lambda b, g: (chunk(b, g), 0)))
        cast_shapes.append(jax.ShapeDtypeStruct(w.shape[1:], BF16))
        vmem += 2 * (_nbytes((rows, w.shape[2]), F32) + _nbytes((rows, w.shape[2]), BF16))
    out, *weights_bf16 = pl.pallas_call(
        _attn_kernel,
        grid=(batch, steps),
        in_specs=[pl.BlockSpec((2, qk_rows, tq), lambda b, g: (b * steps + g, 0, 0)),
                  pl.BlockSpec((seq_len, qk_rows), lambda b, g: (b, 0)),
                  pl.BlockSpec((nq, v_rows, tq), lambda b, g: (b, 0, 0)),
                  pl.BlockSpec((nq, heads, tq), lambda b, g: (b, 0, 0))] + cast_in,
        out_specs=[pl.BlockSpec((2 * tq, d), lambda b, g: (b * steps + g, 0))] + cast_out,
        out_shape=[jax.ShapeDtypeStruct((nt * tq, d), BF16)] + cast_shapes,
        scratch_shapes=scratch,
        compiler_params=_params(("arbitrary", "arbitrary"), vmem),
        name="attention",
    )(qaug, kaug, vaug, c_rows, *next_weights)
    return out, weights_bf16


def _mixer_out_kernel(x_ref, a_ref, part_ref, g2_ref, g_ref, woa_ref, wo_ref, wgu_ref, wd_ref, o_ref, act_ref):
    yc = jnp.dot(a_ref[...], woa_ref[...], preferred_element_type=F32)
    merged = part_ref[...] + g2_ref[...] * yc
    x = x_ref[...] + jnp.dot(merged.astype(BF16), wo_ref[...], preferred_element_type=F32)
    o_ref[...] = _swiglu_half_step(x, g_ref, wgu_ref, wd_ref, act_ref)


def _mixer_out(x, attn, part, g2, norm, weights_bf16, layer):
    t, d = x.shape
    d_ff = weights_bf16[3].shape[0]
    tm = TOKEN_TILE_FFN
    assert t % tm == 0
    tile = pl.BlockSpec((tm, d), lambda i: (i, 0))
    whole = lambda w: pl.BlockSpec(w.shape, lambda i: (0, 0), pipeline_mode=pl.Buffered(1))
    vmem = (sum(_nbytes(w.shape, w.dtype) for w in weights_bf16)
            + _nbytes((tm, d_ff), BF16) + 4 * _nbytes((tm, FF_CHUNK), F32)
            + 2 * (4 * _nbytes((tm, d), F32) + _nbytes((tm, d), BF16)) + 2 * _nbytes((tm, d), F32))
    return pl.pallas_call(
        _mixer_out_kernel,
        grid=(t // tm,),
        in_specs=[tile, tile, tile, tile, _layer(norm, layer)] + [whole(w) for w in weights_bf16],
        out_specs=tile,
        out_shape=jax.ShapeDtypeStruct((t, d), F32),
        scratch_shapes=[pltpu.VMEM((tm, d_ff), BF16)],
        compiler_params=_params(("arbitrary",), vmem),
        name="mixer_out",
    )(x, attn, part, g2, norm, *weights_bf16)


def kernel(x, ffn1_norm, ffn1_w_gu, ffn1_w_down, mix_norm, w_in, b_forget, b_gate, conv_w, sgu_ln_g, sgu_ln_b,
           sgu_w, sgu_b, q_norm_g, k_norm_g, w_out_conv, w_out_sgu, w_out_attn, w_o, ffn2_norm, ffn2_w_gu,
           ffn2_w_down):
    batch, seq_len, d = x.shape
    depth = w_in.shape[0]
    heads = d // HEAD_DIM
    rows = lambda a: a.reshape(depth, 1, -1)

    ffn1 = {"norm": rows(ffn1_norm), "w_gu": ffn1_w_gu, "w_down": ffn1_w_down}
    mix = {
        "mix_norm": rows(mix_norm),
        "w_in_t": jnp.swapaxes(w_in, 1, 2),
        "b_forget": jnp.pad(rows(b_forget), ((0, 0), (0, 0), (0, V7X_LANES - heads))),
        "b_gate": rows(b_gate),
        "conv_w": conv_w,
        "sgu_ln_g": rows(sgu_ln_g),
        "sgu_ln_b": rows(sgu_ln_b),
        "sgu_w": sgu_w,
        "sgu_b_t": jnp.swapaxes(sgu_b, 1, 2),
        "q_norm_g": rows(q_norm_g),
        "k_norm_g": rows(k_norm_g),
        "w_out_conv": w_out_conv,
        "w_out_sgu": w_out_sgu,
    }
    out_weights_f32 = [w_out_attn, w_o, ffn2_w_gu, ffn2_w_down]

    xt = x.reshape(batch * seq_len, d)
    for layer in range(depth):
        xt = _ffn(xt, ffn1, layer)
        part, g2, qaug, kaug, vaug, c_rows = _mixer_in(xt, mix, layer, seq_len=seq_len)
        attn, out_weights = _attention(qaug, kaug, vaug, c_rows, out_weights_f32, layer,
                                       batch=batch, seq_len=seq_len)
        xt = _mixer_out(xt, attn, part, g2, rows(ffn2_norm), out_weights, layer)
    return xt.reshape(batch, seq_len, d)
```

```python
import functools
import math

import jax
import jax.numpy as jnp
from jax import lax
from jax.experimental import pallas as pl
from jax.experimental.pallas import tpu as pltpu

F32 = jnp.float32
BF16 = jnp.bfloat16

RMS_EPS = 1e-6
LN_EPS = 1e-5
SGU_CHUNK = 128
HEAD_DIM = 128
N_BRANCH = 3
MASKED = -1e30
FORGET_SPLIT = 3
ONES_ROWS = 16

V7X_LANES = 128
V7X_SUBLANES = 8
BF16_SUBLANES = 16
V7X_VMEM_BYTES = 64 * 1024 * 1024
LOAD_STEPS = 16
FF_CHUNK = 1024
GATE_CHUNK = 256

TOKEN_TILE_FFN = 512
SEQ_BLOCK = 256


def _layer(arr, layer, cols=None, col_block=0):
    block = (None,) + tuple(arr.shape[1:-1]) + (arr.shape[-1] if cols is None else cols,)
    index = (layer,) + (0,) * (arr.ndim - 2) + (col_block,)
    return pl.BlockSpec(block, lambda *_: index, pipeline_mode=pl.Buffered(1))


def _weight_chunk(arr, layer, n_load, layer_rows=None):
    rows = (arr.shape[1] if layer_rows is None else layer_rows) // n_load
    assert rows % BF16_SUBLANES == 0, (arr.shape, n_load)
    if layer_rows is None:
        assert rows * n_load == arr.shape[1]
        return pl.BlockSpec((None, rows, arr.shape[2]), lambda s: (layer, jnp.minimum(s, n_load - 1), 0))
    assert rows * n_load == layer_rows and arr.shape[0] % layer_rows == 0
    return pl.BlockSpec((rows, arr.shape[1]), lambda s: (layer * n_load + jnp.minimum(s, n_load - 1), 0))


def _token_tile(block, n_load):
    return pl.BlockSpec(block, lambda s: (jnp.maximum(s - n_load, 0),) + (0,) * (len(block) - 1))


def _stash_rows(src_ref, dst_ref, step, cols=None):
    rows = src_ref.shape[0]
    chunk = src_ref[...] if cols is None else src_ref[:, cols]
    dst_ref[pl.ds(pl.multiple_of(step * rows, rows), rows), :] = chunk.astype(BF16)


def _nbytes(shape, dtype):
    return math.prod(shape) * jnp.dtype(dtype).itemsize


def _params(semantics, vmem_bytes):
    assert vmem_bytes <= V7X_VMEM_BYTES, vmem_bytes
    return pltpu.CompilerParams(dimension_semantics=semantics, vmem_limit_bytes=int(vmem_bytes))


def _rms_norm(x, g):
    return x * lax.rsqrt(jnp.mean(x * x, axis=-1, keepdims=True) + RMS_EPS) * g


def _gelu(x):
    return 0.5 * x * (1.0 + lax.erf(x * (2.0 ** -0.5)))


def _ff_chunks(d_ff):
    return [(c, min(c + FF_CHUNK, d_ff)) for c in range(0, d_ff, FF_CHUNK)]


def _row_groups(x):
    return [x[r:r + V7X_SUBLANES] for r in range(0, x.shape[0], V7X_SUBLANES)]


def _swiglu_half_step(x, g_ref, wgu_ref, wd_ref, act_ref):
    d_ff = wd_ref.shape[0]
    h = _rms_norm(x, g_ref[...]).astype(BF16)
    for c0, c1 in _ff_chunks(d_ff):
        g = jnp.dot(h, wgu_ref[:, c0:c1], preferred_element_type=F32)
        u = jnp.dot(h, wgu_ref[:, d_ff + c0:d_ff + c1], preferred_element_type=F32)
        act_ref[:, c0:c1] = (g * jax.nn.sigmoid(g) * u).astype(BF16)
    return x + 0.5 * jnp.dot(act_ref[...], wd_ref[...], preferred_element_type=F32)


def _ffn_weight_specs(w, layer, n_load):
    return [_layer(w["norm"], layer), _weight_chunk(w["w_gu"], layer, n_load),
            _weight_chunk(w["w_down"], layer, n_load)]


def _ffn_weight_args(w):
    return [w["norm"], w["w_gu"], w["w_down"]]


def _ffn_scratch(tm, w):
    d, d_gu = w["w_gu"].shape[1:]
    d_ff = w["w_down"].shape[1]
    return [pltpu.VMEM((d, d_gu), BF16), pltpu.VMEM((d_ff, d), BF16), pltpu.VMEM((tm, d_ff), BF16)]


def _ffn_vmem(tm, d, d_ff, n_load):
    return (_nbytes((d, 2 * d_ff), BF16) + _nbytes((d_ff, d), BF16)
            + 2 * (_nbytes((d, 2 * d_ff), F32) + _nbytes((d_ff, d), F32)) // n_load
            + _nbytes((tm, d_ff), BF16)
            + 4 * _nbytes((tm, FF_CHUNK), F32))


def _ffn_kernel(x_ref, g_ref, wgu32_ref, wd32_ref, *refs, n_load):
    n_cast = (len(refs) - 4) // 2
    cast_in, o_ref, cast_out = refs[:n_cast], refs[n_cast], refs[n_cast + 1:2 * n_cast + 1]
    wgu_ref, wd_ref, act_ref = refs[2 * n_cast + 1:]
    step = pl.program_id(0)

    @pl.when(step < n_load)
    def _():
        _stash_rows(wgu32_ref, wgu_ref, step)
        _stash_rows(wd32_ref, wd_ref, step)

    @pl.when(step >= n_load)
    def _():
        o_ref[...] = _swiglu_half_step(x_ref[...], g_ref, wgu_ref, wd_ref, act_ref)
        for src_ref, dst_ref in zip(cast_in, cast_out):
            dst_ref[...] = src_ref[...].astype(BF16)


def _ffn(x, w, layer, later_weights):
    t, d = x.shape
    d_ff = w["w_down"].shape[1]
    tm = TOKEN_TILE_FFN
    n_load = LOAD_STEPS
    assert t % tm == 0
    tile = _token_tile((tm, d), n_load)
    vmem = _ffn_vmem(tm, d, d_ff, n_load) + 4 * _nbytes((tm, d), F32)
    n_chunks = math.gcd(t // tm, LOAD_STEPS)
    stride = t // tm // n_chunks
    chunk = lambda s: jnp.clip((s - n_load) // stride, 0, n_chunks - 1)
    cast_in, cast_out, cast_shapes = [], [], []
    for lw in later_weights:
        rows = lw.shape[1] // n_chunks
        assert rows * n_chunks == lw.shape[1] and rows % BF16_SUBLANES == 0, (lw.shape, n_chunks)
        cast_in.append(pl.BlockSpec((None, rows, lw.shape[2]), lambda s: (layer, chunk(s), 0)))
        cast_out.append(pl.BlockSpec((rows, lw.shape[2]), lambda s: (chunk(s), 0)))
        cast_shapes.append(jax.ShapeDtypeStruct(lw.shape[1:], BF16))
        vmem += 2 * (_nbytes((rows, lw.shape[2]), F32) + _nbytes((rows, lw.shape[2]), BF16))
    out, *weights_bf16 = pl.pallas_call(
        functools.partial(_ffn_kernel, n_load=n_load),
        grid=(n_load + t // tm,),
        in_specs=[tile] + _ffn_weight_specs(w, layer, n_load) + cast_in,
        out_specs=[tile] + cast_out,
        out_shape=[jax.ShapeDtypeStruct((t, d), F32)] + cast_shapes,
        scratch_shapes=_ffn_scratch(tm, w),
        compiler_params=_params(("arbitrary",), vmem),
        name="ffn",
    )(x, *_ffn_weight_args(w), *later_weights)
    return out, weights_bf16


def _mixer_in_kernel(x_ref, ng_ref, wtm32_ref, wtf32_ref, wtg32_ref, woc32_ref, wos32_ref, bf_ref, bg_ref, cw_ref,
                     lng_ref, lnb_ref, sw_ref, sb_ref, qg_ref, kg_ref,
                     part_ref, g2_ref, qaug_ref, kaug_ref, vaug_ref, crow_ref,
                     wm_ref, wf_ref, wgt_ref, woc_ref, wos_ref, zs_ref, ccarry_ref, vn_ref, yb_ref, *,
                     n_load, tiles_per_seq, q_scale, c_scale):
    step = pl.program_id(0)

    @pl.when(step < n_load)
    def _():
        wm_ref[step] = wtm32_ref[...].T.astype(BF16)
        _stash_rows(woc32_ref, woc_ref, step)
        _stash_rows(wos32_ref, wos_ref, step)

    @pl.when(step < wgt_ref.shape[0])
    def _():
        wgt_ref[step] = wtg32_ref[...].T.astype(BF16)

    @pl.when(step == 0)
    def _():
        rows = wtf32_ref[...]
        slab = jnp.concatenate([rows, jnp.zeros((V7X_LANES - rows.shape[0], rows.shape[1]), F32)], axis=0)
        wf_ref[...] = slab.T.astype(BF16)

    @pl.when(step >= n_load)
    def _():
        _mixer_in_tile(step - n_load, x_ref, ng_ref, wm_ref, wf_ref, wgt_ref, bf_ref, bg_ref, cw_ref, lng_ref,
                       lnb_ref, sw_ref, sb_ref, qg_ref, kg_ref, woc_ref, wos_ref,
                       part_ref, g2_ref, qaug_ref, kaug_ref, vaug_ref, crow_ref,
                       zs_ref, ccarry_ref, vn_ref, yb_ref,
                       tiles_per_seq=tiles_per_seq, q_scale=q_scale, c_scale=c_scale)


def _mixer_in_tile(tile_idx, x_ref, ng_ref, wm_ref, wf_ref, wgt_ref, bf_ref, bg_ref, cw_ref, lng_ref, lnb_ref,
                   sw_ref, sb_ref, qg_ref, kg_ref, woc_ref, wos_ref,
                   part_ref, g2_ref, qaug_ref, kaug_ref, vaug_ref, crow_ref,
                   zs_ref, ccarry_ref, vn_ref, yb_ref, *, tiles_per_seq, q_scale, c_scale):
    tm, d = x_ref.shape
    heads = d // HEAD_DIM
    pad = V7X_SUBLANES

    @pl.when(tile_idx % tiles_per_seq == 0)
    def _():
        zs_ref[0:pad, :] = jnp.zeros((pad, d), F32)
        ccarry_ref[...] = jnp.zeros_like(ccarry_ref)

    h = _rms_norm(x_ref[...], ng_ref[...]).astype(BF16)

    def proj(w_ref, c0, c1):
        width = w_ref.shape[2]
        assert c0 % width == 0 and c1 % width == 0
        return jnp.concatenate([jnp.dot(h, w_ref[j], preferred_element_type=F32)
                                for j in range(c0 // width, c1 // width)], axis=1)

    pa = proj(wm_ref, 0, 3 * d)
    f = jnp.dot(h, wf_ref[...], preferred_element_type=F32) + bf_ref[...]
    ps = proj(wm_ref, 3 * d, 5 * d)
    pq = proj(wm_ref, 5 * d, 8 * d)
    gates = [jax.nn.sigmoid(proj(wgt_ref, b * d, (b + 1) * d) + bg_ref[:, b * d:(b + 1) * d])
             for b in range(N_BRANCH)]

    zs_ref[pad:pad + tm, :] = pa[:, d:2 * d] * pa[:, 2 * d:3 * d]
    conv = (cw_ref[0:1, :] * zs_ref[pad - 2:pad - 2 + tm, :]
            + cw_ref[1:2, :] * zs_ref[pad - 1:pad - 1 + tm, :]
            + cw_ref[2:3, :] * zs_ref[pad:pad + tm, :])
    ya_in = (pa[:, 0:d] * conv).astype(BF16)
    zs_ref[0:pad, :] = zs_ref[tm:tm + pad, :]

    c = jnp.minimum(f, 0.0) - jnp.log1p(jnp.exp(-jnp.abs(f)))
    t_idx = lax.broadcasted_iota(jnp.int32, c.shape, 0)
    shift = 1
    while shift < tm:
        c = c + jnp.where(t_idx >= shift, pltpu.roll(c, shift, axis=0), 0.0)
        shift *= 2
    c = c + ccarry_ref[0:1, :]
    ccarry_ref[0:1, :] = c[tm - 1:tm, :]
    c = c * c_scale
    crow_ref[...] = c.T[0:heads, :]

    u = _gelu(ps[:, 0:d])
    vv = _gelu(ps[:, d:2 * d])
    mu = jnp.mean(vv, axis=-1, keepdims=True)
    vc = vv - mu
    var = jnp.mean(vc * vc, axis=-1, keepdims=True)
    vn_ref[...] = (vc * lax.rsqrt(var + LN_EPS) * lng_ref[...] + lnb_ref[...]).astype(BF16)

    lane = lax.broadcasted_iota(jnp.int32, (tm, HEAD_DIM), 1)
    feature = lax.broadcasted_iota(jnp.int32, (HEAD_DIM, tm), 0)
    ones_rows = jnp.where(feature < FORGET_SPLIT, 1.0, 0.0).astype(BF16)
    qn = []
    for g in range(heads):
        hs = slice(g * HEAD_DIM, (g + 1) * HEAD_DIM)
        qn.append(_rms_norm(pq[:, hs], qg_ref[:, hs]) * q_scale)
        ks = slice(d + g * HEAD_DIM, d + (g + 1) * HEAD_DIM)
        kaug_ref[:, 2 * g * HEAD_DIM:(2 * g + 1) * HEAD_DIM] = _rms_norm(pq[:, ks], kg_ref[:, hs]).astype(BF16)
        rest = -jnp.broadcast_to(c[:, g:g + 1], (tm, HEAD_DIM))
        slab = jnp.zeros((tm, HEAD_DIM), F32)
        for term in range(FORGET_SPLIT):
            piece = rest.astype(BF16).astype(F32)
            slab = jnp.where(lane == term, piece, slab)
            rest = rest - piece
        kaug_ref[:, (2 * g + 1) * HEAD_DIM:(2 * g + 2) * HEAD_DIM] = slab.astype(BF16)
    qt = jnp.concatenate(qn, axis=1).T.astype(BF16)
    vt = pq[:, 2 * d:3 * d].T.astype(BF16)
    v_rows = HEAD_DIM + ONES_ROWS
    for g in range(heads):
        hs = slice(g * HEAD_DIM, (g + 1) * HEAD_DIM)
        qaug_ref[2 * g * HEAD_DIM:(2 * g + 1) * HEAD_DIM, :] = qt[hs, :]
        qaug_ref[(2 * g + 1) * HEAD_DIM:(2 * g + 2) * HEAD_DIM, :] = ones_rows
        vaug_ref[g * v_rows:g * v_rows + HEAD_DIM, :] = vt[hs, :]
        vaug_ref[g * v_rows + HEAD_DIM:(g + 1) * v_rows, :] = jnp.ones((ONES_ROWS, tm), BF16)

    ya = jnp.dot(ya_in, woc_ref[...], preferred_element_type=F32)
    n_chunks = tm // SGU_CHUNK
    pos_t = lax.broadcasted_iota(jnp.int32, (SGU_CHUNK, SGU_CHUNK), 0)
    pos_s = lax.broadcasted_iota(jnp.int32, (SGU_CHUNK, SGU_CHUNK), 1)
    for g in range(heads):
        hs = slice(g * HEAD_DIM, (g + 1) * HEAD_DIM)
        w = jnp.where(pos_s <= pos_t, sw_ref[g], 0.0).astype(BF16)
        rhs = jnp.concatenate([vn_ref[c0 * SGU_CHUNK:(c0 + 1) * SGU_CHUNK, hs] for c0 in range(n_chunks)], axis=1)
        s = jnp.dot(w, rhs, preferred_element_type=F32) + sb_ref[:, g:g + 1]
        for c0 in range(n_chunks):
            rows = slice(c0 * SGU_CHUNK, (c0 + 1) * SGU_CHUNK)
            yb_ref[rows, hs] = (u[rows, hs] * s[:, c0 * SGU_CHUNK:(c0 + 1) * SGU_CHUNK]).astype(BF16)
    yb = jnp.dot(yb_ref[...], wos_ref[...], preferred_element_type=F32)
    part_ref[...] = gates[0] * ya + gates[1] * yb
    g2_ref[...] = gates[2]


def _mixer_in(x, w, layer, *, seq_len):
    t, d = x.shape
    tm = SEQ_BLOCK
    heads = d // HEAD_DIM
    assert seq_len % tm == 0 and tm % SGU_CHUNK == 0 and d % HEAD_DIM == 0 and heads <= V7X_SUBLANES
    n_load = LOAD_STEPS
    w_t = w["w_in_t"]
    n_in = w_t.shape[1]
    n_main = 8 * d
    main_rows = n_main // n_load
    n_gate = N_BRANCH * d // GATE_CHUNK
    assert n_in == n_main + heads + N_BRANCH * d and main_rows * n_load == n_main and main_rows % V7X_LANES == 0
    assert n_gate * GATE_CHUNK == N_BRANCH * d and n_gate <= n_load and d % GATE_CHUNK == 0
    kernel = functools.partial(
        _mixer_in_kernel, n_load=n_load, tiles_per_seq=seq_len // tm,
        q_scale=HEAD_DIM ** -0.5 * math.log2(math.e), c_scale=math.log2(math.e))
    main_spec = pl.BlockSpec((None, main_rows, d), lambda s: (layer, jnp.minimum(s, n_load - 1), 0))
    forget_spec = pl.BlockSpec((pl.Squeezed(), pl.Element(V7X_SUBLANES), pl.Element(d)),
                               lambda s: (layer, n_main, 0))
    gate_align = math.gcd(n_main + heads, GATE_CHUNK, V7X_SUBLANES)
    gate_spec = pl.BlockSpec(
        (pl.Squeezed(), pl.Element(GATE_CHUNK), pl.Element(d)),
        lambda s: (layer, pl.multiple_of(n_main + heads + GATE_CHUNK * jnp.minimum(s, n_gate - 1), gate_align), 0))
    streamed = ["w_out_conv", "w_out_sgu"]
    small = ["b_forget", "b_gate", "conv_w", "sgu_ln_g", "sgu_ln_b", "sgu_w", "sgu_b_t", "q_norm_g", "k_norm_g"]
    weight_specs = ([_layer(w["mix_norm"], layer), main_spec, forget_spec, gate_spec]
                    + [_weight_chunk(w[n], layer, n_load) for n in streamed] + [_layer(w[n], layer) for n in small])
    weight_args = [w["mix_norm"], w_t, w_t, w_t] + [w[n] for n in streamed] + [w[n] for n in small]
    v_rows = heads * (HEAD_DIM + ONES_ROWS)
    tile = _token_tile((tm, d), n_load)
    resident = [pltpu.VMEM((n_load, d, main_rows), BF16), pltpu.VMEM((d, V7X_LANES), BF16),
                pltpu.VMEM((n_gate, d, GATE_CHUNK), BF16), pltpu.VMEM((d, d), BF16), pltpu.VMEM((d, d), BF16)]
    scratch = resident + [pltpu.VMEM((tm + V7X_SUBLANES, d), F32), pltpu.VMEM((V7X_SUBLANES, V7X_LANES), F32),
                          pltpu.VMEM((tm, d), BF16), pltpu.VMEM((tm, d), BF16)]
    vmem = (sum(_nbytes(s.shape, s.dtype) for s in scratch)
            + 2 * _nbytes((main_rows + GATE_CHUNK + V7X_SUBLANES + 2 * d // n_load, d), F32)
            + 2 * _nbytes((main_rows, d), F32)
            + sum(_nbytes(w[n].shape[1:], w[n].dtype) for n in small)
            + 2 * (3 * _nbytes((tm, d), F32) + 4 * _nbytes((tm, d), BF16) + _nbytes((v_rows, tm), BF16)
                   + _nbytes((V7X_SUBLANES, tm), F32))
            + 10 * _nbytes((tm, d), F32))
    return pl.pallas_call(
        kernel,
        grid=(n_load + t // tm,),
        in_specs=[tile] + weight_specs,
        out_specs=[tile, tile,
                   _token_tile((None, 2 * d, tm), n_load),
                   _token_tile((tm, 2 * d), n_load),
                   _token_tile((None, v_rows, tm), n_load),
                   _token_tile((None, heads, tm), n_load)],
        out_shape=[jax.ShapeDtypeStruct((t, d), F32), jax.ShapeDtypeStruct((t, d), F32),
                   jax.ShapeDtypeStruct((t // tm, 2 * d, tm), BF16), jax.ShapeDtypeStruct((t, 2 * d), BF16),
                   jax.ShapeDtypeStruct((t // tm, v_rows, tm), BF16),
                   jax.ShapeDtypeStruct((t // tm, heads, tm), F32)],
        scratch_shapes=scratch,
        compiler_params=_params(("arbitrary",), vmem),
        name="mixer_in",
    )(x, *weight_args)


def _attn_kernel(qaug_ref, kaug_ref, vaug_ref, crow_ref, o_ref, s0_ref, s1_ref, m_ref, acc_ref):
    tq = qaug_ref.shape[2]
    heads = m_ref.shape[0]
    g = pl.program_id(1)
    ia, ib = 2 * g, 2 * g + 1
    head_cols = [slice(h * HEAD_DIM, (h + 1) * HEAD_DIM) for h in range(heads)]
    qk_cols = [slice(h * 2 * HEAD_DIM, (h + 1) * 2 * HEAD_DIM) for h in range(heads)]
    v_rows = [slice(h * (HEAD_DIM + ONES_ROWS), (h + 1) * (HEAD_DIM + ONES_ROWS)) for h in range(heads)]
    key_pos = lax.broadcasted_iota(jnp.int32, (tq, tq), 0)
    query_pos = lax.broadcasted_iota(jnp.int32, (tq, tq), 1)
    causal = key_pos <= query_pos

    s_refs = (s0_ref, s1_ref)

    def reset():
        m_ref[...] = jnp.full(m_ref.shape, MASKED, F32)
        acc_ref[...] = jnp.zeros(acc_ref.shape, F32)

    def logits(qb, j, slot, diagonal):
        rows = pl.ds(pl.multiple_of(j * tq, tq), tq)
        for h in range(heads):
            s = jnp.dot(kaug_ref[rows, qk_cols[h]], qaug_ref[qb, qk_cols[h], :], preferred_element_type=F32)
            s_refs[slot][h] = jnp.where(causal, s, MASKED) if diagonal else s

    def softmax_pv(i, j, slot):
        probs, rescale = [], []
        for h in range(heads):
            s = s_refs[slot][h]
            cq = crow_ref[i, h:h + 1, :]
            m_old = m_ref[h]
            block_max = jnp.max(functools.reduce(jnp.maximum, _row_groups(s)), axis=0, keepdims=True)
            m_new = jnp.maximum(m_old, block_max + cq)
            m_ref[h] = m_new
            probs.append(jnp.exp2(s + (cq - m_new)).astype(BF16))
            rescale.append(jnp.exp2(m_old - m_new))
        for h in range(heads):
            acc_ref[h] = rescale[h] * acc_ref[h] + jnp.dot(vaug_ref[j, v_rows[h], :], probs[h],
                                                           preferred_element_type=F32)

    def finish(qb):
        for h, hs in enumerate(head_cols):
            row_sum = acc_ref[h, HEAD_DIM:HEAD_DIM + 1, :]
            o_ref[qb * tq:(qb + 1) * tq, hs] = (acc_ref[h, 0:HEAD_DIM, :] * (1.0 / row_sum)).T.astype(BF16)

    def stage(qb_next, j_next, slot_next, diagonal, i, j, slot):
        logits(qb_next, j_next, slot_next, diagonal)
        softmax_pv(i, j, slot)

    reset()

    @pl.when(g == 0)
    def _():
        logits(0, 0, 0, True)
        stage(1, 0, 1, False, ia, 0, 0)
        finish(0)
        reset()
        stage(1, 1, 0, True, ib, 0, 1)
        softmax_pv(ib, 1, 0)
        finish(1)

    @pl.when(g > 0)
    def _():
        logits(0, 0, 0, False)

        def pair_a(t, carry):
            j = 2 * t
            stage(0, j + 1, 1, False, ia, j, 0)
            stage(0, j + 2, 0, False, ia, j + 1, 1)
            return carry

        lax.fori_loop(0, g - 1, pair_a, 0)
        stage(0, ia - 1, 1, False, ia, ia - 2, 0)
        stage(0, ia, 0, True, ia, ia - 1, 1)
        stage(1, 0, 1, False, ia, ia, 0)
        finish(0)
        reset()

        def pair_b(t, carry):
            j = 2 * t
            stage(1, j + 1, 0, False, ib, j, 1)
            stage(1, j + 2, 1, False, ib, j + 1, 0)
            return carry

        lax.fori_loop(0, g, pair_b, 0)
        stage(1, ib, 0, True, ib, ia, 1)
        softmax_pv(ib, ib, 0)
        finish(1)


def _attention(qaug, kaug, vaug, c_rows, *, batch, seq_len):
    nt, qk_rows, tq = qaug.shape
    nq = seq_len // tq
    heads = qk_rows // (2 * HEAD_DIM)
    d = heads * HEAD_DIM
    v_rows = vaug.shape[1]
    assert nt == batch * nq and tq % V7X_LANES == 0 and v_rows == heads * (HEAD_DIM + ONES_ROWS)
    assert nq % 2 == 0
    steps = nq // 2
    scratch = [pltpu.VMEM((heads, tq, tq), F32), pltpu.VMEM((heads, tq, tq), F32),
               pltpu.VMEM((heads, 1, tq), F32),
               pltpu.VMEM((heads, HEAD_DIM + ONES_ROWS, tq), F32)]
    vmem = (2 * (_nbytes((seq_len, qk_rows), BF16) + _nbytes((nq, v_rows, tq), BF16)
                 + 2 * _nbytes((qk_rows, tq), BF16) + 2 * _nbytes((tq, d), BF16)
                 + _nbytes((nq, V7X_SUBLANES, tq), F32))
            + sum(_nbytes(s.shape, s.dtype) for s in scratch)
            + _nbytes((V7X_SUBLANES * heads, tq), F32)
            + 16 * _nbytes((tq, tq), F32))
    return pl.pallas_call(
        _attn_kernel,
        grid=(batch, steps),
        in_specs=[pl.BlockSpec((2, qk_rows, tq), lambda b, g: (b * steps + g, 0, 0)),
                  pl.BlockSpec((seq_len, qk_rows), lambda b, g: (b, 0)),
                  pl.BlockSpec((nq, v_rows, tq), lambda b, g: (b, 0, 0)),
                  pl.BlockSpec((nq, heads, tq), lambda b, g: (b, 0, 0))],
        out_specs=pl.BlockSpec((2 * tq, d), lambda b, g: (b * steps + g, 0)),
        out_shape=jax.ShapeDtypeStruct((nt * tq, d), BF16),
        scratch_shapes=scratch,
        compiler_params=_params(("arbitrary", "arbitrary"), vmem),
        name="attention",
    )(qaug, kaug, vaug, c_rows)


def _mixer_out_kernel(x_ref, a_ref, part_ref, g2_ref, g_ref, woa_ref, wo_ref, wgu_ref, wd_ref, o_ref, act_ref):
    yc = jnp.dot(a_ref[...], woa_ref[...], preferred_element_type=F32)
    merged = part_ref[...] + g2_ref[...] * yc
    x = x_ref[...] + jnp.dot(merged.astype(BF16), wo_ref[...], preferred_element_type=F32)
    o_ref[...] = _swiglu_half_step(x, g_ref, wgu_ref, wd_ref, act_ref)


def _mixer_out(x, attn, part, g2, norm, weights_bf16, layer):
    t, d = x.shape
    d_ff = weights_bf16[3].shape[0]
    tm = TOKEN_TILE_FFN
    assert t % tm == 0
    tile = pl.BlockSpec((tm, d), lambda i: (i, 0))
    whole = lambda w: pl.BlockSpec(w.shape, lambda i: (0, 0), pipeline_mode=pl.Buffered(1))
    vmem = (sum(_nbytes(w.shape, w.dtype) for w in weights_bf16)
            + _nbytes((tm, d_ff), BF16) + 4 * _nbytes((tm, FF_CHUNK), F32)
            + 2 * (4 * _nbytes((tm, d), F32) + _nbytes((tm, d), BF16)) + 2 * _nbytes((tm, d), F32))
    return pl.pallas_call(
        _mixer_out_kernel,
        grid=(t // tm,),
        in_specs=[tile, tile, tile, tile, _layer(norm, layer)] + [whole(w) for w in weights_bf16],
        out_specs=tile,
        out_shape=jax.ShapeDtypeStruct((t, d), F32),
        scratch_shapes=[pltpu.VMEM((tm, d_ff), BF16)],
        compiler_params=_params(("arbitrary",), vmem),
        name="mixer_out",
    )(x, attn, part, g2, norm, *weights_bf16)


def kernel(x, ffn1_norm, ffn1_w_gu, ffn1_w_down, mix_norm, w_in, b_forget, b_gate, conv_w, sgu_ln_g, sgu_ln_b,
           sgu_w, sgu_b, q_norm_g, k_norm_g, w_out_conv, w_out_sgu, w_out_attn, w_o, ffn2_norm, ffn2_w_gu,
           ffn2_w_down):
    batch, seq_len, d = x.shape
    depth = w_in.shape[0]
    heads = d // HEAD_DIM
    rows = lambda a: a.reshape(depth, 1, -1)

    ffn1 = {"norm": rows(ffn1_norm), "w_gu": ffn1_w_gu, "w_down": ffn1_w_down}
    mix = {
        "mix_norm": rows(mix_norm),
        "w_in_t": jnp.swapaxes(w_in, 1, 2),
        "b_forget": jnp.pad(rows(b_forget), ((0, 0), (0, 0), (0, V7X_LANES - heads))),
        "b_gate": rows(b_gate),
        "conv_w": conv_w,
        "sgu_ln_g": rows(sgu_ln_g),
        "sgu_ln_b": rows(sgu_ln_b),
        "sgu_w": sgu_w,
        "sgu_b_t": jnp.swapaxes(sgu_b, 1, 2),
        "q_norm_g": rows(q_norm_g),
        "k_norm_g": rows(k_norm_g),
        "w_out_conv": w_out_conv,
        "w_out_sgu": w_out_sgu,
    }
    out_weights_f32 = [w_out_attn, w_o, ffn2_w_gu, ffn2_w_down]

    xt = x.reshape(batch * seq_len, d)
    for layer in range(depth):
        xt, out_weights = _ffn(xt, ffn1, layer, out_weights_f32)
        part, g2, qaug, kaug, vaug, c_rows = _mixer_in(xt, mix, layer, seq_len=seq_len)
        attn = _attention(qaug, kaug, vaug, c_rows, batch=batch, seq_len=seq_len)
        xt = _mixer_out(xt, attn, part, g2, rows(ffn2_norm), out_weights, layer)
    return xt.reshape(batch, seq_len, d)
```

```python
import functools
import math

import jax
import jax.numpy as jnp
from jax import lax
from jax.experimental import pallas as pl
from jax.experimental.pallas import tpu as pltpu

F32 = jnp.float32
BF16 = jnp.bfloat16

RMS_EPS = 1e-6
LN_EPS = 1e-5
SGU_CHUNK = 128
HEAD_DIM = 128
N_BRANCH = 3
MASKED = -1e30
FORGET_SPLIT = 3
ONES_ROWS = 16

V7X_LANES = 128
V7X_SUBLANES = 8
BF16_SUBLANES = 16
V7X_VMEM_BYTES = 64 * 1024 * 1024
LOAD_STEPS = 16
FF_CHUNK = 1024
GATE_CHUNK = 256

TOKEN_TILE_FFN = 512
SEQ_BLOCK = 256


def _layer(arr, layer):
    index = (layer,) + (0,) * (arr.ndim - 1)
    return pl.BlockSpec((None,) + tuple(arr.shape[1:]), lambda *_: index, pipeline_mode=pl.Buffered(1))


def _weight_chunk(arr, layer, n_load):
    rows = arr.shape[1] // n_load
    assert rows * n_load == arr.shape[1] and rows % BF16_SUBLANES == 0, (arr.shape, n_load)
    return pl.BlockSpec((None, rows, arr.shape[2]), lambda s: (layer, jnp.minimum(s, n_load - 1), 0))


def _token_tile(block, n_load):
    return pl.BlockSpec(block, lambda s: (jnp.maximum(s - n_load, 0),) + (0,) * (len(block) - 1))


def _stash_rows(src_ref, dst_ref, step):
    rows = src_ref.shape[0]
    dst_ref[pl.ds(pl.multiple_of(step * rows, rows), rows), :] = src_ref[...].astype(BF16)


def _nbytes(shape, dtype):
    return math.prod(shape) * jnp.dtype(dtype).itemsize


def _params(semantics, vmem_bytes):
    assert vmem_bytes <= V7X_VMEM_BYTES, vmem_bytes
    return pltpu.CompilerParams(dimension_semantics=semantics, vmem_limit_bytes=int(vmem_bytes))


def _rms_norm(x, g):
    return x * lax.rsqrt(jnp.mean(x * x, axis=-1, keepdims=True) + RMS_EPS) * g


def _gelu(x):
    return 0.5 * x * (1.0 + lax.erf(x * (2.0 ** -0.5)))


def _ff_chunks(d_ff):
    return [(c, min(c + FF_CHUNK, d_ff)) for c in range(0, d_ff, FF_CHUNK)]


def _row_groups(x):
    return [x[r:r + V7X_SUBLANES] for r in range(0, x.shape[0], V7X_SUBLANES)]


def _swiglu_half_step(x, g_ref, wgu_ref, wd_ref, act_ref):
    d_ff = wd_ref.shape[0]
    h = _rms_norm(x, g_ref[...]).astype(BF16)
    for c0, c1 in _ff_chunks(d_ff):
        g = jnp.dot(h, wgu_ref[:, c0:c1], preferred_element_type=F32)
        u = jnp.dot(h, wgu_ref[:, d_ff + c0:d_ff + c1], preferred_element_type=F32)
        act_ref[:, c0:c1] = (g * jax.nn.sigmoid(g) * u).astype(BF16)
    return x + 0.5 * jnp.dot(act_ref[...], wd_ref[...], preferred_element_type=F32)


def _ffn_weight_specs(w, layer, n_load):
    return [_layer(w["norm"], layer), _weight_chunk(w["w_gu"], layer, n_load),
            _weight_chunk(w["w_down"], layer, n_load)]


def _ffn_weight_args(w):
    return [w["norm"], w["w_gu"], w["w_down"]]


def _ffn_scratch(tm, w):
    d, d_gu = w["w_gu"].shape[1:]
    d_ff = w["w_down"].shape[1]
    return [pltpu.VMEM((d, d_gu), BF16), pltpu.VMEM((d_ff, d), BF16), pltpu.VMEM((tm, d_ff), BF16)]


def _ffn_vmem(tm, d, d_ff, n_load):
    return (_nbytes((d, 2 * d_ff), BF16) + _nbytes((d_ff, d), BF16)
            + 2 * (_nbytes((d, 2 * d_ff), F32) + _nbytes((d_ff, d), F32)) // n_load
            + _nbytes((tm, d_ff), BF16)
            + 4 * _nbytes((tm, FF_CHUNK), F32))


def _ffn_kernel(x_ref, g_ref, wgu32_ref, wd32_ref, o_ref, wgu_ref, wd_ref, act_ref, *, n_load):
    step = pl.program_id(0)

    @pl.when(step < n_load)
    def _():
        _stash_rows(wgu32_ref, wgu_ref, step)
        _stash_rows(wd32_ref, wd_ref, step)

    @pl.when(step >= n_load)
    def _():
        o_ref[...] = _swiglu_half_step(x_ref[...], g_ref, wgu_ref, wd_ref, act_ref)


def _ffn(x, w, layer):
    t, d = x.shape
    d_ff = w["w_down"].shape[1]
    tm = TOKEN_TILE_FFN
    n_load = LOAD_STEPS
    assert t % tm == 0
    tile = _token_tile((tm, d), n_load)
    vmem = _ffn_vmem(tm, d, d_ff, n_load) + 4 * _nbytes((tm, d), F32)
    return pl.pallas_call(
        functools.partial(_ffn_kernel, n_load=n_load),
        grid=(n_load + t // tm,),
        in_specs=[tile] + _ffn_weight_specs(w, layer, n_load),
        out_specs=tile,
        out_shape=jax.ShapeDtypeStruct((t, d), F32),
        scratch_shapes=_ffn_scratch(tm, w),
        compiler_params=_params(("arbitrary",), vmem),
        name="ffn",
    )(x, *_ffn_weight_args(w))


def _mixer_in_kernel(x_ref, ng_ref, wtm32_ref, wtf32_ref, wtg32_ref, woc32_ref, wos32_ref, bf_ref, bg_ref, cw_ref,
                     lng_ref, lnb_ref, sw_ref, sb_ref, qg_ref, kg_ref,
                     part_ref, g2_ref, qaug_ref, kaug_ref, vaug_ref, crow_ref,
                     wm_ref, wf_ref, wgt_ref, woc_ref, wos_ref, zs_ref, ccarry_ref, vn_ref, yb_ref, *,
                     n_load, tiles_per_seq, q_scale, c_scale):
    step = pl.program_id(0)

    @pl.when(step < n_load)
    def _():
        wm_ref[step] = wtm32_ref[...].T.astype(BF16)
        _stash_rows(woc32_ref, woc_ref, step)
        _stash_rows(wos32_ref, wos_ref, step)

    @pl.when(step < wgt_ref.shape[0])
    def _():
        wgt_ref[step] = wtg32_ref[...].T.astype(BF16)

    @pl.when(step == 0)
    def _():
        rows = wtf32_ref[...]
        slab = jnp.concatenate([rows, jnp.zeros((V7X_LANES - rows.shape[0], rows.shape[1]), F32)], axis=0)
        wf_ref[...] = slab.T.astype(BF16)

    @pl.when(step >= n_load)
    def _():
        _mixer_in_tile(step - n_load, x_ref, ng_ref, wm_ref, wf_ref, wgt_ref, bf_ref, bg_ref, cw_ref, lng_ref,
                       lnb_ref, sw_ref, sb_ref, qg_ref, kg_ref, woc_ref, wos_ref,
                       part_ref, g2_ref, qaug_ref, kaug_ref, vaug_ref, crow_ref,
                       zs_ref, ccarry_ref, vn_ref, yb_ref,
                       tiles_per_seq=tiles_per_seq, q_scale=q_scale, c_scale=c_scale)


def _mixer_in_tile(tile_idx, x_ref, ng_ref, wm_ref, wf_ref, wgt_ref, bf_ref, bg_ref, cw_ref, lng_ref, lnb_ref,
                   sw_ref, sb_ref, qg_ref, kg_ref, woc_ref, wos_ref,
                   part_ref, g2_ref, qaug_ref, kaug_ref, vaug_ref, crow_ref,
                   zs_ref, ccarry_ref, vn_ref, yb_ref, *, tiles_per_seq, q_scale, c_scale):
    tm, d = x_ref.shape
    heads = d // HEAD_DIM
    pad = V7X_SUBLANES

    @pl.when(tile_idx % tiles_per_seq == 0)
    def _():
        zs_ref[0:pad, :] = jnp.zeros((pad, d), F32)
        ccarry_ref[...] = jnp.zeros_like(ccarry_ref)

    h = _rms_norm(x_ref[...], ng_ref[...]).astype(BF16)

    def proj(w_ref, c0, c1):
        width = w_ref.shape[2]
        assert c0 % width == 0 and c1 % width == 0
        return jnp.concatenate([jnp.dot(h, w_ref[j], preferred_element_type=F32)
                                for j in range(c0 // width, c1 // width)], axis=1)

    pa = proj(wm_ref, 0, 3 * d)
    f = jnp.dot(h, wf_ref[...], preferred_element_type=F32) + bf_ref[...]
    ps = proj(wm_ref, 3 * d, 5 * d)
    pq = proj(wm_ref, 5 * d, 8 * d)
    gates = [jax.nn.sigmoid(proj(wgt_ref, b * d, (b + 1) * d) + bg_ref[:, b * d:(b + 1) * d])
             for b in range(N_BRANCH)]

    zs_ref[pad:pad + tm, :] = pa[:, d:2 * d] * pa[:, 2 * d:3 * d]
    conv = (cw_ref[0:1, :] * zs_ref[pad - 2:pad - 2 + tm, :]
            + cw_ref[1:2, :] * zs_ref[pad - 1:pad - 1 + tm, :]
            + cw_ref[2:3, :] * zs_ref[pad:pad + tm, :])
    ya_in = (pa[:, 0:d] * conv).astype(BF16)
    zs_ref[0:pad, :] = zs_ref[tm:tm + pad, :]

    c = jnp.minimum(f, 0.0) - jnp.log1p(jnp.exp(-jnp.abs(f)))
    t_idx = lax.broadcasted_iota(jnp.int32, c.shape, 0)
    shift = 1
    while shift < tm:
        c = c + jnp.where(t_idx >= shift, pltpu.roll(c, shift, axis=0), 0.0)
        shift *= 2
    c = c + ccarry_ref[0:1, :]
    ccarry_ref[0:1, :] = c[tm - 1:tm, :]
    c = c * c_scale
    crow_ref[...] = c.T[0:heads, :]

    u = _gelu(ps[:, 0:d])
    vv = _gelu(ps[:, d:2 * d])
    mu = jnp.mean(vv, axis=-1, keepdims=True)
    vc = vv - mu
    var = jnp.mean(vc * vc, axis=-1, keepdims=True)
    vn_ref[...] = (vc * lax.rsqrt(var + LN_EPS) * lng_ref[...] + lnb_ref[...]).astype(BF16)

    lane = lax.broadcasted_iota(jnp.int32, (tm, HEAD_DIM), 1)
    feature = lax.broadcasted_iota(jnp.int32, (HEAD_DIM, tm), 0)
    ones_rows = jnp.where(feature < FORGET_SPLIT, 1.0, 0.0).astype(BF16)
    qn = []
    for g in range(heads):
        hs = slice(g * HEAD_DIM, (g + 1) * HEAD_DIM)
        qn.append(_rms_norm(pq[:, hs], qg_ref[:, hs]) * q_scale)
        ks = slice(d + g * HEAD_DIM, d + (g + 1) * HEAD_DIM)
        kaug_ref[:, 2 * g * HEAD_DIM:(2 * g + 1) * HEAD_DIM] = _rms_norm(pq[:, ks], kg_ref[:, hs]).astype(BF16)
        rest = -jnp.broadcast_to(c[:, g:g + 1], (tm, HEAD_DIM))
        slab = jnp.zeros((tm, HEAD_DIM), F32)
        for term in range(FORGET_SPLIT):
            piece = rest.astype(BF16).astype(F32)
            slab = jnp.where(lane == term, piece, slab)
            rest = rest - piece
        kaug_ref[:, (2 * g + 1) * HEAD_DIM:(2 * g + 2) * HEAD_DIM] = slab.astype(BF16)
    qt = jnp.concatenate(qn, axis=1).T.astype(BF16)
    vt = pq[:, 2 * d:3 * d].T.astype(BF16)
    v_rows = HEAD_DIM + ONES_ROWS
    for g in range(heads):
        hs = slice(g * HEAD_DIM, (g + 1) * HEAD_DIM)
        qaug_ref[2 * g * HEAD_DIM:(2 * g + 1) * HEAD_DIM, :] = qt[hs, :]
        qaug_ref[(2 * g + 1) * HEAD_DIM:(2 * g + 2) * HEAD_DIM, :] = ones_rows
        vaug_ref[g * v_rows:g * v_rows + HEAD_DIM, :] = vt[hs, :]
        vaug_ref[g * v_rows + HEAD_DIM:(g + 1) * v_rows, :] = jnp.ones((ONES_ROWS, tm), BF16)

    ya = jnp.dot(ya_in, woc_ref[...], preferred_element_type=F32)
    n_chunks = tm // SGU_CHUNK
    pos_t = lax.broadcasted_iota(jnp.int32, (SGU_CHUNK, SGU_CHUNK), 0)
    pos_s = lax.broadcasted_iota(jnp.int32, (SGU_CHUNK, SGU_CHUNK), 1)
    for g in range(heads):
        hs = slice(g * HEAD_DIM, (g + 1) * HEAD_DIM)
        w = jnp.where(pos_s <= pos_t, sw_ref[g], 0.0).astype(BF16)
        rhs = jnp.concatenate([vn_ref[c0 * SGU_CHUNK:(c0 + 1) * SGU_CHUNK, hs] for c0 in range(n_chunks)], axis=1)
        s = jnp.dot(w, rhs, preferred_element_type=F32) + sb_ref[:, g:g + 1]
        for c0 in range(n_chunks):
            rows = slice(c0 * SGU_CHUNK, (c0 + 1) * SGU_CHUNK)
            yb_ref[rows, hs] = (u[rows, hs] * s[:, c0 * SGU_CHUNK:(c0 + 1) * SGU_CHUNK]).astype(BF16)
    yb = jnp.dot(yb_ref[...], wos_ref[...], preferred_element_type=F32)
    part_ref[...] = gates[0] * ya + gates[1] * yb
    g2_ref[...] = gates[2]


def _mixer_in(x, w, layer, *, seq_len):
    t, d = x.shape
    tm = SEQ_BLOCK
    heads = d // HEAD_DIM
    assert seq_len % tm == 0 and tm % SGU_CHUNK == 0 and d % HEAD_DIM == 0 and heads <= V7X_SUBLANES
    n_load = LOAD_STEPS
    w_t = w["w_in_t"]
    n_in = w_t.shape[1]
    n_main = 8 * d
    main_rows = n_main // n_load
    n_gate = N_BRANCH * d // GATE_CHUNK
    assert n_in == n_main + heads + N_BRANCH * d and main_rows * n_load == n_main and main_rows % V7X_LANES == 0
    assert n_gate * GATE_CHUNK == N_BRANCH * d and n_gate <= n_load and d % GATE_CHUNK == 0
    kernel = functools.partial(
        _mixer_in_kernel, n_load=n_load, tiles_per_seq=seq_len // tm,
        q_scale=HEAD_DIM ** -0.5 * math.log2(math.e), c_scale=math.log2(math.e))
    main_spec = pl.BlockSpec((None, main_rows, d), lambda s: (layer, jnp.minimum(s, n_load - 1), 0))
    forget_spec = pl.BlockSpec((pl.Squeezed(), pl.Element(V7X_SUBLANES), pl.Element(d)),
                               lambda s: (layer, n_main, 0))
    gate_align = math.gcd(n_main + heads, GATE_CHUNK, V7X_SUBLANES)
    gate_spec = pl.BlockSpec(
        (pl.Squeezed(), pl.Element(GATE_CHUNK), pl.Element(d)),
        lambda s: (layer, pl.multiple_of(n_main + heads + GATE_CHUNK * jnp.minimum(s, n_gate - 1), gate_align), 0))
    streamed = ["w_out_conv", "w_out_sgu"]
    small = ["b_forget", "b_gate", "conv_w", "sgu_ln_g", "sgu_ln_b", "sgu_w", "sgu_b_t", "q_norm_g", "k_norm_g"]
    weight_specs = ([_layer(w["mix_norm"], layer), main_spec, forget_spec, gate_spec]
                    + [_weight_chunk(w[n], layer, n_load) for n in streamed] + [_layer(w[n], layer) for n in small])
    weight_args = [w["mix_norm"], w_t, w_t, w_t] + [w[n] for n in streamed] + [w[n] for n in small]
    v_rows = heads * (HEAD_DIM + ONES_ROWS)
    tile = _token_tile((tm, d), n_load)
    resident = [pltpu.VMEM((n_load, d, main_rows), BF16), pltpu.VMEM((d, V7X_LANES), BF16),
                pltpu.VMEM((n_gate, d, GATE_CHUNK), BF16), pltpu.VMEM((d, d), BF16), pltpu.VMEM((d, d), BF16)]
    scratch = resident + [pltpu.VMEM((tm + V7X_SUBLANES, d), F32), pltpu.VMEM((V7X_SUBLANES, V7X_LANES), F32),
                          pltpu.VMEM((tm, d), BF16), pltpu.VMEM((tm, d), BF16)]
    vmem = (sum(_nbytes(s.shape, s.dtype) for s in scratch)
            + 2 * _nbytes((main_rows + GATE_CHUNK + V7X_SUBLANES + 2 * d // n_load, d), F32)
            + 2 * _nbytes((main_rows, d), F32)
            + sum(_nbytes(w[n].shape[1:], w[n].dtype) for n in small)
            + 2 * (3 * _nbytes((tm, d), F32) + 4 * _nbytes((tm, d), BF16) + _nbytes((v_rows, tm), BF16)
                   + _nbytes((V7X_SUBLANES, tm), F32))
            + 10 * _nbytes((tm, d), F32))
    return pl.pallas_call(
        kernel,
        grid=(n_load + t // tm,),
        in_specs=[tile] + weight_specs,
        out_specs=[tile, tile,
                   _token_tile((None, 2 * d, tm), n_load),
                   _token_tile((tm, 2 * d), n_load),
                   _token_tile((None, v_rows, tm), n_load),
                   _token_tile((None, heads, tm), n_load)],
        out_shape=[jax.ShapeDtypeStruct((t, d), F32), jax.ShapeDtypeStruct((t, d), F32),
                   jax.ShapeDtypeStruct((t // tm, 2 * d, tm), BF16), jax.ShapeDtypeStruct((t, 2 * d), BF16),
                   jax.ShapeDtypeStruct((t // tm, v_rows, tm), BF16),
                   jax.ShapeDtypeStruct((t // tm, heads, tm), F32)],
        scratch_shapes=scratch,
        compiler_params=_params(("arbitrary",), vmem),
        name="mixer_in",
    )(x, *weight_args)


def _attn_kernel(qaug_ref, kaug_ref, vaug_ref, crow_ref, o_ref, s0_ref, s1_ref, m_ref, acc_ref):
    tq = qaug_ref.shape[2]
    heads = m_ref.shape[0]
    g = pl.program_id(1)
    ia, ib = 2 * g, 2 * g + 1
    head_cols = [slice(h * HEAD_DIM, (h + 1) * HEAD_DIM) for h in range(heads)]
    qk_cols = [slice(h * 2 * HEAD_DIM, (h + 1) * 2 * HEAD_DIM) for h in range(heads)]
    v_rows = [slice(h * (HEAD_DIM + ONES_ROWS), (h + 1) * (HEAD_DIM + ONES_ROWS)) for h in range(heads)]
    key_pos = lax.broadcasted_iota(jnp.int32, (tq, tq), 0)
    query_pos = lax.broadcasted_iota(jnp.int32, (tq, tq), 1)
    causal = key_pos <= query_pos

    s_refs = (s0_ref, s1_ref)

    def reset():
        m_ref[...] = jnp.full(m_ref.shape, MASKED, F32)
        acc_ref[...] = jnp.zeros(acc_ref.shape, F32)

    def logits(qb, j, slot, diagonal):
        rows = pl.ds(pl.multiple_of(j * tq, tq), tq)
        for h in range(heads):
            s = jnp.dot(kaug_ref[rows, qk_cols[h]], qaug_ref[qb, qk_cols[h], :], preferred_element_type=F32)
            s_refs[slot][h] = jnp.where(causal, s, MASKED) if diagonal else s

    def softmax_pv(i, j, slot):
        probs, rescale = [], []
        for h in range(heads):
            s = s_refs[slot][h]
            cq = crow_ref[i, h:h + 1, :]
            m_old = m_ref[h]
            block_max = jnp.max(functools.reduce(jnp.maximum, _row_groups(s)), axis=0, keepdims=True)
            m_new = jnp.maximum(m_old, block_max + cq)
            m_ref[h] = m_new
            probs.append(jnp.exp2(s + (cq - m_new)).astype(BF16))
            rescale.append(jnp.exp2(m_old - m_new))
        for h in range(heads):
            acc_ref[h] = rescale[h] * acc_ref[h] + jnp.dot(vaug_ref[j, v_rows[h], :], probs[h],
                                                           preferred_element_type=F32)

    def finish(qb):
        for h, hs in enumerate(head_cols):
            row_sum = acc_ref[h, HEAD_DIM:HEAD_DIM + 1, :]
            o_ref[qb * tq:(qb + 1) * tq, hs] = (acc_ref[h, 0:HEAD_DIM, :] * (1.0 / row_sum)).T.astype(BF16)

    def stage(qb_next, j_next, slot_next, diagonal, i, j, slot):
        logits(qb_next, j_next, slot_next, diagonal)
        softmax_pv(i, j, slot)

    reset()

    @pl.when(g == 0)
    def _():
        logits(0, 0, 0, True)
        stage(1, 0, 1, False, ia, 0, 0)
        finish(0)
        reset()
        stage(1, 1, 0, True, ib, 0, 1)
        softmax_pv(ib, 1, 0)
        finish(1)

    @pl.when(g > 0)
    def _():
        logits(0, 0, 0, False)

        def pair_a(t, carry):
            j = 2 * t
            stage(0, j + 1, 1, False, ia, j, 0)
            stage(0, j + 2, 0, False, ia, j + 1, 1)
            return carry

        lax.fori_loop(0, g - 1, pair_a, 0)
        stage(0, ia - 1, 1, False, ia, ia - 2, 0)
        stage(0, ia, 0, True, ia, ia - 1, 1)
        stage(1, 0, 1, False, ia, ia, 0)
        finish(0)
        reset()

        def pair_b(t, carry):
            j = 2 * t
            stage(1, j + 1, 0, False, ib, j, 1)
            stage(1, j + 2, 1, False, ib, j + 1, 0)
            return carry

        lax.fori_loop(0, g, pair_b, 0)
        stage(1, ib, 0, True, ib, ia, 1)
        softmax_pv(ib, ib, 0)
        finish(1)


def _attention(qaug, kaug, vaug, c_rows, *, batch, seq_len):
    nt, qk_rows, tq = qaug.shape
    nq = seq_len // tq
    heads = qk_rows // (2 * HEAD_DIM)
    d = heads * HEAD_DIM
    v_rows = vaug.shape[1]
    assert nt == batch * nq and tq % V7X_LANES == 0 and v_rows == heads * (HEAD_DIM + ONES_ROWS)
    assert nq % 2 == 0
    steps = nq // 2
    scratch = [pltpu.VMEM((heads, tq, tq), F32), pltpu.VMEM((heads, tq, tq), F32),
               pltpu.VMEM((heads, 1, tq), F32),
               pltpu.VMEM((heads, HEAD_DIM + ONES_ROWS, tq), F32)]
    vmem = (2 * (_nbytes((seq_len, qk_rows), BF16) + _nbytes((nq, v_rows, tq), BF16)
                 + 2 * _nbytes((qk_rows, tq), BF16) + 2 * _nbytes((tq, d), BF16)
                 + _nbytes((nq, V7X_SUBLANES, tq), F32))
            + sum(_nbytes(s.shape, s.dtype) for s in scratch)
            + _nbytes((V7X_SUBLANES * heads, tq), F32)
            + 16 * _nbytes((tq, tq), F32))
    return pl.pallas_call(
        _attn_kernel,
        grid=(batch, steps),
        in_specs=[pl.BlockSpec((2, qk_rows, tq), lambda b, g: (b * steps + g, 0, 0)),
                  pl.BlockSpec((seq_len, qk_rows), lambda b, g: (b, 0)),
                  pl.BlockSpec((nq, v_rows, tq), lambda b, g: (b, 0, 0)),
                  pl.BlockSpec((nq, heads, tq), lambda b, g: (b, 0, 0))],
        out_specs=pl.BlockSpec((2 * tq, d), lambda b, g: (b * steps + g, 0)),
        out_shape=jax.ShapeDtypeStruct((nt * tq, d), BF16),
        scratch_shapes=scratch,
        compiler_params=_params(("arbitrary", "arbitrary"), vmem),
        name="attention",
    )(qaug, kaug, vaug, c_rows)


def _mixer_out_kernel(x_ref, a_ref, part_ref, g2_ref, woa32_ref, wo32_ref, g_ref, wgu32_ref, wd32_ref,
                      o_ref, woa_ref, wo_ref, wgu_ref, wd_ref, act_ref, *, n_load):
    step = pl.program_id(0)

    @pl.when(step < n_load)
    def _():
        _stash_rows(woa32_ref, woa_ref, step)
        _stash_rows(wo32_ref, wo_ref, step)
        _stash_rows(wgu32_ref, wgu_ref, step)
        _stash_rows(wd32_ref, wd_ref, step)

    @pl.when(step >= n_load)
    def _():
        yc = jnp.dot(a_ref[...], woa_ref[...], preferred_element_type=F32)
        merged = part_ref[...] + g2_ref[...] * yc
        x = x_ref[...] + jnp.dot(merged.astype(BF16), wo_ref[...], preferred_element_type=F32)
        o_ref[...] = _swiglu_half_step(x, g_ref, wgu_ref, wd_ref, act_ref)


def _mixer_out(x, attn, part, g2, w_mix, w_ffn, layer):
    t, d = x.shape
    d_ff = w_ffn["w_down"].shape[1]
    tm = TOKEN_TILE_FFN
    n_load = LOAD_STEPS
    assert t % tm == 0
    tile = _token_tile((tm, d), n_load)
    vmem = (_ffn_vmem(tm, d, d_ff, n_load) + 2 * _nbytes((d, d), BF16) + 4 * _nbytes((d, d), F32) // n_load
            + 2 * (4 * _nbytes((tm, d), F32) + _nbytes((tm, d), BF16)) + 2 * _nbytes((tm, d), F32))
    return pl.pallas_call(
        functools.partial(_mixer_out_kernel, n_load=n_load),
        grid=(n_load + t // tm,),
        in_specs=[tile, tile, tile, tile,
                  _weight_chunk(w_mix["w_out_attn"], layer, n_load), _weight_chunk(w_mix["w_o"], layer, n_load)]
        + _ffn_weight_specs(w_ffn, layer, n_load),
        out_specs=tile,
        out_shape=jax.ShapeDtypeStruct((t, d), F32),
        scratch_shapes=[pltpu.VMEM((d, d), BF16), pltpu.VMEM((d, d), BF16)] + _ffn_scratch(tm, w_ffn),
        compiler_params=_params(("arbitrary",), vmem),
        name="mixer_out",
    )(x, attn, part, g2, w_mix["w_out_attn"], w_mix["w_o"], *_ffn_weight_args(w_ffn))


def kernel(x, ffn1_norm, ffn1_w_gu, ffn1_w_down, mix_norm, w_in, b_forget, b_gate, conv_w, sgu_ln_g, sgu_ln_b,
           sgu_w, sgu_b, q_norm_g, k_norm_g, w_out_conv, w_out_sgu, w_out_attn, w_o, ffn2_norm, ffn2_w_gu,
           ffn2_w_down):
    batch, seq_len, d = x.shape
    depth = w_in.shape[0]
    heads = d // HEAD_DIM
    rows = lambda a: a.reshape(depth, 1, -1)

    ffn1 = {"norm": rows(ffn1_norm), "w_gu": ffn1_w_gu, "w_down": ffn1_w_down}
    ffn2 = {"norm": rows(ffn2_norm), "w_gu": ffn2_w_gu, "w_down": ffn2_w_down}
    mix = {
        "mix_norm": rows(mix_norm),
        "w_in_t": jnp.swapaxes(w_in, 1, 2),
        "b_forget": jnp.pad(rows(b_forget), ((0, 0), (0, 0), (0, V7X_LANES - heads))),
        "b_gate": rows(b_gate),
        "conv_w": conv_w,
        "sgu_ln_g": rows(sgu_ln_g),
        "sgu_ln_b": rows(sgu_ln_b),
        "sgu_w": sgu_w,
        "sgu_b_t": jnp.swapaxes(sgu_b, 1, 2),
        "q_norm_g": rows(q_norm_g),
        "k_norm_g": rows(k_norm_g),
        "w_out_conv": w_out_conv,
        "w_out_sgu": w_out_sgu,
        "w_out_attn": w_out_attn,
        "w_o": w_o,
    }

    xt = x.reshape(batch * seq_len, d)
    for layer in range(depth):
        xt = _ffn(xt, ffn1, layer)
        part, g2, qaug, kaug, vaug, c_rows = _mixer_in(xt, mix, layer, seq_len=seq_len)
        attn = _attention(qaug, kaug, vaug, c_rows, batch=batch, seq_len=seq_len)
        xt = _mixer_out(xt, attn, part, g2, mix, ffn2, layer)
    return xt.reshape(batch, seq_len, d)
```

```python
import functools
import math

import jax
import jax.numpy as jnp
from jax import lax
from jax.experimental import pallas as pl
from jax.experimental.pallas import tpu as pltpu

F32 = jnp.float32
BF16 = jnp.bfloat16

RMS_EPS = 1e-6
LN_EPS = 1e-5
SGU_CHUNK = 128
HEAD_DIM = 128
N_BRANCH = 3
MASKED = -1e30
FORGET_SPLIT = 3
ONES_ROWS = 16

V7X_LANES = 128
V7X_SUBLANES = 8
BF16_SUBLANES = 16
V7X_VMEM_BYTES = 64 * 1024 * 1024
LOAD_STEPS = 16
FF_CHUNK = 1024
GATE_CHUNK = 256

TOKEN_TILE_FFN = 1024
TOKEN_TILE_OUT = 512
SEQ_BLOCK = 256


def _layer(arr, layer):
    index = (layer,) + (0,) * (arr.ndim - 1)
    return pl.BlockSpec((None,) + tuple(arr.shape[1:]), lambda *_: index, pipeline_mode=pl.Buffered(1))


def _weight_chunk(arr, layer, n_load):
    rows = arr.shape[1] // n_load
    assert rows * n_load == arr.shape[1] and rows % BF16_SUBLANES == 0, (arr.shape, n_load)
    return pl.BlockSpec((None, rows, arr.shape[2]), lambda s: (layer, jnp.minimum(s, n_load - 1), 0))


def _token_tile(block, n_load):
    return pl.BlockSpec(block, lambda s: (jnp.maximum(s - n_load, 0),) + (0,) * (len(block) - 1))


def _stash_rows(src_ref, dst_ref, step):
    rows = src_ref.shape[0]
    dst_ref[pl.ds(pl.multiple_of(step * rows, rows), rows), :] = src_ref[...].astype(BF16)


def _nbytes(shape, dtype):
    return math.prod(shape) * jnp.dtype(dtype).itemsize


def _params(semantics, vmem_bytes):
    assert vmem_bytes <= V7X_VMEM_BYTES, vmem_bytes
    return pltpu.CompilerParams(dimension_semantics=semantics, vmem_limit_bytes=int(vmem_bytes))


def _rms_norm(x, g):
    return x * lax.rsqrt(jnp.mean(x * x, axis=-1, keepdims=True) + RMS_EPS) * g


def _gelu(x):
    return 0.5 * x * (1.0 + lax.erf(x * (2.0 ** -0.5)))


def _ff_chunks(d_ff):
    return [(c, min(c + FF_CHUNK, d_ff)) for c in range(0, d_ff, FF_CHUNK)]


def _row_groups(x):
    return [x[r:r + V7X_SUBLANES] for r in range(0, x.shape[0], V7X_SUBLANES)]


def _swiglu_half_step(x, g_ref, wgu_ref, wd_ref, act_ref):
    d_ff = wd_ref.shape[0]
    h = _rms_norm(x, g_ref[...]).astype(BF16)
    for c0, c1 in _ff_chunks(d_ff):
        g = jnp.dot(h, wgu_ref[:, c0:c1], preferred_element_type=F32)
        u = jnp.dot(h, wgu_ref[:, d_ff + c0:d_ff + c1], preferred_element_type=F32)
        act_ref[:, c0:c1] = (g * jax.nn.sigmoid(g) * u).astype(BF16)
    return x + 0.5 * jnp.dot(act_ref[...], wd_ref[...], preferred_element_type=F32)


def _ffn_weight_specs(w, layer, n_load):
    return [_layer(w["norm"], layer), _weight_chunk(w["w_gu"], layer, n_load),
            _weight_chunk(w["w_down"], layer, n_load)]


def _ffn_weight_args(w):
    return [w["norm"], w["w_gu"], w["w_down"]]


def _ffn_scratch(tm, w):
    d, d_gu = w["w_gu"].shape[1:]
    d_ff = w["w_down"].shape[1]
    return [pltpu.VMEM((d, d_gu), BF16), pltpu.VMEM((d_ff, d), BF16), pltpu.VMEM((tm, d_ff), BF16)]


def _ffn_vmem(tm, d, d_ff, n_load):
    return (_nbytes((d, 2 * d_ff), BF16) + _nbytes((d_ff, d), BF16)
            + 2 * (_nbytes((d, 2 * d_ff), F32) + _nbytes((d_ff, d), F32)) // n_load
            + _nbytes((tm, d_ff), BF16)
            + 4 * _nbytes((tm, FF_CHUNK), F32))


def _ffn_kernel(x_ref, g_ref, wgu32_ref, wd32_ref, o_ref, wgu_ref, wd_ref, act_ref, *, n_load):
    step = pl.program_id(0)

    @pl.when(step < n_load)
    def _():
        _stash_rows(wgu32_ref, wgu_ref, step)
        _stash_rows(wd32_ref, wd_ref, step)

    @pl.when(step >= n_load)
    def _():
        o_ref[...] = _swiglu_half_step(x_ref[...], g_ref, wgu_ref, wd_ref, act_ref)


def _ffn(x, w, layer):
    t, d = x.shape
    d_ff = w["w_down"].shape[1]
    tm = TOKEN_TILE_FFN
    n_load = LOAD_STEPS
    assert t % tm == 0
    tile = _token_tile((tm, d), n_load)
    vmem = _ffn_vmem(tm, d, d_ff, n_load) + 4 * _nbytes((tm, d), F32)
    return pl.pallas_call(
        functools.partial(_ffn_kernel, n_load=n_load),
        grid=(n_load + t // tm,),
        in_specs=[tile] + _ffn_weight_specs(w, layer, n_load),
        out_specs=tile,
        out_shape=jax.ShapeDtypeStruct((t, d), F32),
        scratch_shapes=_ffn_scratch(tm, w),
        compiler_params=_params(("arbitrary",), vmem),
        name="ffn",
    )(x, *_ffn_weight_args(w))


def _mixer_in_kernel(x_ref, ng_ref, wtm32_ref, wtf32_ref, wtg32_ref, woc32_ref, wos32_ref, bf_ref, bg_ref, cw_ref,
                     lng_ref, lnb_ref, sw_ref, sb_ref, qg_ref, kg_ref,
                     part_ref, g2_ref, qaug_ref, kaug_ref, vaug_ref, crow_ref,
                     wm_ref, wf_ref, wgt_ref, woc_ref, wos_ref, zs_ref, ccarry_ref, vn_ref, yb_ref, *,
                     n_load, tiles_per_seq, q_scale, c_scale):
    step = pl.program_id(0)

    @pl.when(step < n_load)
    def _():
        wm_ref[step] = wtm32_ref[...].T.astype(BF16)
        _stash_rows(woc32_ref, woc_ref, step)
        _stash_rows(wos32_ref, wos_ref, step)

    @pl.when(step < wgt_ref.shape[0])
    def _():
        wgt_ref[step] = wtg32_ref[...].T.astype(BF16)

    @pl.when(step == 0)
    def _():
        rows = wtf32_ref[...]
        slab = jnp.concatenate([rows, jnp.zeros((V7X_LANES - rows.shape[0], rows.shape[1]), F32)], axis=0)
        wf_ref[...] = slab.T.astype(BF16)

    @pl.when(step >= n_load)
    def _():
        _mixer_in_tile(step - n_load, x_ref, ng_ref, wm_ref, wf_ref, wgt_ref, bf_ref, bg_ref, cw_ref, lng_ref,
                       lnb_ref, sw_ref, sb_ref, qg_ref, kg_ref, woc_ref, wos_ref,
                       part_ref, g2_ref, qaug_ref, kaug_ref, vaug_ref, crow_ref,
                       zs_ref, ccarry_ref, vn_ref, yb_ref,
                       tiles_per_seq=tiles_per_seq, q_scale=q_scale, c_scale=c_scale)


def _mixer_in_tile(tile_idx, x_ref, ng_ref, wm_ref, wf_ref, wgt_ref, bf_ref, bg_ref, cw_ref, lng_ref, lnb_ref,
                   sw_ref, sb_ref, qg_ref, kg_ref, woc_ref, wos_ref,
                   part_ref, g2_ref, qaug_ref, kaug_ref, vaug_ref, crow_ref,
                   zs_ref, ccarry_ref, vn_ref, yb_ref, *, tiles_per_seq, q_scale, c_scale):
    tm, d = x_ref.shape
    heads = d // HEAD_DIM
    pad = V7X_SUBLANES

    @pl.when(tile_idx % tiles_per_seq == 0)
    def _():
        zs_ref[0:pad, :] = jnp.zeros((pad, d), F32)
        ccarry_ref[...] = jnp.zeros_like(ccarry_ref)

    h = _rms_norm(x_ref[...], ng_ref[...]).astype(BF16)

    def proj(w_ref, c0, c1):
        width = w_ref.shape[2]
        assert c0 % width == 0 and c1 % width == 0
        return jnp.concatenate([jnp.dot(h, w_ref[j], preferred_element_type=F32)
                                for j in range(c0 // width, c1 // width)], axis=1)

    pa = proj(wm_ref, 0, 3 * d)
    f = jnp.dot(h, wf_ref[...], preferred_element_type=F32) + bf_ref[...]
    ps = proj(wm_ref, 3 * d, 5 * d)
    pq = proj(wm_ref, 5 * d, 8 * d)
    gates = [jax.nn.sigmoid(proj(wgt_ref, b * d, (b + 1) * d) + bg_ref[:, b * d:(b + 1) * d])
             for b in range(N_BRANCH)]

    zs_ref[pad:pad + tm, :] = pa[:, d:2 * d] * pa[:, 2 * d:3 * d]
    conv = (cw_ref[0:1, :] * zs_ref[pad - 2:pad - 2 + tm, :]
            + cw_ref[1:2, :] * zs_ref[pad - 1:pad - 1 + tm, :]
            + cw_ref[2:3, :] * zs_ref[pad:pad + tm, :])
    ya_in = (pa[:, 0:d] * conv).astype(BF16)
    zs_ref[0:pad, :] = zs_ref[tm:tm + pad, :]

    c = jnp.minimum(f, 0.0) - jnp.log1p(jnp.exp(-jnp.abs(f)))
    t_idx = lax.broadcasted_iota(jnp.int32, c.shape, 0)
    shift = 1
    while shift < tm:
        c = c + jnp.where(t_idx >= shift, pltpu.roll(c, shift, axis=0), 0.0)
        shift *= 2
    c = c + ccarry_ref[0:1, :]
    ccarry_ref[0:1, :] = c[tm - 1:tm, :]
    c = c * c_scale
    crow_ref[...] = c.T[0:heads, :]

    u = _gelu(ps[:, 0:d])
    vv = _gelu(ps[:, d:2 * d])
    mu = jnp.mean(vv, axis=-1, keepdims=True)
    vc = vv - mu
    var = jnp.mean(vc * vc, axis=-1, keepdims=True)
    vn_ref[...] = (vc * lax.rsqrt(var + LN_EPS) * lng_ref[...] + lnb_ref[...]).astype(BF16)

    lane = lax.broadcasted_iota(jnp.int32, (tm, HEAD_DIM), 1)
    feature = lax.broadcasted_iota(jnp.int32, (HEAD_DIM, tm), 0)
    ones_rows = jnp.where(feature < FORGET_SPLIT, 1.0, 0.0).astype(BF16)
    qn = []
    for g in range(heads):
        hs = slice(g * HEAD_DIM, (g + 1) * HEAD_DIM)
        qn.append(_rms_norm(pq[:, hs], qg_ref[:, hs]) * q_scale)
        ks = slice(d + g * HEAD_DIM, d + (g + 1) * HEAD_DIM)
        kaug_ref[:, 2 * g * HEAD_DIM:(2 * g + 1) * HEAD_DIM] = _rms_norm(pq[:, ks], kg_ref[:, hs]).astype(BF16)
        rest = -jnp.broadcast_to(c[:, g:g + 1], (tm, HEAD_DIM))
        slab = jnp.zeros((tm, HEAD_DIM), F32)
        for term in range(FORGET_SPLIT):
            piece = rest.astype(BF16).astype(F32)
            slab = jnp.where(lane == term, piece, slab)
            rest = rest - piece
        kaug_ref[:, (2 * g + 1) * HEAD_DIM:(2 * g + 2) * HEAD_DIM] = slab.astype(BF16)
    qt = jnp.concatenate(qn, axis=1).T.astype(BF16)
    vt = pq[:, 2 * d:3 * d].T.astype(BF16)
    v_rows = HEAD_DIM + ONES_ROWS
    for g in range(heads):
        hs = slice(g * HEAD_DIM, (g + 1) * HEAD_DIM)
        qaug_ref[2 * g * HEAD_DIM:(2 * g + 1) * HEAD_DIM, :] = qt[hs, :]
        qaug_ref[(2 * g + 1) * HEAD_DIM:(2 * g + 2) * HEAD_DIM, :] = ones_rows
        vaug_ref[g * v_rows:g * v_rows + HEAD_DIM, :] = vt[hs, :]
        vaug_ref[g * v_rows + HEAD_DIM:(g + 1) * v_rows, :] = jnp.ones((ONES_ROWS, tm), BF16)

    ya = jnp.dot(ya_in, woc_ref[...], preferred_element_type=F32)
    n_chunks = tm // SGU_CHUNK
    pos_t = lax.broadcasted_iota(jnp.int32, (SGU_CHUNK, SGU_CHUNK), 0)
    pos_s = lax.broadcasted_iota(jnp.int32, (SGU_CHUNK, SGU_CHUNK), 1)
    for g in range(heads):
        hs = slice(g * HEAD_DIM, (g + 1) * HEAD_DIM)
        w = jnp.where(pos_s <= pos_t, sw_ref[g], 0.0).astype(BF16)
        rhs = jnp.concatenate([vn_ref[c0 * SGU_CHUNK:(c0 + 1) * SGU_CHUNK, hs] for c0 in range(n_chunks)], axis=1)
        s = jnp.dot(w, rhs, preferred_element_type=F32) + sb_ref[:, g:g + 1]
        for c0 in range(n_chunks):
            rows = slice(c0 * SGU_CHUNK, (c0 + 1) * SGU_CHUNK)
            yb_ref[rows, hs] = (u[rows, hs] * s[:, c0 * SGU_CHUNK:(c0 + 1) * SGU_CHUNK]).astype(BF16)
    yb = jnp.dot(yb_ref[...], wos_ref[...], preferred_element_type=F32)
    part_ref[...] = gates[0] * ya + gates[1] * yb
    g2_ref[...] = gates[2]


def _mixer_in(x, w, layer, *, seq_len):
    t, d = x.shape
    tm = SEQ_BLOCK
    heads = d // HEAD_DIM
    assert seq_len % tm == 0 and tm % SGU_CHUNK == 0 and d % HEAD_DIM == 0 and heads <= V7X_SUBLANES
    n_load = LOAD_STEPS
    w_t = w["w_in_t"]
    n_in = w_t.shape[1]
    n_main = 8 * d
    main_rows = n_main // n_load
    n_gate = N_BRANCH * d // GATE_CHUNK
    assert n_in == n_main + heads + N_BRANCH * d and main_rows * n_load == n_main and main_rows % V7X_LANES == 0
    assert n_gate * GATE_CHUNK == N_BRANCH * d and n_gate <= n_load and d % GATE_CHUNK == 0
    kernel = functools.partial(
        _mixer_in_kernel, n_load=n_load, tiles_per_seq=seq_len // tm,
        q_scale=HEAD_DIM ** -0.5 * math.log2(math.e), c_scale=math.log2(math.e))
    main_spec = pl.BlockSpec((None, main_rows, d), lambda s: (layer, jnp.minimum(s, n_load - 1), 0))
    forget_spec = pl.BlockSpec((pl.Squeezed(), pl.Element(V7X_SUBLANES), pl.Element(d)),
                               lambda s: (layer, n_main, 0))
    gate_align = math.gcd(n_main + heads, GATE_CHUNK, V7X_SUBLANES)
    gate_spec = pl.BlockSpec(
        (pl.Squeezed(), pl.Element(GATE_CHUNK), pl.Element(d)),
        lambda s: (layer, pl.multiple_of(n_main + heads + GATE_CHUNK * jnp.minimum(s, n_gate - 1), gate_align), 0))
    streamed = ["w_out_conv", "w_out_sgu"]
    small = ["b_forget", "b_gate", "conv_w", "sgu_ln_g", "sgu_ln_b", "sgu_w", "sgu_b_t", "q_norm_g", "k_norm_g"]
    weight_specs = ([_layer(w["mix_norm"], layer), main_spec, forget_spec, gate_spec]
                    + [_weight_chunk(w[n], layer, n_load) for n in streamed] + [_layer(w[n], layer) for n in small])
    weight_args = [w["mix_norm"], w_t, w_t, w_t] + [w[n] for n in streamed] + [w[n] for n in small]
    v_rows = heads * (HEAD_DIM + ONES_ROWS)
    tile = _token_tile((tm, d), n_load)
    resident = [pltpu.VMEM((n_load, d, main_rows), BF16), pltpu.VMEM((d, V7X_LANES), BF16),
                pltpu.VMEM((n_gate, d, GATE_CHUNK), BF16), pltpu.VMEM((d, d), BF16), pltpu.VMEM((d, d), BF16)]
    scratch = resident + [pltpu.VMEM((tm + V7X_SUBLANES, d), F32), pltpu.VMEM((V7X_SUBLANES, V7X_LANES), F32),
                          pltpu.VMEM((tm, d), BF16), pltpu.VMEM((tm, d), BF16)]
    vmem = (sum(_nbytes(s.shape, s.dtype) for s in scratch)
            + 2 * _nbytes((main_rows + GATE_CHUNK + V7X_SUBLANES + 2 * d // n_load, d), F32)
            + 2 * _nbytes((main_rows, d), F32)
            + sum(_nbytes(w[n].shape[1:], w[n].dtype) for n in small)
            + 2 * (3 * _nbytes((tm, d), F32) + 4 * _nbytes((tm, d), BF16) + _nbytes((v_rows, tm), BF16)
                   + _nbytes((V7X_SUBLANES, tm), F32))
            + 10 * _nbytes((tm, d), F32))
    return pl.pallas_call(
        kernel,
        grid=(n_load + t // tm,),
        in_specs=[tile] + weight_specs,
        out_specs=[tile, tile,
                   _token_tile((None, 2 * d, tm), n_load),
                   _token_tile((tm, 2 * d), n_load),
                   _token_tile((None, v_rows, tm), n_load),
                   _token_tile((None, heads, tm), n_load)],
        out_shape=[jax.ShapeDtypeStruct((t, d), F32), jax.ShapeDtypeStruct((t, d), F32),
                   jax.ShapeDtypeStruct((t // tm, 2 * d, tm), BF16), jax.ShapeDtypeStruct((t, 2 * d), BF16),
                   jax.ShapeDtypeStruct((t // tm, v_rows, tm), BF16),
                   jax.ShapeDtypeStruct((t // tm, heads, tm), F32)],
        scratch_shapes=scratch,
        compiler_params=_params(("arbitrary",), vmem),
        name="mixer_in",
    )(x, *weight_args)


def _attn_kernel(qaug_ref, kaug_ref, vaug_ref, crow_ref, o_ref, s0_ref, s1_ref, m_ref, acc_ref):
    tq = qaug_ref.shape[2]
    heads = m_ref.shape[0]
    g = pl.program_id(1)
    ia, ib = 2 * g, 2 * g + 1
    head_cols = [slice(h * HEAD_DIM, (h + 1) * HEAD_DIM) for h in range(heads)]
    qk_cols = [slice(h * 2 * HEAD_DIM, (h + 1) * 2 * HEAD_DIM) for h in range(heads)]
    v_rows = [slice(h * (HEAD_DIM + ONES_ROWS), (h + 1) * (HEAD_DIM + ONES_ROWS)) for h in range(heads)]
    key_pos = lax.broadcasted_iota(jnp.int32, (tq, tq), 0)
    query_pos = lax.broadcasted_iota(jnp.int32, (tq, tq), 1)
    causal = key_pos <= query_pos

    s_refs = (s0_ref, s1_ref)

    def reset():
        m_ref[...] = jnp.full(m_ref.shape, MASKED, F32)
        acc_ref[...] = jnp.zeros(acc_ref.shape, F32)

    def logits(qb, j, slot, diagonal):
        rows = pl.ds(pl.multiple_of(j * tq, tq), tq)
        for h in range(heads):
            s = jnp.dot(kaug_ref[rows, qk_cols[h]], qaug_ref[qb, qk_cols[h], :], preferred_element_type=F32)
            s_refs[slot][h] = jnp.where(causal, s, MASKED) if diagonal else s

    def softmax_pv(i, j, slot):
        probs, rescale = [], []
        for h in range(heads):
            s = s_refs[slot][h]
            cq = crow_ref[i, h:h + 1, :]
            m_old = m_ref[h]
            block_max = jnp.max(functools.reduce(jnp.maximum, _row_groups(s)), axis=0, keepdims=True)
            m_new = jnp.maximum(m_old, block_max + cq)
            m_ref[h] = m_new
            probs.append(jnp.exp2(s + (cq - m_new)).astype(BF16))
            rescale.append(jnp.exp2(m_old - m_new))
        for h in range(heads):
            acc_ref[h] = rescale[h] * acc_ref[h] + jnp.dot(vaug_ref[j, v_rows[h], :], probs[h],
                                                           preferred_element_type=F32)

    def finish(qb):
        for h, hs in enumerate(head_cols):
            row_sum = acc_ref[h, HEAD_DIM:HEAD_DIM + 1, :]
            o_ref[qb * tq:(qb + 1) * tq, hs] = (acc_ref[h, 0:HEAD_DIM, :] * (1.0 / row_sum)).T.astype(BF16)

    def stage(qb_next, j_next, slot_next, diagonal, i, j, slot):
        logits(qb_next, j_next, slot_next, diagonal)
        softmax_pv(i, j, slot)

    reset()

    @pl.when(g == 0)
    def _():
        logits(0, 0, 0, True)
        stage(1, 0, 1, False, ia, 0, 0)
        finish(0)
        reset()
        stage(1, 1, 0, True, ib, 0, 1)
        softmax_pv(ib, 1, 0)
        finish(1)

    @pl.when(g > 0)
    def _():
        logits(0, 0, 0, False)

        def pair_a(t, carry):
            j = 2 * t
            stage(0, j + 1, 1, False, ia, j, 0)
            stage(0, j + 2, 0, False, ia, j + 1, 1)
            return carry

        lax.fori_loop(0, g - 1, pair_a, 0)
        stage(0, ia - 1, 1, False, ia, ia - 2, 0)
        stage(0, ia, 0, True, ia, ia - 1, 1)
        stage(1, 0, 1, False, ia, ia, 0)
        finish(0)
        reset()

        def pair_b(t, carry):
            j = 2 * t
            stage(1, j + 1, 0, False, ib, j, 1)
            stage(1, j + 2, 1, False, ib, j + 1, 0)
            return carry

        lax.fori_loop(0, g, pair_b, 0)
        stage(1, ib, 0, True, ib, ia, 1)
        softmax_pv(ib, ib, 0)
        finish(1)


def _attention(qaug, kaug, vaug, c_rows, *, batch, seq_len):
    nt, qk_rows, tq = qaug.shape
    nq = seq_len // tq
    heads = qk_rows // (2 * HEAD_DIM)
    d = heads * HEAD_DIM
    v_rows = vaug.shape[1]
    assert nt == batch * nq and tq % V7X_LANES == 0 and v_rows == heads * (HEAD_DIM + ONES_ROWS)
    assert nq % 2 == 0
    steps = nq // 2
    scratch = [pltpu.VMEM((heads, tq, tq), F32), pltpu.VMEM((heads, tq, tq), F32),
               pltpu.VMEM((heads, 1, tq), F32),
               pltpu.VMEM((heads, HEAD_DIM + ONES_ROWS, tq), F32)]
    vmem = (2 * (_nbytes((seq_len, qk_rows), BF16) + _nbytes((nq, v_rows, tq), BF16)
                 + 2 * _nbytes((qk_rows, tq), BF16) + 2 * _nbytes((tq, d), BF16)
                 + _nbytes((nq, V7X_SUBLANES, tq), F32))
            + sum(_nbytes(s.shape, s.dtype) for s in scratch)
            + _nbytes((V7X_SUBLANES * heads, tq), F32)
            + 16 * _nbytes((tq, tq), F32))
    return pl.pallas_call(
        _attn_kernel,
        grid=(batch, steps),
        in_specs=[pl.BlockSpec((2, qk_rows, tq), lambda b, g: (b * steps + g, 0, 0)),
                  pl.BlockSpec((seq_len, qk_rows), lambda b, g: (b, 0)),
                  pl.BlockSpec((nq, v_rows, tq), lambda b, g: (b, 0, 0)),
                  pl.BlockSpec((nq, heads, tq), lambda b, g: (b, 0, 0))],
        out_specs=pl.BlockSpec((2 * tq, d), lambda b, g: (b * steps + g, 0)),
        out_shape=jax.ShapeDtypeStruct((nt * tq, d), BF16),
        scratch_shapes=scratch,
        compiler_params=_params(("arbitrary", "arbitrary"), vmem),
        name="attention",
    )(qaug, kaug, vaug, c_rows)


def _mixer_out_kernel(x_ref, a_ref, part_ref, g2_ref, woa32_ref, wo32_ref, g_ref, wgu32_ref, wd32_ref,
                      o_ref, woa_ref, wo_ref, wgu_ref, wd_ref, act_ref, *, n_load):
    step = pl.program_id(0)

    @pl.when(step < n_load)
    def _():
        _stash_rows(woa32_ref, woa_ref, step)
        _stash_rows(wo32_ref, wo_ref, step)
        _stash_rows(wgu32_ref, wgu_ref, step)
        _stash_rows(wd32_ref, wd_ref, step)

    @pl.when(step >= n_load)
    def _():
        yc = jnp.dot(a_ref[...], woa_ref[...], preferred_element_type=F32)
        merged = part_ref[...] + g2_ref[...] * yc
        x = x_ref[...] + jnp.dot(merged.astype(BF16), wo_ref[...], preferred_element_type=F32)
        o_ref[...] = _swiglu_half_step(x, g_ref, wgu_ref, wd_ref, act_ref)


def _mixer_out(x, attn, part, g2, w_mix, w_ffn, layer):
    t, d = x.shape
    d_ff = w_ffn["w_down"].shape[1]
    tm = TOKEN_TILE_OUT
    n_load = LOAD_STEPS
    assert t % tm == 0
    tile = _token_tile((tm, d), n_load)
    vmem = (_ffn_vmem(tm, d, d_ff, n_load) + 2 * _nbytes((d, d), BF16) + 4 * _nbytes((d, d), F32) // n_load
            + 2 * (4 * _nbytes((tm, d), F32) + _nbytes((tm, d), BF16)) + 2 * _nbytes((tm, d), F32))
    return pl.pallas_call(
        functools.partial(_mixer_out_kernel, n_load=n_load),
        grid=(n_load + t // tm,),
        in_specs=[tile, tile, tile, tile,
                  _weight_chunk(w_mix["w_out_attn"], layer, n_load), _weight_chunk(w_mix["w_o"], layer, n_load)]
        + _ffn_weight_specs(w_ffn, layer, n_load),
        out_specs=tile,
        out_shape=jax.ShapeDtypeStruct((t, d), F32),
        scratch_shapes=[pltpu.VMEM((d, d), BF16), pltpu.VMEM((d, d), BF16)] + _ffn_scratch(tm, w_ffn),
        compiler_params=_params(("arbitrary",), vmem),
        name="mixer_out",
    )(x, attn, part, g2, w_mix["w_out_attn"], w_mix["w_o"], *_ffn_weight_args(w_ffn))


def kernel(x, ffn1_norm, ffn1_w_gu, ffn1_w_down, mix_norm, w_in, b_forget, b_gate, conv_w, sgu_ln_g, sgu_ln_b,
           sgu_w, sgu_b, q_norm_g, k_norm_g, w_out_conv, w_out_sgu, w_out_attn, w_o, ffn2_norm, ffn2_w_gu,
           ffn2_w_down):
    batch, seq_len, d = x.shape
    depth = w_in.shape[0]
    heads = d // HEAD_DIM
    rows = lambda a: a.reshape(depth, 1, -1)

    ffn1 = {"norm": rows(ffn1_norm), "w_gu": ffn1_w_gu, "w_down": ffn1_w_down}
    ffn2 = {"norm": rows(ffn2_norm), "w_gu": ffn2_w_gu, "w_down": ffn2_w_down}
    mix = {
        "mix_norm": rows(mix_norm),
        "w_in_t": jnp.swapaxes(w_in, 1, 2),
        "b_forget": jnp.pad(rows(b_forget), ((0, 0), (0, 0), (0, V7X_LANES - heads))),
        "b_gate": rows(b_gate),
        "conv_w": conv_w,
        "sgu_ln_g": rows(sgu_ln_g),
        "sgu_ln_b": rows(sgu_ln_b),
        "sgu_w": sgu_w,
        "sgu_b_t": jnp.swapaxes(sgu_b, 1, 2),
        "q_norm_g": rows(q_norm_g),
        "k_norm_g": rows(k_norm_g),
        "w_out_conv": w_out_conv,
        "w_out_sgu": w_out_sgu,
        "w_out_attn": w_out_attn,
        "w_o": w_o,
    }

    xt = x.reshape(batch * seq_len, d)
    for layer in range(depth):
        xt = _ffn(xt, ffn1, layer)
        part, g2, qaug, kaug, vaug, c_rows = _mixer_in(xt, mix, layer, seq_len=seq_len)
        attn = _attention(qaug, kaug, vaug, c_rows, batch=batch, seq_len=seq_len)
        xt = _mixer_out(xt, attn, part, g2, mix, ffn2, layer)
    return xt.reshape(batch, seq_len, d)
```

```python
import functools
import math

import jax
import jax.numpy as jnp
from jax import lax
from jax.experimental import pallas as pl
from jax.experimental.pallas import tpu as pltpu

F32 = jnp.float32
BF16 = jnp.bfloat16

RMS_EPS = 1e-6
LN_EPS = 1e-5
SGU_CHUNK = 128
HEAD_DIM = 128
N_BRANCH = 3
MASKED = -1e30
FORGET_SPLIT = 3
ONES_ROWS = 16

V7X_LANES = 128
V7X_SUBLANES = 8
BF16_SUBLANES = 16
V7X_VMEM_BYTES = 64 * 1024 * 1024
LOAD_STEPS = 16
FF_CHUNK = 1024
GATE_CHUNK = 256

TOKEN_TILE_FFN = 1024
TOKEN_TILE_OUT = 512
SEQ_BLOCK = 256


def _layer(arr, layer):
    index = (layer,) + (0,) * (arr.ndim - 1)
    return pl.BlockSpec((None,) + tuple(arr.shape[1:]), lambda *_: index, pipeline_mode=pl.Buffered(1))


def _weight_chunk(arr, layer, n_load):
    rows = arr.shape[1] // n_load
    assert rows * n_load == arr.shape[1] and rows % BF16_SUBLANES == 0, (arr.shape, n_load)
    return pl.BlockSpec((None, rows, arr.shape[2]), lambda s: (layer, jnp.minimum(s, n_load - 1), 0))


def _token_tile(block, n_load):
    return pl.BlockSpec(block, lambda s: (jnp.maximum(s - n_load, 0),) + (0,) * (len(block) - 1))


def _stash_rows(src_ref, dst_ref, step):
    rows = src_ref.shape[0]
    dst_ref[pl.ds(pl.multiple_of(step * rows, rows), rows), :] = src_ref[...].astype(BF16)


def _nbytes(shape, dtype):
    return math.prod(shape) * jnp.dtype(dtype).itemsize


def _params(semantics, vmem_bytes):
    assert vmem_bytes <= V7X_VMEM_BYTES, vmem_bytes
    return pltpu.CompilerParams(dimension_semantics=semantics, vmem_limit_bytes=int(vmem_bytes))


def _rms_norm(x, g):
    return x * lax.rsqrt(jnp.mean(x * x, axis=-1, keepdims=True) + RMS_EPS) * g


def _gelu(x):
    return 0.5 * x * (1.0 + lax.erf(x * (2.0 ** -0.5)))


def _ff_chunks(d_ff):
    return [(c, min(c + FF_CHUNK, d_ff)) for c in range(0, d_ff, FF_CHUNK)]


def _row_groups(x):
    return [x[r:r + V7X_SUBLANES] for r in range(0, x.shape[0], V7X_SUBLANES)]


def _swiglu_half_step(x, g_ref, wgu_ref, wd_ref, act_ref):
    d_ff = wd_ref.shape[0]
    h = _rms_norm(x, g_ref[...]).astype(BF16)
    for c0, c1 in _ff_chunks(d_ff):
        g = jnp.dot(h, wgu_ref[:, c0:c1], preferred_element_type=F32)
        u = jnp.dot(h, wgu_ref[:, d_ff + c0:d_ff + c1], preferred_element_type=F32)
        act_ref[:, c0:c1] = (g * jax.nn.sigmoid(g) * u).astype(BF16)
    return x + 0.5 * jnp.dot(act_ref[...], wd_ref[...], preferred_element_type=F32)


def _ffn_weight_specs(w, layer, n_load):
    return [_layer(w["norm"], layer), _weight_chunk(w["w_gu"], layer, n_load),
            _weight_chunk(w["w_down"], layer, n_load)]


def _ffn_weight_args(w):
    return [w["norm"], w["w_gu"], w["w_down"]]


def _ffn_scratch(tm, w):
    d, d_gu = w["w_gu"].shape[1:]
    d_ff = w["w_down"].shape[1]
    return [pltpu.VMEM((d, d_gu), BF16), pltpu.VMEM((d_ff, d), BF16), pltpu.VMEM((tm, d_ff), BF16)]


def _ffn_vmem(tm, d, d_ff, n_load):
    return (_nbytes((d, 2 * d_ff), BF16) + _nbytes((d_ff, d), BF16)
            + 2 * (_nbytes((d, 2 * d_ff), F32) + _nbytes((d_ff, d), F32)) // n_load
            + _nbytes((tm, d_ff), BF16)
            + 4 * _nbytes((tm, FF_CHUNK), F32))


def _ffn_kernel(x_ref, g_ref, wgu32_ref, wd32_ref, o_ref, wgu_ref, wd_ref, act_ref, *, n_load):
    step = pl.program_id(0)

    @pl.when(step < n_load)
    def _():
        _stash_rows(wgu32_ref, wgu_ref, step)
        _stash_rows(wd32_ref, wd_ref, step)

    @pl.when(step >= n_load)
    def _():
        o_ref[...] = _swiglu_half_step(x_ref[...], g_ref, wgu_ref, wd_ref, act_ref)


def _ffn(x, w, layer):
    t, d = x.shape
    d_ff = w["w_down"].shape[1]
    tm = TOKEN_TILE_FFN
    n_load = LOAD_STEPS
    assert t % tm == 0
    tile = _token_tile((tm, d), n_load)
    vmem = _ffn_vmem(tm, d, d_ff, n_load) + 4 * _nbytes((tm, d), F32)
    return pl.pallas_call(
        functools.partial(_ffn_kernel, n_load=n_load),
        grid=(n_load + t // tm,),
        in_specs=[tile] + _ffn_weight_specs(w, layer, n_load),
        out_specs=tile,
        out_shape=jax.ShapeDtypeStruct((t, d), F32),
        scratch_shapes=_ffn_scratch(tm, w),
        compiler_params=_params(("arbitrary",), vmem),
        name="ffn",
    )(x, *_ffn_weight_args(w))


def _mixer_in_kernel(x_ref, ng_ref, wtm32_ref, wtf32_ref, wtg32_ref, woc32_ref, wos32_ref, bf_ref, bg_ref, cw_ref,
                     lng_ref, lnb_ref, sw_ref, sb_ref, qg_ref, kg_ref,
                     part_ref, g2_ref, qaug_ref, kaug_ref, vaug_ref, crow_ref,
                     wm_ref, wf_ref, wgt_ref, woc_ref, wos_ref, zs_ref, ccarry_ref, vn_ref, yb_ref, *,
                     n_load, tiles_per_seq, q_scale, c_scale):
    step = pl.program_id(0)

    @pl.when(step < n_load)
    def _():
        wm_ref[step] = wtm32_ref[...].T.astype(BF16)
        _stash_rows(woc32_ref, woc_ref, step)
        _stash_rows(wos32_ref, wos_ref, step)

    @pl.when(step < wgt_ref.shape[0])
    def _():
        wgt_ref[step] = wtg32_ref[...].T.astype(BF16)

    @pl.when(step == 0)
    def _():
        rows = wtf32_ref[...]
        slab = jnp.concatenate([rows, jnp.zeros((V7X_LANES - rows.shape[0], rows.shape[1]), F32)], axis=0)
        wf_ref[...] = slab.T.astype(BF16)

    @pl.when(step >= n_load)
    def _():
        _mixer_in_tile(step - n_load, x_ref, ng_ref, wm_ref, wf_ref, wgt_ref, bf_ref, bg_ref, cw_ref, lng_ref,
                       lnb_ref, sw_ref, sb_ref, qg_ref, kg_ref, woc_ref, wos_ref,
                       part_ref, g2_ref, qaug_ref, kaug_ref, vaug_ref, crow_ref,
                       zs_ref, ccarry_ref, vn_ref, yb_ref,
                       tiles_per_seq=tiles_per_seq, q_scale=q_scale, c_scale=c_scale)


def _mixer_in_tile(tile_idx, x_ref, ng_ref, wm_ref, wf_ref, wgt_ref, bf_ref, bg_ref, cw_ref, lng_ref, lnb_ref,
                   sw_ref, sb_ref, qg_ref, kg_ref, woc_ref, wos_ref,
                   part_ref, g2_ref, qaug_ref, kaug_ref, vaug_ref, crow_ref,
                   zs_ref, ccarry_ref, vn_ref, yb_ref, *, tiles_per_seq, q_scale, c_scale):
    tm, d = x_ref.shape
    heads = d // HEAD_DIM
    pad = V7X_SUBLANES

    @pl.when(tile_idx % tiles_per_seq == 0)
    def _():
        zs_ref[0:pad, :] = jnp.zeros((pad, d), F32)
        ccarry_ref[...] = jnp.zeros_like(ccarry_ref)

    h = _rms_norm(x_ref[...], ng_ref[...]).astype(BF16)

    def proj(w_ref, c0, c1):
        width = w_ref.shape[2]
        assert c0 % width == 0 and c1 % width == 0
        return jnp.concatenate([jnp.dot(h, w_ref[j], preferred_element_type=F32)
                                for j in range(c0 // width, c1 // width)], axis=1)

    pa = proj(wm_ref, 0, 3 * d)
    f = jnp.dot(h, wf_ref[...], preferred_element_type=F32) + bf_ref[...]
    ps = proj(wm_ref, 3 * d, 5 * d)
    pq = proj(wm_ref, 5 * d, 8 * d)

    zs_ref[pad:pad + tm, :] = pa[:, d:2 * d] * pa[:, 2 * d:3 * d]
    conv = (cw_ref[0:1, :] * zs_ref[pad - 2:pad - 2 + tm, :]
            + cw_ref[1:2, :] * zs_ref[pad - 1:pad - 1 + tm, :]
            + cw_ref[2:3, :] * zs_ref[pad:pad + tm, :])
    ya_in = (pa[:, 0:d] * conv).astype(BF16)
    zs_ref[0:pad, :] = zs_ref[tm:tm + pad, :]

    c = jnp.minimum(f, 0.0) - jnp.log1p(jnp.exp(-jnp.abs(f)))
    t_idx = lax.broadcasted_iota(jnp.int32, c.shape, 0)
    shift = 1
    while shift < tm:
        c = c + jnp.where(t_idx >= shift, pltpu.roll(c, shift, axis=0), 0.0)
        shift *= 2
    c = c + ccarry_ref[0:1, :]
    ccarry_ref[0:1, :] = c[tm - 1:tm, :]
    c = c * c_scale
    crow_ref[...] = c.T[0:heads, :]

    u = _gelu(ps[:, 0:d])
    vv = _gelu(ps[:, d:2 * d])
    mu = jnp.mean(vv, axis=-1, keepdims=True)
    vc = vv - mu
    var = jnp.mean(vc * vc, axis=-1, keepdims=True)
    vn_ref[...] = (vc * lax.rsqrt(var + LN_EPS) * lng_ref[...] + lnb_ref[...]).astype(BF16)

    lane = lax.broadcasted_iota(jnp.int32, (tm, HEAD_DIM), 1)
    feature = lax.broadcasted_iota(jnp.int32, (HEAD_DIM, tm), 0)
    ones_rows = jnp.where(feature < FORGET_SPLIT, 1.0, 0.0).astype(BF16)
    qn = []
    for g in range(heads):
        hs = slice(g * HEAD_DIM, (g + 1) * HEAD_DIM)
        qn.append(_rms_norm(pq[:, hs], qg_ref[:, hs]) * q_scale)
        ks = slice(d + g * HEAD_DIM, d + (g + 1) * HEAD_DIM)
        kaug_ref[:, 2 * g * HEAD_DIM:(2 * g + 1) * HEAD_DIM] = _rms_norm(pq[:, ks], kg_ref[:, hs]).astype(BF16)
        rest = -jnp.broadcast_to(c[:, g:g + 1], (tm, HEAD_DIM))
        slab = jnp.zeros((tm, HEAD_DIM), F32)
        for term in range(FORGET_SPLIT):
            piece = rest.astype(BF16).astype(F32)
            slab = jnp.where(lane == term, piece, slab)
            rest = rest - piece
        kaug_ref[:, (2 * g + 1) * HEAD_DIM:(2 * g + 2) * HEAD_DIM] = slab.astype(BF16)
    qt = jnp.concatenate(qn, axis=1).T.astype(BF16)
    vt = pq[:, 2 * d:3 * d].T.astype(BF16)
    v_rows = HEAD_DIM + ONES_ROWS
    for g in range(heads):
        hs = slice(g * HEAD_DIM, (g + 1) * HEAD_DIM)
        qaug_ref[2 * g * HEAD_DIM:(2 * g + 1) * HEAD_DIM, :] = qt[hs, :]
        qaug_ref[(2 * g + 1) * HEAD_DIM:(2 * g + 2) * HEAD_DIM, :] = ones_rows
        vaug_ref[g * v_rows:g * v_rows + HEAD_DIM, :] = vt[hs, :]
        vaug_ref[g * v_rows + HEAD_DIM:(g + 1) * v_rows, :] = jnp.ones((ONES_ROWS, tm), BF16)

    ya = jnp.dot(ya_in, woc_ref[...], preferred_element_type=F32)
    n_chunks = tm // SGU_CHUNK
    pos_t = lax.broadcasted_iota(jnp.int32, (SGU_CHUNK, SGU_CHUNK), 0)
    pos_s = lax.broadcasted_iota(jnp.int32, (SGU_CHUNK, SGU_CHUNK), 1)
    for g in range(heads):
        hs = slice(g * HEAD_DIM, (g + 1) * HEAD_DIM)
        w = jnp.where(pos_s <= pos_t, sw_ref[g], 0.0).astype(BF16)
        rhs = jnp.concatenate([vn_ref[c0 * SGU_CHUNK:(c0 + 1) * SGU_CHUNK, hs] for c0 in range(n_chunks)], axis=1)
        s = jnp.dot(w, rhs, preferred_element_type=F32) + sb_ref[:, g:g + 1]
        for c0 in range(n_chunks):
            rows = slice(c0 * SGU_CHUNK, (c0 + 1) * SGU_CHUNK)
            yb_ref[rows, hs] = (u[rows, hs] * s[:, c0 * SGU_CHUNK:(c0 + 1) * SGU_CHUNK]).astype(BF16)
    gates = [jax.nn.sigmoid(proj(wgt_ref, b * d, (b + 1) * d) + bg_ref[:, b * d:(b + 1) * d])
             for b in range(N_BRANCH)]
    yb = jnp.dot(yb_ref[...], wos_ref[...], preferred_element_type=F32)
    part_ref[...] = gates[0] * ya + gates[1] * yb
    g2_ref[...] = gates[2]


def _mixer_in(x, w, layer, *, seq_len):
    t, d = x.shape
    tm = SEQ_BLOCK
    heads = d // HEAD_DIM
    assert seq_len % tm == 0 and tm % SGU_CHUNK == 0 and d % HEAD_DIM == 0 and heads <= V7X_SUBLANES
    n_load = LOAD_STEPS
    w_t = w["w_in_t"]
    n_in = w_t.shape[1]
    n_main = 8 * d
    main_rows = n_main // n_load
    n_gate = N_BRANCH * d // GATE_CHUNK
    assert n_in == n_main + heads + N_BRANCH * d and main_rows * n_load == n_main and main_rows % V7X_LANES == 0
    assert n_gate * GATE_CHUNK == N_BRANCH * d and n_gate <= n_load and d % GATE_CHUNK == 0
    kernel = functools.partial(
        _mixer_in_kernel, n_load=n_load, tiles_per_seq=seq_len // tm,
        q_scale=HEAD_DIM ** -0.5 * math.log2(math.e), c_scale=math.log2(math.e))
    main_spec = pl.BlockSpec((None, main_rows, d), lambda s: (layer, jnp.minimum(s, n_load - 1), 0))
    forget_spec = pl.BlockSpec((pl.Squeezed(), pl.Element(V7X_SUBLANES), pl.Element(d)),
                               lambda s: (layer, n_main, 0))
    gate_align = math.gcd(n_main + heads, GATE_CHUNK, V7X_SUBLANES)
    gate_spec = pl.BlockSpec(
        (pl.Squeezed(), pl.Element(GATE_CHUNK), pl.Element(d)),
        lambda s: (layer, pl.multiple_of(n_main + heads + GATE_CHUNK * jnp.minimum(s, n_gate - 1), gate_align), 0))
    streamed = ["w_out_conv", "w_out_sgu"]
    small = ["b_forget", "b_gate", "conv_w", "sgu_ln_g", "sgu_ln_b", "sgu_w", "sgu_b_t", "q_norm_g", "k_norm_g"]
    weight_specs = ([_layer(w["mix_norm"], layer), main_spec, forget_spec, gate_spec]
                    + [_weight_chunk(w[n], layer, n_load) for n in streamed] + [_layer(w[n], layer) for n in small])
    weight_args = [w["mix_norm"], w_t, w_t, w_t] + [w[n] for n in streamed] + [w[n] for n in small]
    v_rows = heads * (HEAD_DIM + ONES_ROWS)
    tile = _token_tile((tm, d), n_load)
    resident = [pltpu.VMEM((n_load, d, main_rows), BF16), pltpu.VMEM((d, V7X_LANES), BF16),
                pltpu.VMEM((n_gate, d, GATE_CHUNK), BF16), pltpu.VMEM((d, d), BF16), pltpu.VMEM((d, d), BF16)]
    scratch = resident + [pltpu.VMEM((tm + V7X_SUBLANES, d), F32), pltpu.VMEM((V7X_SUBLANES, V7X_LANES), F32),
                          pltpu.VMEM((tm, d), BF16), pltpu.VMEM((tm, d), BF16)]
    vmem = (sum(_nbytes(s.shape, s.dtype) for s in scratch)
            + 2 * _nbytes((main_rows + GATE_CHUNK + V7X_SUBLANES + 2 * d // n_load, d), F32)
            + 2 * _nbytes((main_rows, d), F32)
            + sum(_nbytes(w[n].shape[1:], w[n].dtype) for n in small)
            + 2 * (3 * _nbytes((tm, d), F32) + 4 * _nbytes((tm, d), BF16) + _nbytes((v_rows, tm), BF16)
                   + _nbytes((V7X_SUBLANES, tm), F32))
            + 10 * _nbytes((tm, d), F32))
    return pl.pallas_call(
        kernel,
        grid=(n_load + t // tm,),
        in_specs=[tile] + weight_specs,
        out_specs=[tile, tile,
                   _token_tile((None, 2 * d, tm), n_load),
                   _token_tile((tm, 2 * d), n_load),
                   _token_tile((None, v_rows, tm), n_load),
                   _token_tile((None, heads, tm), n_load)],
        out_shape=[jax.ShapeDtypeStruct((t, d), F32), jax.ShapeDtypeStruct((t, d), F32),
                   jax.ShapeDtypeStruct((t // tm, 2 * d, tm), BF16), jax.ShapeDtypeStruct((t, 2 * d), BF16),
                   jax.ShapeDtypeStruct((t // tm, v_rows, tm), BF16),
                   jax.ShapeDtypeStruct((t // tm, heads, tm), F32)],
        scratch_shapes=scratch,
        compiler_params=_params(("arbitrary",), vmem),
        name="mixer_in",
    )(x, *weight_args)


def _attn_kernel(qaug_ref, kaug_ref, vaug_ref, crow_ref, o_ref, s0_ref, s1_ref, m_ref, acc_ref):
    tq = qaug_ref.shape[2]
    heads = m_ref.shape[0]
    g = pl.program_id(1)
    ia, ib = 2 * g, 2 * g + 1
    head_cols = [slice(h * HEAD_DIM, (h + 1) * HEAD_DIM) for h in range(heads)]
    qk_cols = [slice(h * 2 * HEAD_DIM, (h + 1) * 2 * HEAD_DIM) for h in range(heads)]
    v_rows = [slice(h * (HEAD_DIM + ONES_ROWS), (h + 1) * (HEAD_DIM + ONES_ROWS)) for h in range(heads)]
    key_pos = lax.broadcasted_iota(jnp.int32, (tq, tq), 0)
    query_pos = lax.broadcasted_iota(jnp.int32, (tq, tq), 1)
    causal = key_pos <= query_pos

    s_refs = (s0_ref, s1_ref)

    def reset():
        m_ref[...] = jnp.full(m_ref.shape, MASKED, F32)
        acc_ref[...] = jnp.zeros(acc_ref.shape, F32)

    def logits(qb, j, slot, diagonal):
        rows = pl.ds(pl.multiple_of(j * tq, tq), tq)
        for h in range(heads):
            s = jnp.dot(kaug_ref[rows, qk_cols[h]], qaug_ref[qb, qk_cols[h], :], preferred_element_type=F32)
            s_refs[slot][h] = jnp.where(causal, s, MASKED) if diagonal else s

    def softmax_pv(i, j, slot):
        probs, rescale = [], []
        for h in range(heads):
            s = s_refs[slot][h]
            cq = crow_ref[i, h:h + 1, :]
            m_old = m_ref[h]
            block_max = jnp.max(functools.reduce(jnp.maximum, _row_groups(s)), axis=0, keepdims=True)
            m_new = jnp.maximum(m_old, block_max + cq)
            m_ref[h] = m_new
            probs.append(jnp.exp2(s + (cq - m_new)).astype(BF16))
            rescale.append(jnp.exp2(m_old - m_new))
        for h in range(heads):
            acc_ref[h] = rescale[h] * acc_ref[h] + jnp.dot(vaug_ref[j, v_rows[h], :], probs[h],
                                                           preferred_element_type=F32)

    def finish(qb):
        for h, hs in enumerate(head_cols):
            row_sum = acc_ref[h, HEAD_DIM:HEAD_DIM + 1, :]
            o_ref[qb * tq:(qb + 1) * tq, hs] = (acc_ref[h, 0:HEAD_DIM, :] * (1.0 / row_sum)).T.astype(BF16)

    def stage(qb_next, j_next, slot_next, diagonal, i, j, slot):
        logits(qb_next, j_next, slot_next, diagonal)
        softmax_pv(i, j, slot)

    reset()

    @pl.when(g == 0)
    def _():
        logits(0, 0, 0, True)
        stage(1, 0, 1, False, ia, 0, 0)
        finish(0)
        reset()
        stage(1, 1, 0, True, ib, 0, 1)
        softmax_pv(ib, 1, 0)
        finish(1)

    @pl.when(g > 0)
    def _():
        logits(0, 0, 0, False)

        def pair_a(t, carry):
            j = 2 * t
            stage(0, j + 1, 1, False, ia, j, 0)
            stage(0, j + 2, 0, False, ia, j + 1, 1)
            return carry

        lax.fori_loop(0, g - 1, pair_a, 0)
        stage(0, ia - 1, 1, False, ia, ia - 2, 0)
        stage(0, ia, 0, True, ia, ia - 1, 1)
        stage(1, 0, 1, False, ia, ia, 0)
        finish(0)
        reset()

        def pair_b(t, carry):
            j = 2 * t
            stage(1, j + 1, 0, False, ib, j, 1)
            stage(1, j + 2, 1, False, ib, j + 1, 0)
            return carry

        lax.fori_loop(0, g, pair_b, 0)
        stage(1, ib, 0, True, ib, ia, 1)
        softmax_pv(ib, ib, 0)
        finish(1)


def _attention(qaug, kaug, vaug, c_rows, *, batch, seq_len):
    nt, qk_rows, tq = qaug.shape
    nq = seq_len // tq
    heads = qk_rows // (2 * HEAD_DIM)
    d = heads * HEAD_DIM
    v_rows = vaug.shape[1]
    assert nt == batch * nq and tq % V7X_LANES == 0 and v_rows == heads * (HEAD_DIM + ONES_ROWS)
    assert nq % 2 == 0
    steps = nq // 2
    scratch = [pltpu.VMEM((heads, tq, tq), F32), pltpu.VMEM((heads, tq, tq), F32),
               pltpu.VMEM((heads, 1, tq), F32),
               pltpu.VMEM((heads, HEAD_DIM + ONES_ROWS, tq), F32)]
    vmem = (2 * (_nbytes((seq_len, qk_rows), BF16) + _nbytes((nq, v_rows, tq), BF16)
                 + 2 * _nbytes((qk_rows, tq), BF16) + 2 * _nbytes((tq, d), BF16)
                 + _nbytes((nq, V7X_SUBLANES, tq), F32))
            + sum(_nbytes(s.shape, s.dtype) for s in scratch)
            + _nbytes((V7X_SUBLANES * heads, tq), F32)
            + 16 * _nbytes((tq, tq), F32))
    return pl.pallas_call(
        _attn_kernel,
        grid=(batch, steps),
        in_specs=[pl.BlockSpec((2, qk_rows, tq), lambda b, g: (b * steps + g, 0, 0)),
                  pl.BlockSpec((seq_len, qk_rows), lambda b, g: (b, 0)),
                  pl.BlockSpec((nq, v_rows, tq), lambda b, g: (b, 0, 0)),
                  pl.BlockSpec((nq, heads, tq), lambda b, g: (b, 0, 0))],
        out_specs=pl.BlockSpec((2 * tq, d), lambda b, g: (b * steps + g, 0)),
        out_shape=jax.ShapeDtypeStruct((nt * tq, d), BF16),
        scratch_shapes=scratch,
        compiler_params=_params(("arbitrary", "arbitrary"), vmem),
        name="attention",
    )(qaug, kaug, vaug, c_rows)


def _mixer_out_kernel(x_ref, a_ref, part_ref, g2_ref, woa32_ref, wo32_ref, g_ref, wgu32_ref, wd32_ref,
                      o_ref, woa_ref, wo_ref, wgu_ref, wd_ref, act_ref, *, n_load):
    step = pl.program_id(0)

    @pl.when(step < n_load)
    def _():
        _stash_rows(woa32_ref, woa_ref, step)
        _stash_rows(wo32_ref, wo_ref, step)
        _stash_rows(wgu32_ref, wgu_ref, step)
        _stash_rows(wd32_ref, wd_ref, step)

    @pl.when(step >= n_load)
    def _():
        yc = jnp.dot(a_ref[...], woa_ref[...], preferred_element_type=F32)
        merged = part_ref[...] + g2_ref[...] * yc
        x = x_ref[...] + jnp.dot(merged.astype(BF16), wo_ref[...], preferred_element_type=F32)
        o_ref[...] = _swiglu_half_step(x, g_ref, wgu_ref, wd_ref, act_ref)


def _mixer_out(x, attn, part, g2, w_mix, w_ffn, layer):
    t, d = x.shape
    d_ff = w_ffn["w_down"].shape[1]
    tm = TOKEN_TILE_OUT
    n_load = LOAD_STEPS
    assert t % tm == 0
    tile = _token_tile((tm, d), n_load)
    vmem = (_ffn_vmem(tm, d, d_ff, n_load) + 2 * _nbytes((d, d), BF16) + 4 * _nbytes((d, d), F32) // n_load
            + 2 * (4 * _nbytes((tm, d), F32) + _nbytes((tm, d), BF16)) + 2 * _nbytes((tm, d), F32))
    return pl.pallas_call(
        functools.partial(_mixer_out_kernel, n_load=n_load),
        grid=(n_load + t // tm,),
        in_specs=[tile, tile, tile, tile,
                  _weight_chunk(w_mix["w_out_attn"], layer, n_load), _weight_chunk(w_mix["w_o"], layer, n_load)]
        + _ffn_weight_specs(w_ffn, layer, n_load),
        out_specs=tile,
        out_shape=jax.ShapeDtypeStruct((t, d), F32),
        scratch_shapes=[pltpu.VMEM((d, d), BF16), pltpu.VMEM((d, d), BF16)] + _ffn_scratch(tm, w_ffn),
        compiler_params=_params(("arbitrary",), vmem),
        name="mixer_out",
    )(x, attn, part, g2, w_mix["w_out_attn"], w_mix["w_o"], *_ffn_weight_args(w_ffn))


def kernel(x, ffn1_norm, ffn1_w_gu, ffn1_w_down, mix_norm, w_in, b_forget, b_gate, conv_w, sgu_ln_g, sgu_ln_b,
           sgu_w, sgu_b, q_norm_g, k_norm_g, w_out_conv, w_out_sgu, w_out_attn, w_o, ffn2_norm, ffn2_w_gu,
           ffn2_w_down):
    batch, seq_len, d = x.shape
    depth = w_in.shape[0]
    heads = d // HEAD_DIM
    rows = lambda a: a.reshape(depth, 1, -1)

    ffn1 = {"norm": rows(ffn1_norm), "w_gu": ffn1_w_gu, "w_down": ffn1_w_down}
    ffn2 = {"norm": rows(ffn2_norm), "w_gu": ffn2_w_gu, "w_down": ffn2_w_down}
    mix = {
        "mix_norm": rows(mix_norm),
        "w_in_t": jnp.swapaxes(w_in, 1, 2),
        "b_forget": jnp.pad(rows(b_forget), ((0, 0), (0, 0), (0, V7X_LANES - heads))),
        "b_gate": rows(b_gate),
        "conv_w": conv_w,
        "sgu_ln_g": rows(sgu_ln_g),
        "sgu_ln_b": rows(sgu_ln_b),
        "sgu_w": sgu_w,
        "sgu_b_t": jnp.swapaxes(sgu_b, 1, 2),
        "q_norm_g": rows(q_norm_g),
        "k_norm_g": rows(k_norm_g),
        "w_out_conv": w_out_conv,
        "w_out_sgu": w_out_sgu,
        "w_out_attn": w_out_attn,
        "w_o": w_o,
    }

    xt = x.reshape(batch * seq_len, d)
    for layer in range(depth):
        xt = _ffn(xt, ffn1, layer)
        part, g2, qaug, kaug, vaug, c_rows = _mixer_in(xt, mix, layer, seq_len=seq_len)
        attn = _attention(qaug, kaug, vaug, c_rows, batch=batch, seq_len=seq_len)
        xt = _mixer_out(xt, attn, part, g2, mix, ffn2, layer)
    return xt.reshape(batch, seq_len, d)
```

```python
import functools
import math

import jax
import jax.numpy as jnp
from jax import lax
from jax.experimental import pallas as pl
from jax.experimental.pallas import tpu as pltpu

F32 = jnp.float32
BF16 = jnp.bfloat16

RMS_EPS = 1e-6
LN_EPS = 1e-5
SGU_CHUNK = 128
HEAD_DIM = 128
N_BRANCH = 3
MASKED = -1e30
FORGET_SPLIT = 3
ONES_ROWS = 16

V7X_LANES = 128
V7X_SUBLANES = 8
BF16_SUBLANES = 16
V7X_VMEM_BYTES = 64 * 1024 * 1024
LOAD_STEPS = 16
LOAD_STEPS_FFN = 8
FF_CHUNK = 1024
GATE_CHUNK = 256

TOKEN_TILE_FFN = 1024
TOKEN_TILE_OUT = 512
SEQ_BLOCK = 256


def _layer(arr, layer):
    index = (layer,) + (0,) * (arr.ndim - 1)
    return pl.BlockSpec((None,) + tuple(arr.shape[1:]), lambda *_: index, pipeline_mode=pl.Buffered(1))


def _weight_chunk(arr, layer, n_load):
    rows = arr.shape[1] // n_load
    assert rows * n_load == arr.shape[1] and rows % BF16_SUBLANES == 0, (arr.shape, n_load)
    return pl.BlockSpec((None, rows, arr.shape[2]), lambda s: (layer, jnp.minimum(s, n_load - 1), 0))


def _token_tile(block, n_load):
    return pl.BlockSpec(block, lambda s: (jnp.maximum(s - n_load, 0),) + (0,) * (len(block) - 1))


def _stash_rows(src_ref, dst_ref, step):
    rows = src_ref.shape[0]
    dst_ref[pl.ds(pl.multiple_of(step * rows, rows), rows), :] = src_ref[...].astype(BF16)


def _nbytes(shape, dtype):
    return math.prod(shape) * jnp.dtype(dtype).itemsize


def _params(semantics, vmem_bytes):
    assert vmem_bytes <= V7X_VMEM_BYTES, vmem_bytes
    return pltpu.CompilerParams(dimension_semantics=semantics, vmem_limit_bytes=int(vmem_bytes))


def _rms_norm(x, g):
    return x * lax.rsqrt(jnp.mean(x * x, axis=-1, keepdims=True) + RMS_EPS) * g


def _gelu(x):
    return 0.5 * x * (1.0 + lax.erf(x * (2.0 ** -0.5)))


def _ff_chunks(d_ff):
    return [(c, min(c + FF_CHUNK, d_ff)) for c in range(0, d_ff, FF_CHUNK)]


def _row_groups(x):
    return [x[r:r + V7X_SUBLANES] for r in range(0, x.shape[0], V7X_SUBLANES)]


def _swiglu_half_step(x, g_ref, wgu_ref, wd_ref, act_ref):
    d_ff = wd_ref.shape[0]
    h = _rms_norm(x, g_ref[...]).astype(BF16)
    for c0, c1 in _ff_chunks(d_ff):
        g = jnp.dot(h, wgu_ref[:, c0:c1], preferred_element_type=F32)
        u = jnp.dot(h, wgu_ref[:, d_ff + c0:d_ff + c1], preferred_element_type=F32)
        act_ref[:, c0:c1] = (g * jax.nn.sigmoid(g) * u).astype(BF16)
    return x + 0.5 * jnp.dot(act_ref[...], wd_ref[...], preferred_element_type=F32)


def _ffn_weight_specs(w, layer, n_load):
    return [_layer(w["norm"], layer), _weight_chunk(w["w_gu"], layer, n_load),
            _weight_chunk(w["w_down"], layer, n_load)]


def _ffn_weight_args(w):
    return [w["norm"], w["w_gu"], w["w_down"]]


def _ffn_scratch(tm, w):
    d, d_gu = w["w_gu"].shape[1:]
    d_ff = w["w_down"].shape[1]
    return [pltpu.VMEM((d, d_gu), BF16), pltpu.VMEM((d_ff, d), BF16), pltpu.VMEM((tm, d_ff), BF16)]


def _ffn_vmem(tm, d, d_ff, n_load):
    return (_nbytes((d, 2 * d_ff), BF16) + _nbytes((d_ff, d), BF16)
            + 2 * (_nbytes((d, 2 * d_ff), F32) + _nbytes((d_ff, d), F32)) // n_load
            + _nbytes((tm, d_ff), BF16)
            + 4 * _nbytes((tm, FF_CHUNK), F32))


def _ffn_kernel(x_ref, g_ref, wgu32_ref, wd32_ref, o_ref, wgu_ref, wd_ref, act_ref, *, n_load):
    step = pl.program_id(0)

    @pl.when(step < n_load)
    def _():
        _stash_rows(wgu32_ref, wgu_ref, step)
        _stash_rows(wd32_ref, wd_ref, step)

    @pl.when(step >= n_load)
    def _():
        o_ref[...] = _swiglu_half_step(x_ref[...], g_ref, wgu_ref, wd_ref, act_ref)


def _ffn(x, w, layer):
    t, d = x.shape
    d_ff = w["w_down"].shape[1]
    tm = TOKEN_TILE_FFN
    n_load = LOAD_STEPS_FFN
    assert t % tm == 0
    tile = _token_tile((tm, d), n_load)
    vmem = _ffn_vmem(tm, d, d_ff, n_load) + 4 * _nbytes((tm, d), F32)
    return pl.pallas_call(
        functools.partial(_ffn_kernel, n_load=n_load),
        grid=(n_load + t // tm,),
        in_specs=[tile] + _ffn_weight_specs(w, layer, n_load),
        out_specs=tile,
        out_shape=jax.ShapeDtypeStruct((t, d), F32),
        scratch_shapes=_ffn_scratch(tm, w),
        compiler_params=_params(("arbitrary",), vmem),
        name="ffn",
    )(x, *_ffn_weight_args(w))


def _mixer_in_kernel(x_ref, ng_ref, wtm32_ref, wtf32_ref, wtg32_ref, woc32_ref, wos32_ref, bf_ref, bg_ref, cw_ref,
                     lng_ref, lnb_ref, sw_ref, sb_ref, qg_ref, kg_ref,
                     part_ref, g2_ref, qaug_ref, kaug_ref, vaug_ref, crow_ref,
                     wm_ref, wf_ref, wgt_ref, woc_ref, wos_ref, zs_ref, ccarry_ref, vn_ref, yb_ref, *,
                     n_load, tiles_per_seq, q_scale, c_scale):
    step = pl.program_id(0)

    @pl.when(step < n_load)
    def _():
        wm_ref[step] = wtm32_ref[...].T.astype(BF16)
        _stash_rows(woc32_ref, woc_ref, step)
        _stash_rows(wos32_ref, wos_ref, step)

    @pl.when(step < wgt_ref.shape[0])
    def _():
        wgt_ref[step] = wtg32_ref[...].T.astype(BF16)

    @pl.when(step == 0)
    def _():
        rows = wtf32_ref[...]
        slab = jnp.concatenate([rows, jnp.zeros((V7X_LANES - rows.shape[0], rows.shape[1]), F32)], axis=0)
        wf_ref[...] = slab.T.astype(BF16)

    @pl.when(step >= n_load)
    def _():
        _mixer_in_tile(step - n_load, x_ref, ng_ref, wm_ref, wf_ref, wgt_ref, bf_ref, bg_ref, cw_ref, lng_ref,
                       lnb_ref, sw_ref, sb_ref, qg_ref, kg_ref, woc_ref, wos_ref,
                       part_ref, g2_ref, qaug_ref, kaug_ref, vaug_ref, crow_ref,
                       zs_ref, ccarry_ref, vn_ref, yb_ref,
                       tiles_per_seq=tiles_per_seq, q_scale=q_scale, c_scale=c_scale)


def _mixer_in_tile(tile_idx, x_ref, ng_ref, wm_ref, wf_ref, wgt_ref, bf_ref, bg_ref, cw_ref, lng_ref, lnb_ref,
                   sw_ref, sb_ref, qg_ref, kg_ref, woc_ref, wos_ref,
                   part_ref, g2_ref, qaug_ref, kaug_ref, vaug_ref, crow_ref,
                   zs_ref, ccarry_ref, vn_ref, yb_ref, *, tiles_per_seq, q_scale, c_scale):
    tm, d = x_ref.shape
    heads = d // HEAD_DIM
    pad = V7X_SUBLANES

    @pl.when(tile_idx % tiles_per_seq == 0)
    def _():
        zs_ref[0:pad, :] = jnp.zeros((pad, d), F32)
        ccarry_ref[...] = jnp.zeros_like(ccarry_ref)

    h = _rms_norm(x_ref[...], ng_ref[...]).astype(BF16)

    def proj(w_ref, c0, c1):
        width = w_ref.shape[2]
        assert c0 % width == 0 and c1 % width == 0
        return jnp.concatenate([jnp.dot(h, w_ref[j], preferred_element_type=F32)
                                for j in range(c0 // width, c1 // width)], axis=1)

    pa = proj(wm_ref, 0, 3 * d)
    f = jnp.dot(h, wf_ref[...], preferred_element_type=F32) + bf_ref[...]
    ps = proj(wm_ref, 3 * d, 5 * d)
    pq = proj(wm_ref, 5 * d, 8 * d)

    zs_ref[pad:pad + tm, :] = pa[:, d:2 * d] * pa[:, 2 * d:3 * d]
    conv = (cw_ref[0:1, :] * zs_ref[pad - 2:pad - 2 + tm, :]
            + cw_ref[1:2, :] * zs_ref[pad - 1:pad - 1 + tm, :]
            + cw_ref[2:3, :] * zs_ref[pad:pad + tm, :])
    ya_in = (pa[:, 0:d] * conv).astype(BF16)
    zs_ref[0:pad, :] = zs_ref[tm:tm + pad, :]

    c = jnp.minimum(f, 0.0) - jnp.log1p(jnp.exp(-jnp.abs(f)))
    t_idx = lax.broadcasted_iota(jnp.int32, c.shape, 0)
    shift = 1
    while shift < tm:
        c = c + jnp.where(t_idx >= shift, pltpu.roll(c, shift, axis=0), 0.0)
        shift *= 2
    c = c + ccarry_ref[0:1, :]
    ccarry_ref[0:1, :] = c[tm - 1:tm, :]
    c = c * c_scale
    crow_ref[...] = c.T[0:heads, :]

    u = _gelu(ps[:, 0:d])
    vv = _gelu(ps[:, d:2 * d])
    mu = jnp.mean(vv, axis=-1, keepdims=True)
    vc = vv - mu
    var = jnp.mean(vc * vc, axis=-1, keepdims=True)
    vn_ref[...] = (vc * lax.rsqrt(var + LN_EPS) * lng_ref[...] + lnb_ref[...]).astype(BF16)

    lane = lax.broadcasted_iota(jnp.int32, (tm, HEAD_DIM), 1)
    feature = lax.broadcasted_iota(jnp.int32, (HEAD_DIM, tm), 0)
    ones_rows = jnp.where(feature < FORGET_SPLIT, 1.0, 0.0).astype(BF16)
    qn = []
    for g in range(heads):
        hs = slice(g * HEAD_DIM, (g + 1) * HEAD_DIM)
        qn.append(_rms_norm(pq[:, hs], qg_ref[:, hs]) * q_scale)
        ks = slice(d + g * HEAD_DIM, d + (g + 1) * HEAD_DIM)
        kaug_ref[:, 2 * g * HEAD_DIM:(2 * g + 1) * HEAD_DIM] = _rms_norm(pq[:, ks], kg_ref[:, hs]).astype(BF16)
        rest = -jnp.broadcast_to(c[:, g:g + 1], (tm, HEAD_DIM))
        slab = jnp.zeros((tm, HEAD_DIM), F32)
        for term in range(FORGET_SPLIT):
            piece = rest.astype(BF16).astype(F32)
            slab = jnp.where(lane == term, piece, slab)
            rest = rest - piece
        kaug_ref[:, (2 * g + 1) * HEAD_DIM:(2 * g + 2) * HEAD_DIM] = slab.astype(BF16)
    qt = jnp.concatenate(qn, axis=1).T.astype(BF16)
    vt = pq[:, 2 * d:3 * d].T.astype(BF16)
    v_rows = HEAD_DIM + ONES_ROWS
    for g in range(heads):
        hs = slice(g * HEAD_DIM, (g + 1) * HEAD_DIM)
        qaug_ref[2 * g * HEAD_DIM:(2 * g + 1) * HEAD_DIM, :] = qt[hs, :]
        qaug_ref[(2 * g + 1) * HEAD_DIM:(2 * g + 2) * HEAD_DIM, :] = ones_rows
        vaug_ref[g * v_rows:g * v_rows + HEAD_DIM, :] = vt[hs, :]
        vaug_ref[g * v_rows + HEAD_DIM:(g + 1) * v_rows, :] = jnp.ones((ONES_ROWS, tm), BF16)

    ya = jnp.dot(ya_in, woc_ref[...], preferred_element_type=F32)
    n_chunks = tm // SGU_CHUNK
    pos_t = lax.broadcasted_iota(jnp.int32, (SGU_CHUNK, SGU_CHUNK), 0)
    pos_s = lax.broadcasted_iota(jnp.int32, (SGU_CHUNK, SGU_CHUNK), 1)
    for g in range(heads):
        hs = slice(g * HEAD_DIM, (g + 1) * HEAD_DIM)
        w = jnp.where(pos_s <= pos_t, sw_ref[g], 0.0).astype(BF16)
        rhs = jnp.concatenate([vn_ref[c0 * SGU_CHUNK:(c0 + 1) * SGU_CHUNK, hs] for c0 in range(n_chunks)], axis=1)
        s = jnp.dot(w, rhs, preferred_element_type=F32) + sb_ref[:, g:g + 1]
        for c0 in range(n_chunks):
            rows = slice(c0 * SGU_CHUNK, (c0 + 1) * SGU_CHUNK)
            yb_ref[rows, hs] = (u[rows, hs] * s[:, c0 * SGU_CHUNK:(c0 + 1) * SGU_CHUNK]).astype(BF16)
    gates = [jax.nn.sigmoid(proj(wgt_ref, b * d, (b + 1) * d) + bg_ref[:, b * d:(b + 1) * d])
             for b in range(N_BRANCH)]
    yb = jnp.dot(yb_ref[...], wos_ref[...], preferred_element_type=F32)
    part_ref[...] = gates[0] * ya + gates[1] * yb
    g2_ref[...] = gates[2]


def _mixer_in(x, w, layer, *, seq_len):
    t, d = x.shape
    tm = SEQ_BLOCK
    heads = d // HEAD_DIM
    assert seq_len % tm == 0 and tm % SGU_CHUNK == 0 and d % HEAD_DIM == 0 and heads <= V7X_SUBLANES
    n_load = LOAD_STEPS
    w_t = w["w_in_t"]
    n_in = w_t.shape[1]
    n_main = 8 * d
    main_rows = n_main // n_load
    n_gate = N_BRANCH * d // GATE_CHUNK
    assert n_in == n_main + heads + N_BRANCH * d and main_rows * n_load == n_main and main_rows % V7X_LANES == 0
    assert n_gate * GATE_CHUNK == N_BRANCH * d and n_gate <= n_load and d % GATE_CHUNK == 0
    kernel = functools.partial(
        _mixer_in_kernel, n_load=n_load, tiles_per_seq=seq_len // tm,
        q_scale=HEAD_DIM ** -0.5 * math.log2(math.e), c_scale=math.log2(math.e))
    main_spec = pl.BlockSpec((None, main_rows, d), lambda s: (layer, jnp.minimum(s, n_load - 1), 0))
    forget_spec = pl.BlockSpec((pl.Squeezed(), pl.Element(V7X_SUBLANES), pl.Element(d)),
                               lambda s: (layer, n_main, 0))
    gate_align = math.gcd(n_main + heads, GATE_CHUNK, V7X_SUBLANES)
    gate_spec = pl.BlockSpec(
        (pl.Squeezed(), pl.Element(GATE_CHUNK), pl.Element(d)),
        lambda s: (layer, pl.multiple_of(n_main + heads + GATE_CHUNK * jnp.minimum(s, n_gate - 1), gate_align), 0))
    streamed = ["w_out_conv", "w_out_sgu"]
    small = ["b_forget", "b_gate", "conv_w", "sgu_ln_g", "sgu_ln_b", "sgu_w", "sgu_b_t", "q_norm_g", "k_norm_g"]
    weight_specs = ([_layer(w["mix_norm"], layer), main_spec, forget_spec, gate_spec]
                    + [_weight_chunk(w[n], layer, n_load) for n in streamed] + [_layer(w[n], layer) for n in small])
    weight_args = [w["mix_norm"], w_t, w_t, w_t] + [w[n] for n in streamed] + [w[n] for n in small]
    v_rows = heads * (HEAD_DIM + ONES_ROWS)
    tile = _token_tile((tm, d), n_load)
    resident = [pltpu.VMEM((n_load, d, main_rows), BF16), pltpu.VMEM((d, V7X_LANES), BF16),
                pltpu.VMEM((n_gate, d, GATE_CHUNK), BF16), pltpu.VMEM((d, d), BF16), pltpu.VMEM((d, d), BF16)]
    scratch = resident + [pltpu.VMEM((tm + V7X_SUBLANES, d), F32), pltpu.VMEM((V7X_SUBLANES, V7X_LANES), F32),
                          pltpu.VMEM((tm, d), BF16), pltpu.VMEM((tm, d), BF16)]
    vmem = (sum(_nbytes(s.shape, s.dtype) for s in scratch)
            + 2 * _nbytes((main_rows + GATE_CHUNK + V7X_SUBLANES + 2 * d // n_load, d), F32)
            + 2 * _nbytes((main_rows, d), F32)
            + sum(_nbytes(w[n].shape[1:], w[n].dtype) for n in small)
            + 2 * (3 * _nbytes((tm, d), F32) + 4 * _nbytes((tm, d), BF16) + _nbytes((v_rows, tm), BF16)
                   + _nbytes((V7X_SUBLANES, tm), F32))
            + 10 * _nbytes((tm, d), F32))
    return pl.pallas_call(
        kernel,
        grid=(n_load + t // tm,),
        in_specs=[tile] + weight_specs,
        out_specs=[tile, tile,
                   _token_tile((None, 2 * d, tm), n_load),
                   _token_tile((tm, 2 * d), n_load),
                   _token_tile((None, v_rows, tm), n_load),
                   _token_tile((None, heads, tm), n_load)],
        out_shape=[jax.ShapeDtypeStruct((t, d), F32), jax.ShapeDtypeStruct((t, d), F32),
                   jax.ShapeDtypeStruct((t // tm, 2 * d, tm), BF16), jax.ShapeDtypeStruct((t, 2 * d), BF16),
                   jax.ShapeDtypeStruct((t // tm, v_rows, tm), BF16),
                   jax.ShapeDtypeStruct((t // tm, heads, tm), F32)],
        scratch_shapes=scratch,
        compiler_params=_params(("arbitrary",), vmem),
        name="mixer_in",
    )(x, *weight_args)


def _attn_kernel(qaug_ref, kaug_ref, vaug_ref, crow_ref, o_ref, s0_ref, s1_ref, m_ref, acc_ref):
    tq = qaug_ref.shape[2]
    heads = m_ref.shape[0]
    g = pl.program_id(1)
    ia, ib = 2 * g, 2 * g + 1
    head_cols = [slice(h * HEAD_DIM, (h + 1) * HEAD_DIM) for h in range(heads)]
    qk_cols = [slice(h * 2 * HEAD_DIM, (h + 1) * 2 * HEAD_DIM) for h in range(heads)]
    v_rows = [slice(h * (HEAD_DIM + ONES_ROWS), (h + 1) * (HEAD_DIM + ONES_ROWS)) for h in range(heads)]
    key_pos = lax.broadcasted_iota(jnp.int32, (tq, tq), 0)
    query_pos = lax.broadcasted_iota(jnp.int32, (tq, tq), 1)
    causal = key_pos <= query_pos

    s_refs = (s0_ref, s1_ref)

    def reset():
        m_ref[...] = jnp.full(m_ref.shape, MASKED, F32)
        acc_ref[...] = jnp.zeros(acc_ref.shape, F32)

    def logits(qb, j, slot, diagonal):
        rows = pl.ds(pl.multiple_of(j * tq, tq), tq)
        for h in range(heads):
            s = jnp.dot(kaug_ref[rows, qk_cols[h]], qaug_ref[qb, qk_cols[h], :], preferred_element_type=F32)
            s_refs[slot][h] = jnp.where(causal, s, MASKED) if diagonal else s

    def softmax_pv(i, j, slot):
        probs, rescale = [], []
        for h in range(heads):
            s = s_refs[slot][h]
            cq = crow_ref[i, h:h + 1, :]
            m_old = m_ref[h]
            block_max = jnp.max(functools.reduce(jnp.maximum, _row_groups(s)), axis=0, keepdims=True)
            m_new = jnp.maximum(m_old, block_max + cq)
            m_ref[h] = m_new
            probs.append(jnp.exp2(s + (cq - m_new)).astype(BF16))
            rescale.append(jnp.exp2(m_old - m_new))
        for h in range(heads):
            acc_ref[h] = rescale[h] * acc_ref[h] + jnp.dot(vaug_ref[j, v_rows[h], :], probs[h],
                                                           preferred_element_type=F32)

    def finish(qb):
        for h, hs in enumerate(head_cols):
            row_sum = acc_ref[h, HEAD_DIM:HEAD_DIM + 1, :]
            o_ref[qb * tq:(qb + 1) * tq, hs] = (acc_ref[h, 0:HEAD_DIM, :] * (1.0 / row_sum)).T.astype(BF16)

    def stage(qb_next, j_next, slot_next, diagonal, i, j, slot):
        logits(qb_next, j_next, slot_next, diagonal)
        softmax_pv(i, j, slot)

    reset()

    @pl.when(g == 0)
    def _():
        logits(0, 0, 0, True)
        stage(1, 0, 1, False, ia, 0, 0)
        finish(0)
        reset()
        stage(1, 1, 0, True, ib, 0, 1)
        softmax_pv(ib, 1, 0)
        finish(1)

    @pl.when(g > 0)
    def _():
        logits(0, 0, 0, False)

        def pair_a(t, carry):
            j = 2 * t
            stage(0, j + 1, 1, False, ia, j, 0)
            stage(0, j + 2, 0, False, ia, j + 1, 1)
            return carry

        lax.fori_loop(0, g - 1, pair_a, 0)
        stage(0, ia - 1, 1, False, ia, ia - 2, 0)
        stage(0, ia, 0, True, ia, ia - 1, 1)
        stage(1, 0, 1, False, ia, ia, 0)
        finish(0)
        reset()

        def pair_b(t, carry):
            j = 2 * t
            stage(1, j + 1, 0, False, ib, j, 1)
            stage(1, j + 2, 1, False, ib, j + 1, 0)
            return carry

        lax.fori_loop(0, g, pair_b, 0)
        stage(1, ib, 0, True, ib, ia, 1)
        softmax_pv(ib, ib, 0)
        finish(1)


def _attention(qaug, kaug, vaug, c_rows, *, batch, seq_len):
    nt, qk_rows, tq = qaug.shape
    nq = seq_len // tq
    heads = qk_rows // (2 * HEAD_DIM)
    d = heads * HEAD_DIM
    v_rows = vaug.shape[1]
    assert nt == batch * nq and tq % V7X_LANES == 0 and v_rows == heads * (HEAD_DIM + ONES_ROWS)
    assert nq % 2 == 0
    steps = nq // 2
    scratch = [pltpu.VMEM((heads, tq, tq), F32), pltpu.VMEM((heads, tq, tq), F32),
               pltpu.VMEM((heads, 1, tq), F32),
               pltpu.VMEM((heads, HEAD_DIM + ONES_ROWS, tq), F32)]
    vmem = (2 * (_nbytes((seq_len, qk_rows), BF16) + _nbytes((nq, v_rows, tq), BF16)
                 + 2 * _nbytes((qk_rows, tq), BF16) + 2 * _nbytes((tq, d), BF16)
                 + _nbytes((nq, V7X_SUBLANES, tq), F32))
            + sum(_nbytes(s.shape, s.dtype) for s in scratch)
            + _nbytes((V7X_SUBLANES * heads, tq), F32)
            + 16 * _nbytes((tq, tq), F32))
    return pl.pallas_call(
        _attn_kernel,
        grid=(batch, steps),
        in_specs=[pl.BlockSpec((2, qk_rows, tq), lambda b, g: (b * steps + g, 0, 0)),
                  pl.BlockSpec((seq_len, qk_rows), lambda b, g: (b, 0)),
                  pl.BlockSpec((nq, v_rows, tq), lambda b, g: (b, 0, 0)),
                  pl.BlockSpec((nq, heads, tq), lambda b, g: (b, 0, 0))],
        out_specs=pl.BlockSpec((2 * tq, d), lambda b, g: (b * steps + g, 0)),
        out_shape=jax.ShapeDtypeStruct((nt * tq, d), BF16),
        scratch_shapes=scratch,
        compiler_params=_params(("arbitrary", "arbitrary"), vmem),
        name="attention",
    )(qaug, kaug, vaug, c_rows)


def _mixer_out_kernel(x_ref, a_ref, part_ref, g2_ref, woa32_ref, wo32_ref, g_ref, wgu32_ref, wd32_ref,
                      o_ref, woa_ref, wo_ref, wgu_ref, wd_ref, act_ref, *, n_load):
    step = pl.program_id(0)

    @pl.when(step < n_load)
    def _():
        _stash_rows(woa32_ref, woa_ref, step)
        _stash_rows(wo32_ref, wo_ref, step)
        _stash_rows(wgu32_ref, wgu_ref, step)
        _stash_rows(wd32_ref, wd_ref, step)

    @pl.when(step >= n_load)
    def _():
        yc = jnp.dot(a_ref[...], woa_ref[...], preferred_element_type=F32)
        merged = part_ref[...] + g2_ref[...] * yc
        x = x_ref[...] + jnp.dot(merged.astype(BF16), wo_ref[...], preferred_element_type=F32)
        o_ref[...] = _swiglu_half_step(x, g_ref, wgu_ref, wd_ref, act_ref)


def _mixer_out(x, attn, part, g2, w_mix, w_ffn, layer):
    t, d = x.shape
    d_ff = w_ffn["w_down"].shape[1]
    tm = TOKEN_TILE_OUT
    n_load = LOAD_STEPS_FFN
    assert t % tm == 0
    tile = _token_tile((tm, d), n_load)
    vmem = (_ffn_vmem(tm, d, d_ff, n_load) + 2 * _nbytes((d, d), BF16) + 4 * _nbytes((d, d), F32) // n_load
            + 2 * (4 * _nbytes((tm, d), F32) + _nbytes((tm, d), BF16)) + 2 * _nbytes((tm, d), F32))
    return pl.pallas_call(
        functools.partial(_mixer_out_kernel, n_load=n_load),
        grid=(n_load + t // tm,),
        in_specs=[tile, tile, tile, tile,
                  _weight_chunk(w_mix["w_out_attn"], layer, n_load), _weight_chunk(w_mix["w_o"], layer, n_load)]
        + _ffn_weight_specs(w_ffn, layer, n_load),
        out_specs=tile,
        out_shape=jax.ShapeDtypeStruct((t, d), F32),
        scratch_shapes=[pltpu.VMEM((d, d), BF16), pltpu.VMEM((d, d), BF16)] + _ffn_scratch(tm, w_ffn),
        compiler_params=_params(("arbitrary",), vmem),
        name="mixer_out",
    )(x, attn, part, g2, w_mix["w_out_attn"], w_mix["w_o"], *_ffn_weight_args(w_ffn))


def kernel(x, ffn1_norm, ffn1_w_gu, ffn1_w_down, mix_norm, w_in, b_forget, b_gate, conv_w, sgu_ln_g, sgu_ln_b,
           sgu_w, sgu_b, q_norm_g, k_norm_g, w_out_conv, w_out_sgu, w_out_attn, w_o, ffn2_norm, ffn2_w_gu,
           ffn2_w_down):
    batch, seq_len, d = x.shape
    depth = w_in.shape[0]
    heads = d // HEAD_DIM
    rows = lambda a: a.reshape(depth, 1, -1)

    ffn1 = {"norm": rows(ffn1_norm), "w_gu": ffn1_w_gu, "w_down": ffn1_w_down}
    ffn2 = {"norm": rows(ffn2_norm), "w_gu": ffn2_w_gu, "w_down": ffn2_w_down}
    mix = {
        "mix_norm": rows(mix_norm),
        "w_in_t": jnp.swapaxes(w_in, 1, 2),
        "b_forget": jnp.pad(rows(b_forget), ((0, 0), (0, 0), (0, V7X_LANES - heads))),
        "b_gate": rows(b_gate),
        "conv_w": conv_w,
        "sgu_ln_g": rows(sgu_ln_g),
        "sgu_ln_b": rows(sgu_ln_b),
        "sgu_w": sgu_w,
        "sgu_b_t": jnp.swapaxes(sgu_b, 1, 2),
        "q_norm_g": rows(q_norm_g),
        "k_norm_g": rows(k_norm_g),
        "w_out_conv": w_out_conv,
        "w_out_sgu": w_out_sgu,
        "w_out_attn": w_out_attn,
        "w_o": w_o,
    }

    xt = x.reshape(batch * seq_len, d)
    for layer in range(depth):
        xt = _ffn(xt, ffn1, layer)
        part, g2, qaug, kaug, vaug, c_rows = _mixer_in(xt, mix, layer, seq_len=seq_len)
        attn = _attention(qaug, kaug, vaug, c_rows, batch=batch, seq_len=seq_len)
        xt = _mixer_out(xt, attn, part, g2, mix, ffn2, layer)
    return xt.reshape(batch, seq_len, d)
```

```python
import functools
import math

import jax
import jax.numpy as jnp
from jax import lax
from jax.experimental import pallas as pl
from jax.experimental.pallas import tpu as pltpu

F32 = jnp.float32
BF16 = jnp.bfloat16

RMS_EPS = 1e-6
LN_EPS = 1e-5
SGU_CHUNK = 128
HEAD_DIM = 128
N_BRANCH = 3
MASKED = -1e30
FORGET_SPLIT = 3
ONES_ROWS = 16

V7X_LANES = 128
V7X_SUBLANES = 8
BF16_SUBLANES = 16
V7X_VMEM_BYTES = 64 * 1024 * 1024
LOAD_STEPS = 16
LOAD_STEPS_FFN = 8
FF_CHUNK = 1024
GATE_CHUNK = 256

TOKEN_TILE_FFN = 1024
TOKEN_TILE_OUT = 512
SEQ_BLOCK = 256


def _layer(arr, layer):
    index = (layer,) + (0,) * (arr.ndim - 1)
    return pl.BlockSpec((None,) + tuple(arr.shape[1:]), lambda *_: index, pipeline_mode=pl.Buffered(1))


def _weight_chunk(arr, layer, n_load):
    rows = arr.shape[1] // n_load
    assert rows * n_load == arr.shape[1] and rows % BF16_SUBLANES == 0, (arr.shape, n_load)
    return pl.BlockSpec((None, rows, arr.shape[2]), lambda s: (layer, jnp.minimum(s, n_load - 1), 0))


def _token_tile(block, n_load):
    return pl.BlockSpec(block, lambda s: (jnp.maximum(s - n_load, 0),) + (0,) * (len(block) - 1))


def _stash_rows(src_ref, dst_ref, step):
    rows = src_ref.shape[0]
    dst_ref[pl.ds(pl.multiple_of(step * rows, rows), rows), :] = src_ref[...].astype(BF16)


def _nbytes(shape, dtype):
    return math.prod(shape) * jnp.dtype(dtype).itemsize


def _params(semantics, vmem_bytes):
    assert vmem_bytes <= V7X_VMEM_BYTES, vmem_bytes
    return pltpu.CompilerParams(dimension_semantics=semantics, vmem_limit_bytes=int(vmem_bytes))


def _rms_norm(x, g):
    return x * lax.rsqrt(jnp.mean(x * x, axis=-1, keepdims=True) + RMS_EPS) * g


def _gelu(x):
    return 0.5 * x * (1.0 + lax.erf(x * (2.0 ** -0.5)))


def _ff_chunks(d_ff):
    return [(c, min(c + FF_CHUNK, d_ff)) for c in range(0, d_ff, FF_CHUNK)]


def _row_groups(x):
    return [x[r:r + V7X_SUBLANES] for r in range(0, x.shape[0], V7X_SUBLANES)]


def _swiglu_half_step(x, g_ref, wgu_ref, wd_ref, act_ref):
    d_ff = wd_ref.shape[0]
    h = _rms_norm(x, g_ref[...]).astype(BF16)
    for c0, c1 in _ff_chunks(d_ff):
        g = jnp.dot(h, wgu_ref[:, c0:c1], preferred_element_type=F32)
        u = jnp.dot(h, wgu_ref[:, d_ff + c0:d_ff + c1], preferred_element_type=F32)
        act_ref[:, c0:c1] = (g * jax.nn.sigmoid(g) * u).astype(BF16)
    return x + 0.5 * jnp.dot(act_ref[...], wd_ref[...], preferred_element_type=F32)


def _ffn_weight_specs(w, layer, n_load):
    return [_layer(w["norm"], layer), _weight_chunk(w["w_gu"], layer, n_load),
            _weight_chunk(w["w_down"], layer, n_load)]


def _ffn_weight_args(w):
    return [w["norm"], w["w_gu"], w["w_down"]]


def _ffn_scratch(tm, w):
    d, d_gu = w["w_gu"].shape[1:]
    d_ff = w["w_down"].shape[1]
    return [pltpu.VMEM((d, d_gu), BF16), pltpu.VMEM((d_ff, d), BF16), pltpu.VMEM((tm, d_ff), BF16)]


def _ffn_vmem(tm, d, d_ff, n_load):
    return (_nbytes((d, 2 * d_ff), BF16) + _nbytes((d_ff, d), BF16)
            + 2 * (_nbytes((d, 2 * d_ff), F32) + _nbytes((d_ff, d), F32)) // n_load
            + _nbytes((tm, d_ff), BF16)
            + 4 * _nbytes((tm, FF_CHUNK), F32))


def _ffn_kernel(x_ref, g_ref, wgu32_ref, wd32_ref, o_ref, wgu_ref, wd_ref, act_ref, *, n_load):
    step = pl.program_id(0)

    @pl.when(step < n_load)
    def _():
        _stash_rows(wgu32_ref, wgu_ref, step)
        _stash_rows(wd32_ref, wd_ref, step)

    @pl.when(step >= n_load)
    def _():
        o_ref[...] = _swiglu_half_step(x_ref[...], g_ref, wgu_ref, wd_ref, act_ref)


def _ffn(x, w, layer):
    t, d = x.shape
    d_ff = w["w_down"].shape[1]
    tm = TOKEN_TILE_FFN
    n_load = LOAD_STEPS_FFN
    assert t % tm == 0
    tile = _token_tile((tm, d), n_load)
    vmem = _ffn_vmem(tm, d, d_ff, n_load) + 4 * _nbytes((tm, d), F32)
    return pl.pallas_call(
        functools.partial(_ffn_kernel, n_load=n_load),
        grid=(n_load + t // tm,),
        in_specs=[tile] + _ffn_weight_specs(w, layer, n_load),
        out_specs=tile,
        out_shape=jax.ShapeDtypeStruct((t, d), F32),
        scratch_shapes=_ffn_scratch(tm, w),
        compiler_params=_params(("arbitrary",), vmem),
        name="ffn",
    )(x, *_ffn_weight_args(w))


def _mixer_in_kernel(x_ref, ng_ref, wtm32_ref, wtf32_ref, wtg32_ref, woc32_ref, wos32_ref, bf_ref, bg_ref, cw_ref,
                     lng_ref, lnb_ref, sw_ref, sb_ref, qg_ref, kg_ref,
                     part_ref, g2_ref, qaug_ref, kaug_ref, vaug_ref, crow_ref,
                     wm_ref, wf_ref, wgt_ref, woc_ref, wos_ref, zs_ref, ccarry_ref, vn_ref, yb_ref, *,
                     n_load, tiles_per_seq, q_scale, c_scale):
    step = pl.program_id(0)

    @pl.when(step < n_load)
    def _():
        wm_ref[step] = wtm32_ref[...].T.astype(BF16)
        _stash_rows(woc32_ref, woc_ref, step)
        _stash_rows(wos32_ref, wos_ref, step)

    @pl.when(step < wgt_ref.shape[0])
    def _():
        wgt_ref[step] = wtg32_ref[...].T.astype(BF16)

    @pl.when(step == 0)
    def _():
        rows = wtf32_ref[...]
        slab = jnp.concatenate([rows, jnp.zeros((V7X_LANES - rows.shape[0], rows.shape[1]), F32)], axis=0)
        wf_ref[...] = slab.T.astype(BF16)

    @pl.when(step >= n_load)
    def _():
        _mixer_in_tile(step - n_load, x_ref, ng_ref, wm_ref, wf_ref, wgt_ref, bf_ref, bg_ref, cw_ref, lng_ref,
                       lnb_ref, sw_ref, sb_ref, qg_ref, kg_ref, woc_ref, wos_ref,
                       part_ref, g2_ref, qaug_ref, kaug_ref, vaug_ref, crow_ref,
                       zs_ref, ccarry_ref, vn_ref, yb_ref,
                       tiles_per_seq=tiles_per_seq, q_scale=q_scale, c_scale=c_scale)


def _mixer_in_tile(tile_idx, x_ref, ng_ref, wm_ref, wf_ref, wgt_ref, bf_ref, bg_ref, cw_ref, lng_ref, lnb_ref,
                   sw_ref, sb_ref, qg_ref, kg_ref, woc_ref, wos_ref,
                   part_ref, g2_ref, qaug_ref, kaug_ref, vaug_ref, crow_ref,
                   zs_ref, ccarry_ref, vn_ref, yb_ref, *, tiles_per_seq, q_scale, c_scale):
    tm, d = x_ref.shape
    heads = d // HEAD_DIM
    pad = V7X_SUBLANES

    @pl.when(tile_idx % tiles_per_seq == 0)
    def _():
        zs_ref[0:pad, :] = jnp.zeros((pad, d), F32)
        ccarry_ref[...] = jnp.zeros_like(ccarry_ref)

    h = _rms_norm(x_ref[...], ng_ref[...]).astype(BF16)

    def proj(w_ref, c0, c1):
        width = w_ref.shape[2]
        assert c0 % width == 0 and c1 % width == 0
        return jnp.concatenate([jnp.dot(h, w_ref[j], preferred_element_type=F32)
                                for j in range(c0 // width, c1 // width)], axis=1)

    pa = proj(wm_ref, 0, 3 * d)
    f = jnp.dot(h, wf_ref[...], preferred_element_type=F32) + bf_ref[...]
    ps = proj(wm_ref, 3 * d, 5 * d)
    pq = proj(wm_ref, 5 * d, 8 * d)

    zs_ref[pad:pad + tm, :] = pa[:, d:2 * d] * pa[:, 2 * d:3 * d]
    conv = (cw_ref[0:1, :] * zs_ref[pad - 2:pad - 2 + tm, :]
            + cw_ref[1:2, :] * zs_ref[pad - 1:pad - 1 + tm, :]
            + cw_ref[2:3, :] * zs_ref[pad:pad + tm, :])
    ya_in = (pa[:, 0:d] * conv).astype(BF16)
    zs_ref[0:pad, :] = zs_ref[tm:tm + pad, :]

    c = jnp.minimum(f, 0.0) - jnp.log1p(jnp.exp(-jnp.abs(f)))
    t_idx = lax.broadcasted_iota(jnp.int32, c.shape, 0)
    shift = 1
    while shift < tm:
        c = c + jnp.where(t_idx >= shift, pltpu.roll(c, shift, axis=0), 0.0)
        shift *= 2
    c = c + ccarry_ref[0:1, :]
    ccarry_ref[0:1, :] = c[tm - 1:tm, :]
    c = c * c_scale
    crow_ref[...] = c.T[0:heads, :]

    u = _gelu(ps[:, 0:d])
    vv = _gelu(ps[:, d:2 * d])
    mu = jnp.mean(vv, axis=-1, keepdims=True)
    vc = vv - mu
    var = jnp.mean(vc * vc, axis=-1, keepdims=True)
    vn_ref[...] = (vc * lax.rsqrt(var + LN_EPS) * lng_ref[...] + lnb_ref[...]).astype(BF16)

    lane = lax.broadcasted_iota(jnp.int32, (tm, HEAD_DIM), 1)
    feature = lax.broadcasted_iota(jnp.int32, (HEAD_DIM, tm), 0)
    ones_rows = jnp.where(feature < FORGET_SPLIT, 1.0, 0.0).astype(BF16)
    pieces, rest = [], -c
    for _ in range(FORGET_SPLIT):
        pieces.append(rest.astype(BF16).astype(F32))
        rest = rest - pieces[-1]
    qn = []
    for g in range(heads):
        hs = slice(g * HEAD_DIM, (g + 1) * HEAD_DIM)
        qn.append(_rms_norm(pq[:, hs], qg_ref[:, hs]) * q_scale)
        ks = slice(d + g * HEAD_DIM, d + (g + 1) * HEAD_DIM)
        kaug_ref[:, 2 * g * HEAD_DIM:(2 * g + 1) * HEAD_DIM] = _rms_norm(pq[:, ks], kg_ref[:, hs]).astype(BF16)
        slab = jnp.zeros((tm, HEAD_DIM), F32)
        for term, piece in enumerate(pieces):
            slab = jnp.where(lane == term, piece[:, g:g + 1], slab)
        kaug_ref[:, (2 * g + 1) * HEAD_DIM:(2 * g + 2) * HEAD_DIM] = slab.astype(BF16)
    qt = jnp.concatenate(qn, axis=1).T.astype(BF16)
    vt = pq[:, 2 * d:3 * d].T.astype(BF16)
    v_rows = HEAD_DIM + ONES_ROWS
    for g in range(heads):
        hs = slice(g * HEAD_DIM, (g + 1) * HEAD_DIM)
        qaug_ref[2 * g * HEAD_DIM:(2 * g + 1) * HEAD_DIM, :] = qt[hs, :]
        qaug_ref[(2 * g + 1) * HEAD_DIM:(2 * g + 2) * HEAD_DIM, :] = ones_rows
        vaug_ref[g * v_rows:g * v_rows + HEAD_DIM, :] = vt[hs, :]
        vaug_ref[g * v_rows + HEAD_DIM:(g + 1) * v_rows, :] = jnp.ones((ONES_ROWS, tm), BF16)

    ya = jnp.dot(ya_in, woc_ref[...], preferred_element_type=F32)
    n_chunks = tm // SGU_CHUNK
    pos_t = lax.broadcasted_iota(jnp.int32, (SGU_CHUNK, SGU_CHUNK), 0)
    pos_s = lax.broadcasted_iota(jnp.int32, (SGU_CHUNK, SGU_CHUNK), 1)
    for g in range(heads):
        hs = slice(g * HEAD_DIM, (g + 1) * HEAD_DIM)
        w = jnp.where(pos_s <= pos_t, sw_ref[g], 0.0).astype(BF16)
        rhs = jnp.concatenate([vn_ref[c0 * SGU_CHUNK:(c0 + 1) * SGU_CHUNK, hs] for c0 in range(n_chunks)], axis=1)
        s = jnp.dot(w, rhs, preferred_element_type=F32) + sb_ref[:, g:g + 1]
        for c0 in range(n_chunks):
            rows = slice(c0 * SGU_CHUNK, (c0 + 1) * SGU_CHUNK)
            yb_ref[rows, hs] = (u[rows, hs] * s[:, c0 * SGU_CHUNK:(c0 + 1) * SGU_CHUNK]).astype(BF16)
    gates = [jax.nn.sigmoid(proj(wgt_ref, b * d, (b + 1) * d) + bg_ref[:, b * d:(b + 1) * d])
             for b in range(N_BRANCH)]
    yb = jnp.dot(yb_ref[...], wos_ref[...], preferred_element_type=F32)
    part_ref[...] = gates[0] * ya + gates[1] * yb
    g2_ref[...] = gates[2]


def _mixer_in(x, w, layer, *, seq_len):
    t, d = x.shape
    tm = SEQ_BLOCK
    heads = d // HEAD_DIM
    assert seq_len % tm == 0 and tm % SGU_CHUNK == 0 and d % HEAD_DIM == 0 and heads <= V7X_SUBLANES
    n_load = LOAD_STEPS
    w_t = w["w_in_t"]
    n_in = w_t.shape[1]
    n_main = 8 * d
    main_rows = n_main // n_load
    n_gate = N_BRANCH * d // GATE_CHUNK
    assert n_in == n_main + heads + N_BRANCH * d and main_rows * n_load == n_main and main_rows % V7X_LANES == 0
    assert n_gate * GATE_CHUNK == N_BRANCH * d and n_gate <= n_load and d % GATE_CHUNK == 0
    kernel = functools.partial(
        _mixer_in_kernel, n_load=n_load, tiles_per_seq=seq_len // tm,
        q_scale=HEAD_DIM ** -0.5 * math.log2(math.e), c_scale=math.log2(math.e))
    main_spec = pl.BlockSpec((None, main_rows, d), lambda s: (layer, jnp.minimum(s, n_load - 1), 0))
    forget_spec = pl.BlockSpec((pl.Squeezed(), pl.Element(V7X_SUBLANES), pl.Element(d)),
                               lambda s: (layer, n_main, 0))
    gate_align = math.gcd(n_main + heads, GATE_CHUNK, V7X_SUBLANES)
    gate_spec = pl.BlockSpec(
        (pl.Squeezed(), pl.Element(GATE_CHUNK), pl.Element(d)),
        lambda s: (layer, pl.multiple_of(n_main + heads + GATE_CHUNK * jnp.minimum(s, n_gate - 1), gate_align), 0))
    streamed = ["w_out_conv", "w_out_sgu"]
    small = ["b_forget", "b_gate", "conv_w", "sgu_ln_g", "sgu_ln_b", "sgu_w", "sgu_b_t", "q_norm_g", "k_norm_g"]
    weight_specs = ([_layer(w["mix_norm"], layer), main_spec, forget_spec, gate_spec]
                    + [_weight_chunk(w[n], layer, n_load) for n in streamed] + [_layer(w[n], layer) for n in small])
    weight_args = [w["mix_norm"], w_t, w_t, w_t] + [w[n] for n in streamed] + [w[n] for n in small]
    v_rows = heads * (HEAD_DIM + ONES_ROWS)
    tile = _token_tile((tm, d), n_load)
    resident = [pltpu.VMEM((n_load, d, main_rows), BF16), pltpu.VMEM((d, V7X_LANES), BF16),
                pltpu.VMEM((n_gate, d, GATE_CHUNK), BF16), pltpu.VMEM((d, d), BF16), pltpu.VMEM((d, d), BF16)]
    scratch = resident + [pltpu.VMEM((tm + V7X_SUBLANES, d), F32), pltpu.VMEM((V7X_SUBLANES, V7X_LANES), F32),
                          pltpu.VMEM((tm, d), BF16), pltpu.VMEM((tm, d), BF16)]
    vmem = (sum(_nbytes(s.shape, s.dtype) for s in scratch)
            + 2 * _nbytes((main_rows + GATE_CHUNK + V7X_SUBLANES + 2 * d // n_load, d), F32)
            + 2 * _nbytes((main_rows, d), F32)
            + sum(_nbytes(w[n].shape[1:], w[n].dtype) for n in small)
            + 2 * (3 * _nbytes((tm, d), F32) + 4 * _nbytes((tm, d), BF16) + _nbytes((v_rows, tm), BF16)
                   + _nbytes((V7X_SUBLANES, tm), F32))
            + 10 * _nbytes((tm, d), F32))
    return pl.pallas_call(
        kernel,
        grid=(n_load + t // tm,),
        in_specs=[tile] + weight_specs,
        out_specs=[tile, tile,
                   _token_tile((None, 2 * d, tm), n_load),
                   _token_tile((tm, 2 * d), n_load),
                   _token_tile((None, v_rows, tm), n_load),
                   _token_tile((None, heads, tm), n_load)],
        out_shape=[jax.ShapeDtypeStruct((t, d), F32), jax.ShapeDtypeStruct((t, d), F32),
                   jax.ShapeDtypeStruct((t // tm, 2 * d, tm), BF16), jax.ShapeDtypeStruct((t, 2 * d), BF16),
                   jax.ShapeDtypeStruct((t // tm, v_rows, tm), BF16),
                   jax.ShapeDtypeStruct((t // tm, heads, tm), F32)],
        scratch_shapes=scratch,
        compiler_params=_params(("arbitrary",), vmem),
        name="mixer_in",
    )(x, *weight_args)


def _attn_kernel(qaug_ref, kaug_ref, vaug_ref, crow_ref, o_ref, s0_ref, s1_ref, m_ref, acc_ref):
    tq = qaug_ref.shape[2]
    heads = m_ref.shape[0]
    g = pl.program_id(1)
    ia, ib = 2 * g, 2 * g + 1
    head_cols = [slice(h * HEAD_DIM, (h + 1) * HEAD_DIM) for h in range(heads)]
    qk_cols = [slice(h * 2 * HEAD_DIM, (h + 1) * 2 * HEAD_DIM) for h in range(heads)]
    v_rows = [slice(h * (HEAD_DIM + ONES_ROWS), (h + 1) * (HEAD_DIM + ONES_ROWS)) for h in range(heads)]
    key_pos = lax.broadcasted_iota(jnp.int32, (tq, tq), 0)
    query_pos = lax.broadcasted_iota(jnp.int32, (tq, tq), 1)
    causal = key_pos <= query_pos

    s_refs = (s0_ref, s1_ref)

    def reset():
        m_ref[...] = jnp.full(m_ref.shape, MASKED, F32)
        acc_ref[...] = jnp.zeros(acc_ref.shape, F32)

    def logits(qb, j, slot, diagonal):
        rows = pl.ds(pl.multiple_of(j * tq, tq), tq)
        for h in range(heads):
            s = jnp.dot(kaug_ref[rows, qk_cols[h]], qaug_ref[qb, qk_cols[h], :], preferred_element_type=F32)
            s_refs[slot][h] = jnp.where(causal, s, MASKED) if diagonal else s

    def softmax_pv(i, j, slot):
        probs, rescale = [], []
        for h in range(heads):
            s = s_refs[slot][h]
            cq = crow_ref[i, h:h + 1, :]
            m_old = m_ref[h]
            block_max = jnp.max(functools.reduce(jnp.maximum, _row_groups(s)), axis=0, keepdims=True)
            m_new = jnp.maximum(m_old, block_max + cq)
            m_ref[h] = m_new
            probs.append(jnp.exp2(s + (cq - m_new)).astype(BF16))
            rescale.append(jnp.exp2(m_old - m_new))
        for h in range(heads):
            acc_ref[h] = rescale[h] * acc_ref[h] + jnp.dot(vaug_ref[j, v_rows[h], :], probs[h],
                                                           preferred_element_type=F32)

    def finish(qb):
        for h, hs in enumerate(head_cols):
            row_sum = acc_ref[h, HEAD_DIM:HEAD_DIM + 1, :]
            o_ref[qb * tq:(qb + 1) * tq, hs] = (acc_ref[h, 0:HEAD_DIM, :] * (1.0 / row_sum)).T.astype(BF16)

    def stage(qb_next, j_next, slot_next, diagonal, i, j, slot):
        logits(qb_next, j_next, slot_next, diagonal)
        softmax_pv(i, j, slot)

    reset()

    @pl.when(g == 0)
    def _():
        logits(0, 0, 0, True)
        stage(1, 0, 1, False, ia, 0, 0)
        finish(0)
        reset()
        stage(1, 1, 0, True, ib, 0, 1)
        softmax_pv(ib, 1, 0)
        finish(1)

    @pl.when(g > 0)
    def _():
        logits(0, 0, 0, False)

        def pair_a(t, carry):
            j = 2 * t
            stage(0, j + 1, 1, False, ia, j, 0)
            stage(0, j + 2, 0, False, ia, j + 1, 1)
            return carry

        lax.fori_loop(0, g - 1, pair_a, 0)
        stage(0, ia - 1, 1, False, ia, ia - 2, 0)
        stage(0, ia, 0, True, ia, ia - 1, 1)
        stage(1, 0, 1, False, ia, ia, 0)
        finish(0)
        reset()

        def pair_b(t, carry):
            j = 2 * t
            stage(1, j + 1, 0, False, ib, j, 1)
            stage(1, j + 2, 1, False, ib, j + 1, 0)
            return carry

        lax.fori_loop(0, g, pair_b, 0)
        stage(1, ib, 0, True, ib, ia, 1)
        softmax_pv(ib, ib, 0)
        finish(1)


def _attention(qaug, kaug, vaug, c_rows, *, batch, seq_len):
    nt, qk_rows, tq = qaug.shape
    nq = seq_len // tq
    heads = qk_rows // (2 * HEAD_DIM)
    d = heads * HEAD_DIM
    v_rows = vaug.shape[1]
    assert nt == batch * nq and tq % V7X_LANES == 0 and v_rows == heads * (HEAD_DIM + ONES_ROWS)
    assert nq % 2 == 0
    steps = nq // 2
    scratch = [pltpu.VMEM((heads, tq, tq), F32), pltpu.VMEM((heads, tq, tq), F32),
               pltpu.VMEM((heads, 1, tq), F32),
               pltpu.VMEM((heads, HEAD_DIM + ONES_ROWS, tq), F32)]
    vmem = (2 * (_nbytes((seq_len, qk_rows), BF16) + _nbytes((nq, v_rows, tq), BF16)
                 + 2 * _nbytes((qk_rows, tq), BF16) + 2 * _nbytes((tq, d), BF16)
                 + _nbytes((nq, V7X_SUBLANES, tq), F32))
            + sum(_nbytes(s.shape, s.dtype) for s in scratch)
            + _nbytes((V7X_SUBLANES * heads, tq), F32)
            + 16 * _nbytes((tq, tq), F32))
    return pl.pallas_call(
        _attn_kernel,
        grid=(batch, steps),
        in_specs=[pl.BlockSpec((2, qk_rows, tq), lambda b, g: (b * steps + g, 0, 0)),
                  pl.BlockSpec((seq_len, qk_rows), lambda b, g: (b, 0)),
                  pl.BlockSpec((nq, v_rows, tq), lambda b, g: (b, 0, 0)),
                  pl.BlockSpec((nq, heads, tq), lambda b, g: (b, 0, 0))],
        out_specs=pl.BlockSpec((2 * tq, d), lambda b, g: (b * steps + g, 0)),
        out_shape=jax.ShapeDtypeStruct((nt * tq, d), BF16),
        scratch_shapes=scratch,
        compiler_params=_params(("arbitrary", "arbitrary"), vmem),
        name="attention",
    )(qaug, kaug, vaug, c_rows)


def _mixer_out_kernel(x_ref, a_ref, part_ref, g2_ref, woa32_ref, wo32_ref, g_ref, wgu32_ref, wd32_ref,
                      o_ref, woa_ref, wo_ref, wgu_ref, wd_ref, act_ref, *, n_load):
    step = pl.program_id(0)

    @pl.when(step < n_load)
    def _():
        _stash_rows(woa32_ref, woa_ref, step)
        _stash_rows(wo32_ref, wo_ref, step)
        _stash_rows(wgu32_ref, wgu_ref, step)
        _stash_rows(wd32_ref, wd_ref, step)

    @pl.when(step >= n_load)
    def _():
        yc = jnp.dot(a_ref[...], woa_ref[...], preferred_element_type=F32)
        merged = part_ref[...] + g2_ref[...] * yc
        x = x_ref[...] + jnp.dot(merged.astype(BF16), wo_ref[...], preferred_element_type=F32)
        o_ref[...] = _swiglu_half_step(x, g_ref, wgu_ref, wd_ref, act_ref)


def _mixer_out(x, attn, part, g2, w_mix, w_ffn, layer):
    t, d = x.shape
    d_ff = w_ffn["w_down"].shape[1]
    tm = TOKEN_TILE_OUT
    n_load = LOAD_STEPS_FFN
    assert t % tm == 0
    tile = _token_tile((tm, d), n_load)
    vmem = (_ffn_vmem(tm, d, d_ff, n_load) + 2 * _nbytes((d, d), BF16) + 4 * _nbytes((d, d), F32) // n_load
            + 2 * (4 * _nbytes((tm, d), F32) + _nbytes((tm, d), BF16)) + 2 * _nbytes((tm, d), F32))
    return pl.pallas_call(
        functools.partial(_mixer_out_kernel, n_load=n_load),
        grid=(n_load + t // tm,),
        in_specs=[tile, tile, tile, tile,
                  _weight_chunk(w_mix["w_out_attn"], layer, n_load), _weight_chunk(w_mix["w_o"], layer, n_load)]
        + _ffn_weight_specs(w_ffn, layer, n_load),
        out_specs=tile,
        out_shape=jax.ShapeDtypeStruct((t, d), F32),
        scratch_shapes=[pltpu.VMEM((d, d), BF16), pltpu.VMEM((d, d), BF16)] + _ffn_scratch(tm, w_ffn),
        compiler_params=_params(("arbitrary",), vmem),
        name="mixer_out",
    )(x, attn, part, g2, w_mix["w_out_attn"], w_mix["w_o"], *_ffn_weight_args(w_ffn))


def kernel(x, ffn1_norm, ffn1_w_gu, ffn1_w_down, mix_norm, w_in, b_forget, b_gate, conv_w, sgu_ln_g, sgu_ln_b,
           sgu_w, sgu_b, q_norm_g, k_norm_g, w_out_conv, w_out_sgu, w_out_attn, w_o, ffn2_norm, ffn2_w_gu,
           ffn2_w_down):
    batch, seq_len, d = x.shape
    depth = w_in.shape[0]
    heads = d // HEAD_DIM
    rows = lambda a: a.reshape(depth, 1, -1)

    ffn1 = {"norm": rows(ffn1_norm), "w_gu": ffn1_w_gu, "w_down": ffn1_w_down}
    ffn2 = {"norm": rows(ffn2_norm), "w_gu": ffn2_w_gu, "w_down": ffn2_w_down}
    mix = {
        "mix_norm": rows(mix_norm),
        "w_in_t": jnp.swapaxes(w_in, 1, 2),
        "b_forget": jnp.pad(rows(b_forget), ((0, 0), (0, 0), (0, V7X_LANES - heads))),
        "b_gate": rows(b_gate),
        "conv_w": conv_w,
        "sgu_ln_g": rows(sgu_ln_g),
        "sgu_ln_b": rows(sgu_ln_b),
        "sgu_w": sgu_w,
        "sgu_b_t": jnp.swapaxes(sgu_b, 1, 2),
        "q_norm_g": rows(q_norm_g),
        "k_norm_g": rows(k_norm_g),
        "w_out_conv": w_out_conv,
        "w_out_sgu": w_out_sgu,
        "w_out_attn": w_out_attn,
        "w_o": w_o,
    }

    xt = x.reshape(batch * seq_len, d)
    for layer in range(depth):
        xt = _ffn(xt, ffn1, layer)
        part, g2, qaug, kaug, vaug, c_rows = _mixer_in(xt, mix, layer, seq_len=seq_len)
        attn = _attention(qaug, kaug, vaug, c_rows, batch=batch, seq_len=seq_len)
        xt = _mixer_out(xt, attn, part, g2, mix, ffn2, layer)
    return xt.reshape(batch, seq_len, d)
```

```python
import functools
import math

import jax
import jax.numpy as jnp
from jax import lax
from jax.experimental import pallas as pl
from jax.experimental.pallas import tpu as pltpu

F32 = jnp.float32
BF16 = jnp.bfloat16

RMS_EPS = 1e-6
LN_EPS = 1e-5
SGU_CHUNK = 128
HEAD_DIM = 128
N_BRANCH = 3
MASKED = -1e30
FORGET_SPLIT = 3
ONES_ROWS = 16

V7X_LANES = 128
V7X_SUBLANES = 8
BF16_SUBLANES = 16
V7X_VMEM_BYTES = 64 * 1024 * 1024
LOAD_STEPS = 16
LOAD_STEPS_FFN = 8
FF_CHUNK = 1024
GATE_CHUNK = 256

TOKEN_TILE_FFN = 1024
TOKEN_TILE_OUT = 512
SEQ_BLOCK = 256


def _layer(arr, layer):
    index = (layer,) + (0,) * (arr.ndim - 1)
    return pl.BlockSpec((None,) + tuple(arr.shape[1:]), lambda *_: index, pipeline_mode=pl.Buffered(1))


def _weight_chunk(arr, layer, n_load):
    rows = arr.shape[1] // n_load
    assert rows * n_load == arr.shape[1] and rows % BF16_SUBLANES == 0, (arr.shape, n_load)
    return pl.BlockSpec((None, rows, arr.shape[2]), lambda s: (layer, jnp.minimum(s, n_load - 1), 0))


def _token_tile(block, n_load):
    return pl.BlockSpec(block, lambda s: (jnp.maximum(s - n_load, 0),) + (0,) * (len(block) - 1))


def _stash_rows(src_ref, dst_ref, step):
    rows = src_ref.shape[0]
    dst_ref[pl.ds(pl.multiple_of(step * rows, rows), rows), :] = src_ref[...].astype(BF16)


def _nbytes(shape, dtype):
    return math.prod(shape) * jnp.dtype(dtype).itemsize


def _params(semantics, vmem_bytes):
    assert vmem_bytes <= V7X_VMEM_BYTES, vmem_bytes
    return pltpu.CompilerParams(dimension_semantics=semantics, vmem_limit_bytes=int(vmem_bytes))


def _rms_norm(x, g):
    return x * lax.rsqrt(jnp.mean(x * x, axis=-1, keepdims=True) + RMS_EPS) * g


def _gelu(x):
    return 0.5 * x * (1.0 + lax.erf(x * (2.0 ** -0.5)))


def _ff_chunks(d_ff):
    return [(c, min(c + FF_CHUNK, d_ff)) for c in range(0, d_ff, FF_CHUNK)]


def _row_groups(x):
    return [x[r:r + V7X_SUBLANES] for r in range(0, x.shape[0], V7X_SUBLANES)]


def _swiglu_half_step(x, g_ref, wgu_ref, wd_ref, act_ref):
    d_ff = wd_ref.shape[0]
    h = _rms_norm(x, g_ref[...]).astype(BF16)
    for c0, c1 in _ff_chunks(d_ff):
        g = jnp.dot(h, wgu_ref[:, c0:c1], preferred_element_type=F32)
        u = jnp.dot(h, wgu_ref[:, d_ff + c0:d_ff + c1], preferred_element_type=F32)
        act_ref[:, c0:c1] = (g * jax.nn.sigmoid(g) * u).astype(BF16)
    return x + 0.5 * jnp.dot(act_ref[...], wd_ref[...], preferred_element_type=F32)


def _ffn_weight_specs(w, layer, n_load):
    return [_layer(w["norm"], layer), _weight_chunk(w["w_gu"], layer, n_load),
            _weight_chunk(w["w_down"], layer, n_load)]


def _ffn_weight_args(w):
    return [w["norm"], w["w_gu"], w["w_down"]]


def _ffn_scratch(tm, w):
    d, d_gu = w["w_gu"].shape[1:]
    d_ff = w["w_down"].shape[1]
    return [pltpu.VMEM((d, d_gu), BF16), pltpu.VMEM((d_ff, d), BF16), pltpu.VMEM((tm, d_ff), BF16)]


def _ffn_vmem(tm, d, d_ff, n_load):
    return (_nbytes((d, 2 * d_ff), BF16) + _nbytes((d_ff, d), BF16)
            + 2 * (_nbytes((d, 2 * d_ff), F32) + _nbytes((d_ff, d), F32)) // n_load
            + _nbytes((tm, d_ff), BF16)
            + 4 * _nbytes((tm, FF_CHUNK), F32))


def _ffn_kernel(x_ref, g_ref, wgu32_ref, wd32_ref, o_ref, wgu_ref, wd_ref, act_ref, *, n_load):
    step = pl.program_id(0)

    @pl.when(step < n_load)
    def _():
        _stash_rows(wgu32_ref, wgu_ref, step)
        _stash_rows(wd32_ref, wd_ref, step)

    @pl.when(step >= n_load)
    def _():
        o_ref[...] = _swiglu_half_step(x_ref[...], g_ref, wgu_ref, wd_ref, act_ref)


def _ffn(x, w, layer):
    t, d = x.shape
    d_ff = w["w_down"].shape[1]
    tm = TOKEN_TILE_FFN
    n_load = LOAD_STEPS_FFN
    assert t % tm == 0
    tile = _token_tile((tm, d), n_load)
    vmem = _ffn_vmem(tm, d, d_ff, n_load) + 4 * _nbytes((tm, d), F32)
    return pl.pallas_call(
        functools.partial(_ffn_kernel, n_load=n_load),
        grid=(n_load + t // tm,),
        in_specs=[tile] + _ffn_weight_specs(w, layer, n_load),
        out_specs=tile,
        out_shape=jax.ShapeDtypeStruct((t, d), F32),
        scratch_shapes=_ffn_scratch(tm, w),
        compiler_params=_params(("arbitrary",), vmem),
        name="ffn",
    )(x, *_ffn_weight_args(w))


def _mixer_in_kernel(x_ref, ng_ref, wtm32_ref, wtf32_ref, wtg32_ref, woc32_ref, wos32_ref, bf_ref, bg_ref, cw_ref,
                     lng_ref, lnb_ref, sw_ref, sb_ref, qg_ref, kg_ref,
                     part_ref, g2_ref, qaug_ref, kaug_ref, vaug_ref, crow_ref,
                     wm_ref, wf_ref, wgt_ref, woc_ref, wos_ref, zs_ref, ccarry_ref, vn_ref, yb_ref, *,
                     n_load, tiles_per_seq, q_scale, c_scale):
    step = pl.program_id(0)

    @pl.when(step < n_load)
    def _():
        wm_ref[step] = wtm32_ref[...].T.astype(BF16)
        _stash_rows(woc32_ref, woc_ref, step)
        _stash_rows(wos32_ref, wos_ref, step)

    @pl.when(step < wgt_ref.shape[0])
    def _():
        wgt_ref[step] = wtg32_ref[...].T.astype(BF16)

    @pl.when(step == 0)
    def _():
        rows = wtf32_ref[...]
        slab = jnp.concatenate([rows, jnp.zeros((V7X_LANES - rows.shape[0], rows.shape[1]), F32)], axis=0)
        wf_ref[...] = slab.T.astype(BF16)

    @pl.when(step >= n_load)
    def _():
        _mixer_in_tile(step - n_load, x_ref, ng_ref, wm_ref, wf_ref, wgt_ref, bf_ref, bg_ref, cw_ref, lng_ref,
                       lnb_ref, sw_ref, sb_ref, qg_ref, kg_ref, woc_ref, wos_ref,
                       part_ref, g2_ref, qaug_ref, kaug_ref, vaug_ref, crow_ref,
                       zs_ref, ccarry_ref, vn_ref, yb_ref,
                       tiles_per_seq=tiles_per_seq, q_scale=q_scale, c_scale=c_scale)


def _mixer_in_tile(tile_idx, x_ref, ng_ref, wm_ref, wf_ref, wgt_ref, bf_ref, bg_ref, cw_ref, lng_ref, lnb_ref,
                   sw_ref, sb_ref, qg_ref, kg_ref, woc_ref, wos_ref,
                   part_ref, g2_ref, qaug_ref, kaug_ref, vaug_ref, crow_ref,
                   zs_ref, ccarry_ref, vn_ref, yb_ref, *, tiles_per_seq, q_scale, c_scale):
    tm, d = x_ref.shape
    heads = d // HEAD_DIM
    pad = V7X_SUBLANES

    @pl.when(tile_idx % tiles_per_seq == 0)
    def _():
        zs_ref[0:pad, :] = jnp.zeros((pad, d), F32)
        ccarry_ref[...] = jnp.zeros_like(ccarry_ref)

    h = _rms_norm(x_ref[...], ng_ref[...]).astype(BF16)

    def proj(w_ref, c0, c1):
        width = w_ref.shape[2]
        assert c0 % width == 0 and c1 % width == 0
        return jnp.concatenate([jnp.dot(h, w_ref[j], preferred_element_type=F32)
                                for j in range(c0 // width, c1 // width)], axis=1)

    pa = proj(wm_ref, 0, 3 * d)
    f = jnp.dot(h, wf_ref[...], preferred_element_type=F32) + bf_ref[...]
    ps = proj(wm_ref, 3 * d, 5 * d)
    pq = proj(wm_ref, 5 * d, 8 * d)

    zs_ref[pad:pad + tm, :] = pa[:, d:2 * d] * pa[:, 2 * d:3 * d]
    conv = (cw_ref[0:1, :] * zs_ref[pad - 2:pad - 2 + tm, :]
            + cw_ref[1:2, :] * zs_ref[pad - 1:pad - 1 + tm, :]
            + cw_ref[2:3, :] * zs_ref[pad:pad + tm, :])
    ya_in = (pa[:, 0:d] * conv).astype(BF16)
    zs_ref[0:pad, :] = zs_ref[tm:tm + pad, :]

    c = jnp.minimum(f, 0.0) - jnp.log1p(jnp.exp(-jnp.abs(f)))
    t_idx = lax.broadcasted_iota(jnp.int32, c.shape, 0)
    shift = 1
    while shift < tm:
        c = c + jnp.where(t_idx >= shift, pltpu.roll(c, shift, axis=0), 0.0)
        shift *= 2
    c = c + ccarry_ref[0:1, :]
    ccarry_ref[0:1, :] = c[tm - 1:tm, :]
    c = c * c_scale
    crow_ref[...] = c.T[0:heads, :]

    u = _gelu(ps[:, 0:d])
    vv = _gelu(ps[:, d:2 * d])
    mu = jnp.mean(vv, axis=-1, keepdims=True)
    vc = vv - mu
    var = jnp.mean(vc * vc, axis=-1, keepdims=True)
    vn_ref[...] = (vc * lax.rsqrt(var + LN_EPS) * lng_ref[...] + lnb_ref[...]).astype(BF16)

    lane = lax.broadcasted_iota(jnp.int32, (tm, HEAD_DIM), 1)
    feature = lax.broadcasted_iota(jnp.int32, (HEAD_DIM, tm), 0)
    ones_rows = jnp.where(feature < FORGET_SPLIT, 1.0, 0.0).astype(BF16)
    pieces, rest = [], -c
    for _ in range(FORGET_SPLIT):
        pieces.append(rest.astype(BF16).astype(F32))
        rest = rest - pieces[-1]
    qn = []
    for g in range(heads):
        hs = slice(g * HEAD_DIM, (g + 1) * HEAD_DIM)
        qn.append(_rms_norm(pq[:, hs], qg_ref[:, hs]) * q_scale)
        ks = slice(d + g * HEAD_DIM, d + (g + 1) * HEAD_DIM)
        kaug_ref[:, 2 * g * HEAD_DIM:(2 * g + 1) * HEAD_DIM] = _rms_norm(pq[:, ks], kg_ref[:, hs]).astype(BF16)
        slab = jnp.zeros((tm, HEAD_DIM), F32)
        for term, piece in enumerate(pieces):
            slab = jnp.where(lane == term, piece[:, g:g + 1], slab)
        kaug_ref[:, (2 * g + 1) * HEAD_DIM:(2 * g + 2) * HEAD_DIM] = slab.astype(BF16)
    qt = jnp.concatenate(qn, axis=1).T.astype(BF16)
    vt = pq[:, 2 * d:3 * d].T.astype(BF16)
    v_rows = HEAD_DIM + ONES_ROWS
    for g in range(heads):
        hs = slice(g * HEAD_DIM, (g + 1) * HEAD_DIM)
        qaug_ref[2 * g * HEAD_DIM:(2 * g + 1) * HEAD_DIM, :] = qt[hs, :]
        qaug_ref[(2 * g + 1) * HEAD_DIM:(2 * g + 2) * HEAD_DIM, :] = ones_rows
        vaug_ref[g * v_rows:g * v_rows + HEAD_DIM, :] = vt[hs, :]
        vaug_ref[g * v_rows + HEAD_DIM:(g + 1) * v_rows, :] = jnp.ones((ONES_ROWS, tm), BF16)

    ya = jnp.dot(ya_in, woc_ref[...], preferred_element_type=F32)
    n_chunks = tm // SGU_CHUNK
    pos_t = lax.broadcasted_iota(jnp.int32, (SGU_CHUNK, SGU_CHUNK), 0)
    pos_s = lax.broadcasted_iota(jnp.int32, (SGU_CHUNK, SGU_CHUNK), 1)
    for g in range(heads):
        hs = slice(g * HEAD_DIM, (g + 1) * HEAD_DIM)
        w = jnp.where(pos_s <= pos_t, sw_ref[g], 0.0).astype(BF16)
        rhs = jnp.concatenate([vn_ref[c0 * SGU_CHUNK:(c0 + 1) * SGU_CHUNK, hs] for c0 in range(n_chunks)], axis=1)
        s = jnp.dot(w, rhs, preferred_element_type=F32) + sb_ref[:, g:g + 1]
        for c0 in range(n_chunks):
            rows = slice(c0 * SGU_CHUNK, (c0 + 1) * SGU_CHUNK)
            yb_ref[rows, hs] = (u[rows, hs] * s[:, c0 * SGU_CHUNK:(c0 + 1) * SGU_CHUNK]).astype(BF16)
    gates = [jax.nn.sigmoid(proj(wgt_ref, b * d, (b + 1) * d) + bg_ref[:, b * d:(b + 1) * d])
             for b in range(N_BRANCH)]
    yb = jnp.dot(yb_ref[...], wos_ref[...], preferred_element_type=F32)
    part_ref[...] = gates[0] * ya + gates[1] * yb
    g2_ref[...] = gates[2]


def _mixer_in(x, w, layer, *, seq_len):
    t, d = x.shape
    tm = SEQ_BLOCK
    heads = d // HEAD_DIM
    assert seq_len % tm == 0 and tm % SGU_CHUNK == 0 and d % HEAD_DIM == 0 and heads <= V7X_SUBLANES
    n_load = LOAD_STEPS
    w_t = w["w_in_t"]
    n_in = w_t.shape[1]
    n_main = 8 * d
    main_rows = n_main // n_load
    n_gate = N_BRANCH * d // GATE_CHUNK
    assert n_in == n_main + heads + N_BRANCH * d and main_rows * n_load == n_main and main_rows % V7X_LANES == 0
    assert n_gate * GATE_CHUNK == N_BRANCH * d and n_gate <= n_load and d % GATE_CHUNK == 0
    kernel = functools.partial(
        _mixer_in_kernel, n_load=n_load, tiles_per_seq=seq_len // tm,
        q_scale=HEAD_DIM ** -0.5 * math.log2(math.e), c_scale=math.log2(math.e))
    main_spec = pl.BlockSpec((None, main_rows, d), lambda s: (layer, jnp.minimum(s, n_load - 1), 0))
    forget_spec = pl.BlockSpec((pl.Squeezed(), pl.Element(V7X_SUBLANES), pl.Element(d)),
                               lambda s: (layer, n_main, 0))
    gate_align = math.gcd(n_main + heads, GATE_CHUNK, V7X_SUBLANES)
    gate_spec = pl.BlockSpec(
        (pl.Squeezed(), pl.Element(GATE_CHUNK), pl.Element(d)),
        lambda s: (layer, pl.multiple_of(n_main + heads + GATE_CHUNK * jnp.minimum(s, n_gate - 1), gate_align), 0))
    streamed = ["w_out_conv", "w_out_sgu"]
    small = ["b_forget", "b_gate", "conv_w", "sgu_ln_g", "sgu_ln_b", "sgu_w", "sgu_b_t", "q_norm_g", "k_norm_g"]
    weight_specs = ([_layer(w["mix_norm"], layer), main_spec, forget_spec, gate_spec]
                    + [_weight_chunk(w[n], layer, n_load) for n in streamed] + [_layer(w[n], layer) for n in small])
    weight_args = [w["mix_norm"], w_t, w_t, w_t] + [w[n] for n in streamed] + [w[n] for n in small]
    v_rows = heads * (HEAD_DIM + ONES_ROWS)
    tile = _token_tile((tm, d), n_load)
    resident = [pltpu.VMEM((n_load, d, main_rows), BF16), pltpu.VMEM((d, V7X_LANES), BF16),
                pltpu.VMEM((n_gate, d, GATE_CHUNK), BF16), pltpu.VMEM((d, d), BF16), pltpu.VMEM((d, d), BF16)]
    scratch = resident + [pltpu.VMEM((tm + V7X_SUBLANES, d), F32), pltpu.VMEM((V7X_SUBLANES, V7X_LANES), F32),
                          pltpu.VMEM((tm, d), BF16), pltpu.VMEM((tm, d), BF16)]
    vmem = (sum(_nbytes(s.shape, s.dtype) for s in scratch)
            + 2 * _nbytes((main_rows + GATE_CHUNK + V7X_SUBLANES + 2 * d // n_load, d), F32)
            + 2 * _nbytes((main_rows, d), F32)
            + sum(_nbytes(w[n].shape[1:], w[n].dtype) for n in small)
            + 2 * (3 * _nbytes((tm, d), F32) + 4 * _nbytes((tm, d), BF16) + _nbytes((v_rows, tm), BF16)
                   + _nbytes((V7X_SUBLANES, tm), F32))
            + 10 * _nbytes((tm, d), F32))
    return pl.pallas_call(
        kernel,
        grid=(n_load + t // tm,),
        in_specs=[tile] + weight_specs,
        out_specs=[tile, tile,
                   _token_tile((None, 2 * d, tm), n_load),
                   _token_tile((tm, 2 * d), n_load),
                   _token_tile((None, v_rows, tm), n_load),
                   _token_tile((None, heads, tm), n_load)],
        out_shape=[jax.ShapeDtypeStruct((t, d), F32), jax.ShapeDtypeStruct((t, d), F32),
                   jax.ShapeDtypeStruct((t // tm, 2 * d, tm), BF16), jax.ShapeDtypeStruct((t, 2 * d), BF16),
                   jax.ShapeDtypeStruct((t // tm, v_rows, tm), BF16),
                   jax.ShapeDtypeStruct((t // tm, heads, tm), F32)],
        scratch_shapes=scratch,
        compiler_params=_params(("arbitrary",), vmem),
        name="mixer_in",
    )(x, *weight_args)


def _attn_kernel(qaug_ref, kaug_ref, vaug_ref, crow_ref, o_ref, s0_ref, s1_ref, m_ref, acc_ref):
    tq = qaug_ref.shape[2]
    heads = m_ref.shape[0]
    g = pl.program_id(1)
    ia, ib = 2 * g, 2 * g + 1
    head_cols = [slice(h * HEAD_DIM, (h + 1) * HEAD_DIM) for h in range(heads)]
    qk_cols = [slice(h * 2 * HEAD_DIM, (h + 1) * 2 * HEAD_DIM) for h in range(heads)]
    v_rows = [slice(h * (HEAD_DIM + ONES_ROWS), (h + 1) * (HEAD_DIM + ONES_ROWS)) for h in range(heads)]
    key_pos = lax.broadcasted_iota(jnp.int32, (tq, tq), 0)
    query_pos = lax.broadcasted_iota(jnp.int32, (tq, tq), 1)
    causal = key_pos <= query_pos

    s_refs = (s0_ref, s1_ref)

    def reset():
        m_ref[...] = jnp.full(m_ref.shape, MASKED, F32)
        acc_ref[...] = jnp.zeros(acc_ref.shape, F32)

    def logits(qb, j, slot, diagonal, which=None):
        rows = pl.ds(pl.multiple_of(j * tq, tq), tq)
        for h in (range(heads) if which is None else which):
            s = jnp.dot(kaug_ref[rows, qk_cols[h]], qaug_ref[qb, qk_cols[h], :], preferred_element_type=F32)
            s_refs[slot][h] = jnp.where(causal, s, MASKED) if diagonal else s

    def softmax_pv(i, j, slot, which=None):
        which = list(range(heads) if which is None else which)
        probs, rescale = {}, {}
        for h in which:
            s = s_refs[slot][h]
            cq = crow_ref[i, h:h + 1, :]
            m_old = m_ref[h]
            block_max = jnp.max(functools.reduce(jnp.maximum, _row_groups(s)), axis=0, keepdims=True)
            m_new = jnp.maximum(m_old, block_max + cq)
            m_ref[h] = m_new
            probs[h] = jnp.exp2(s + (cq - m_new)).astype(BF16)
            rescale[h] = jnp.exp2(m_old - m_new)
        for h in which:
            acc_ref[h] = rescale[h] * acc_ref[h] + jnp.dot(vaug_ref[j, v_rows[h], :], probs[h],
                                                           preferred_element_type=F32)

    def finish(qb):
        for h, hs in enumerate(head_cols):
            row_sum = acc_ref[h, HEAD_DIM:HEAD_DIM + 1, :]
            o_ref[qb * tq:(qb + 1) * tq, hs] = (acc_ref[h, 0:HEAD_DIM, :] * (1.0 / row_sum)).T.astype(BF16)

    def stage(qb_next, j_next, slot_next, diagonal, i, j, slot):
        for h in range(heads):
            logits(qb_next, j_next, slot_next, diagonal, [h])
            softmax_pv(i, j, slot, [h])

    reset()

    @pl.when(g == 0)
    def _():
        logits(0, 0, 0, True)
        stage(1, 0, 1, False, ia, 0, 0)
        finish(0)
        reset()
        stage(1, 1, 0, True, ib, 0, 1)
        softmax_pv(ib, 1, 0)
        finish(1)

    @pl.when(g > 0)
    def _():
        logits(0, 0, 0, False)

        def pair_a(t, carry):
            j = 2 * t
            stage(0, j + 1, 1, False, ia, j, 0)
            stage(0, j + 2, 0, False, ia, j + 1, 1)
            return carry

        lax.fori_loop(0, g - 1, pair_a, 0)
        stage(0, ia - 1, 1, False, ia, ia - 2, 0)
        stage(0, ia, 0, True, ia, ia - 1, 1)
        stage(1, 0, 1, False, ia, ia, 0)
        finish(0)
        reset()

        def pair_b(t, carry):
            j = 2 * t
            stage(1, j + 1, 0, False, ib, j, 1)
            stage(1, j + 2, 1, False, ib, j + 1, 0)
            return carry

        lax.fori_loop(0, g, pair_b, 0)
        stage(1, ib, 0, True, ib, ia, 1)
        softmax_pv(ib, ib, 0)
        finish(1)


def _attention(qaug, kaug, vaug, c_rows, *, batch, seq_len):
    nt, qk_rows, tq = qaug.shape
    nq = seq_len // tq
    heads = qk_rows // (2 * HEAD_DIM)
    d = heads * HEAD_DIM
    v_rows = vaug.shape[1]
    assert nt == batch * nq and tq % V7X_LANES == 0 and v_rows == heads * (HEAD_DIM + ONES_ROWS)
    assert nq % 2 == 0
    steps = nq // 2
    scratch = [pltpu.VMEM((heads, tq, tq), F32), pltpu.VMEM((heads, tq, tq), F32),
               pltpu.VMEM((heads, 1, tq), F32),
               pltpu.VMEM((heads, HEAD_DIM + ONES_ROWS, tq), F32)]
    vmem = (2 * (_nbytes((seq_len, qk_rows), BF16) + _nbytes((nq, v_rows, tq), BF16)
                 + 2 * _nbytes((qk_rows, tq), BF16) + 2 * _nbytes((tq, d), BF16)
                 + _nbytes((nq, V7X_SUBLANES, tq), F32))
            + sum(_nbytes(s.shape, s.dtype) for s in scratch)
            + _nbytes((V7X_SUBLANES * heads, tq), F32)
            + 16 * _nbytes((tq, tq), F32))
    return pl.pallas_call(
        _attn_kernel,
        grid=(batch, steps),
        in_specs=[pl.BlockSpec((2, qk_rows, tq), lambda b, g: (b * steps + g, 0, 0)),
                  pl.BlockSpec((seq_len, qk_rows), lambda b, g: (b, 0)),
                  pl.BlockSpec((nq, v_rows, tq), lambda b, g: (b, 0, 0)),
                  pl.BlockSpec((nq, heads, tq), lambda b, g: (b, 0, 0))],
        out_specs=pl.BlockSpec((2 * tq, d), lambda b, g: (b * steps + g, 0)),
        out_shape=jax.ShapeDtypeStruct((nt * tq, d), BF16),
        scratch_shapes=scratch,
        compiler_params=_params(("arbitrary", "arbitrary"), vmem),
        name="attention",
    )(qaug, kaug, vaug, c_rows)


def _mixer_out_kernel(x_ref, a_ref, part_ref, g2_ref, woa32_ref, wo32_ref, g_ref, wgu32_ref, wd32_ref,
                      o_ref, woa_ref, wo_ref, wgu_ref, wd_ref, act_ref, *, n_load):
    step = pl.program_id(0)

    @pl.when(step < n_load)
    def _():
        _stash_rows(woa32_ref, woa_ref, step)
        _stash_rows(wo32_ref, wo_ref, step)
        _stash_rows(wgu32_ref, wgu_ref, step)
        _stash_rows(wd32_ref, wd_ref, step)

    @pl.when(step >= n_load)
    def _():
        yc = jnp.dot(a_ref[...], woa_ref[...], preferred_element_type=F32)
        merged = part_ref[...] + g2_ref[...] * yc
        x = x_ref[...] + jnp.dot(merged.astype(BF16), wo_ref[...], preferred_element_type=F32)
        o_ref[...] = _swiglu_half_step(x, g_ref, wgu_ref, wd_ref, act_ref)


def _mixer_out(x, attn, part, g2, w_mix, w_ffn, layer):
    t, d = x.shape
    d_ff = w_ffn["w_down"].shape[1]
    tm = TOKEN_TILE_OUT
    n_load = LOAD_STEPS_FFN
    assert t % tm == 0
    tile = _token_tile((tm, d), n_load)
    vmem = (_ffn_vmem(tm, d, d_ff, n_load) + 2 * _nbytes((d, d), BF16) + 4 * _nbytes((d, d), F32) // n_load
            + 2 * (4 * _nbytes((tm, d), F32) + _nbytes((tm, d), BF16)) + 2 * _nbytes((tm, d), F32))
    return pl.pallas_call(
        functools.partial(_mixer_out_kernel, n_load=n_load),
        grid=(n_load + t // tm,),
        in_specs=[tile, tile, tile, tile,
                  _weight_chunk(w_mix["w_out_attn"], layer, n_load), _weight_chunk(w_mix["w_o"], layer, n_load)]
        + _ffn_weight_specs(w_ffn, layer, n_load),
        out_specs=tile,
        out_shape=jax.ShapeDtypeStruct((t, d), F32),
        scratch_shapes=[pltpu.VMEM((d, d), BF16), pltpu.VMEM((d, d), BF16)] + _ffn_scratch(tm, w_ffn),
        compiler_params=_params(("arbitrary",), vmem),
        name="mixer_out",
    )(x, attn, part, g2, w_mix["w_out_attn"], w_mix["w_o"], *_ffn_weight_args(w_ffn))


def kernel(x, ffn1_norm, ffn1_w_gu, ffn1_w_down, mix_norm, w_in, b_forget, b_gate, conv_w, sgu_ln_g, sgu_ln_b,
           sgu_w, sgu_b, q_norm_g, k_norm_g, w_out_conv, w_out_sgu, w_out_attn, w_o, ffn2_norm, ffn2_w_gu,
           ffn2_w_down):
    batch, seq_len, d = x.shape
    depth = w_in.shape[0]
    heads = d // HEAD_DIM
    rows = lambda a: a.reshape(depth, 1, -1)

    ffn1 = {"norm": rows(ffn1_norm), "w_gu": ffn1_w_gu, "w_down": ffn1_w_down}
    ffn2 = {"norm": rows(ffn2_norm), "w_gu": ffn2_w_gu, "w_down": ffn2_w_down}
    mix = {
        "mix_norm": rows(mix_norm),
        "w_in_t": jnp.swapaxes(w_in, 1, 2),
        "b_forget": jnp.pad(rows(b_forget), ((0, 0), (0, 0), (0, V7X_LANES - heads))),
        "b_gate": rows(b_gate),
        "conv_w": conv_w,
        "sgu_ln_g": rows(sgu_ln_g),
        "sgu_ln_b": rows(sgu_ln_b),
        "sgu_w": sgu_w,
        "sgu_b_t": jnp.swapaxes(sgu_b, 1, 2),
        "q_norm_g": rows(q_norm_g),
        "k_norm_g": rows(k_norm_g),
        "w_out_conv": w_out_conv,
        "w_out_sgu": w_out_sgu,
        "w_out_attn": w_out_attn,
        "w_o": w_o,
    }

    xt = x.reshape(batch * seq_len, d)
    for layer in range(depth):
        xt = _ffn(xt, ffn1, layer)
        part, g2, qaug, kaug, vaug, c_rows = _mixer_in(xt, mix, layer, seq_len=seq_len)
        attn = _attention(qaug, kaug, vaug, c_rows, batch=batch, seq_len=seq_len)
        xt = _mixer_out(xt, attn, part, g2, mix, ffn2, layer)
    return xt.reshape(batch, seq_len, d)
```
